```python
import math, functools
import jax, jax.numpy as jnp
from jax import lax
import numpy as np

D_MODEL = 1024
BATCH = 8
SEQ = 4096
DEPTH = 2
DEC_BATCH = 32
DEC_SEQ = 4
PAST_LEN = 16384
PAGE_SIZE = 128

F32 = jnp.float32
HEAD_DIM = 64
MIX_WIDTH = D_MODEL
W_A = 3 * MIX_WIDTH // 8
W_B = MIX_WIDTH // 4
W_C = MIX_WIDTH - W_A - W_B
H_A = W_A // HEAD_DIM
H_B = W_B // HEAD_DIM
H_C = W_C // HEAD_DIM
DB = HEAD_DIM // 2
GDN_CONV = 4
GDN_CHUNK = 64
DECAY_LORA = 64
AAA_LORA = 64
GATE_LORA = 128
RWKV_GN_EPS = 64e-5
MEM_TOKENS = 256
MEM_HEADS = 4
MEM_HD = 128
MEM_INNER = MEM_HEADS * MEM_HD
D_FF = 2816
FFN_CONV = 3
Q_BLOCK = 128
NORM_EPS = 1e-6
A_COLS = 4 * W_A + 2 * H_A
B_COLS = 3 * W_B
C_COLS = 3 * W_C + DECAY_LORA + AAA_LORA + GATE_LORA
N_IN_COLS = A_COLS + B_COLS + C_COLS

kernel_name = "hymba_gdn_diffattn_rwkv7_convglu_decode_step"


def rmsnorm(x, g, eps=NORM_EPS):
    xf = x.astype(F32)
    y = xf * lax.rsqrt(jnp.mean(xf * xf, axis=-1, keepdims=True) + eps)
    return (y * g.astype(F32)).astype(x.dtype)


def causal_dwconv(x, buf, w, b=None):
    t = x.shape[1]
    width = w.shape[0]
    xp = jnp.concatenate([buf.astype(x.dtype), x], axis=1)
    y = xp[:, 0:t] * w[0]
    for j in range(1, width):
        y = y + xp[:, j:j + t] * w[j]
    if b is not None:
        y = y + b
    return y, xp[:, xp.shape[1] - (width - 1):]


def gated_delta_rule(q, k, v, beta, g, s0):
    bsz, t, h, _ = q.shape
    dv = v.shape[-1]
    c = math.gcd(t, GDN_CHUNK)
    n = t // c

    def blocks(a):
        return jnp.swapaxes(a.astype(F32).reshape((bsz, n, c) + a.shape[2:]), 2, 3)

    q, k, v, beta, g = map(blocks, (q, k, v, beta, g))
    gam = jnp.cumsum(g, axis=-1)
    diff = gam[..., :, None] - gam[..., None, :]
    strict = jnp.tril(jnp.ones((c, c), bool), -1)
    incl = jnp.tril(jnp.ones((c, c), bool))
    kb = k * beta[..., None]
    a_mat = jnp.einsum('bnhid,bnhjd->bnhij', kb, k) * jnp.exp(jnp.where(strict, diff, -jnp.inf))
    m = a_mat + jnp.eye(c, dtype=F32)
    u = lax.linalg.triangular_solve(m, v * beta[..., None], left_side=True, lower=True, unit_diagonal=True)
    w = lax.linalg.triangular_solve(m, kb * jnp.exp(gam)[..., None], left_side=True, lower=True, unit_diagonal=True)
    qk = jnp.einsum('bnhid,bnhjd->bnhij', q, k) * jnp.exp(jnp.where(incl, diff, -jnp.inf))
    q_dec = q * jnp.exp(gam)[..., None]
    k_dec = k * jnp.exp(gam[..., -1:] - gam)[..., None]
    g_last = jnp.exp(gam[..., -1])

    def step(s, xs):
        q_c, k_c, u_c, w_c, qk_c, gl = xs
        v_new = u_c - jnp.einsum('bhcd,bhde->bhce', w_c, s)
        o = jnp.einsum('bhcd,bhde->bhce', q_c, s) + jnp.einsum('bhij,bhje->bhie', qk_c, v_new)
        s = s * gl[..., None, None] + jnp.einsum('bhcd,bhce->bhde', k_c, v_new)
        return s, o

    xs = tuple(jnp.moveaxis(a, 1, 0) for a in (q_dec, k_dec, u, w, qk, g_last))
    s, o = lax.scan(step, s0.astype(F32), xs)
    o = jnp.swapaxes(jnp.moveaxis(o, 0, 1), 2, 3).reshape(bsz, t, h, dv)
    return o, s


def gdn_mixer(cols, conv_buf, s0, conv_w, a_log, dt_bias, norm_g):
    bsz, t, _ = cols.shape
    qkv_pre, z, b_logit, a_logit = jnp.split(cols, [3 * W_A, 4 * W_A, 4 * W_A + H_A], axis=-1)
    qkv, new_buf = causal_dwconv(qkv_pre, conv_buf, conv_w)
    qkv = jax.nn.silu(qkv).astype(F32)
    q, k, v = [a.reshape(bsz, t, H_A, HEAD_DIM) for a in jnp.split(qkv, 3, axis=-1)]
    q = q * lax.rsqrt(jnp.sum(q * q, axis=-1, keepdims=True) + 1e-6) * (HEAD_DIM ** -0.5)
    k = k * lax.rsqrt(jnp.sum(k * k, axis=-1, keepdims=True) + 1e-6)
    beta = jax.nn.sigmoid(b_logit.astype(F32))
    g = -jnp.exp(a_log.astype(F32)) * jax.nn.softplus(a_logit.astype(F32) + dt_bias.astype(F32))
    o, s = gated_delta_rule(q, k, v, beta, g, s0)
    o = rmsnorm(o, norm_g) * jax.nn.silu(z.astype(F32).reshape(bsz, t, H_A, HEAD_DIM))
    return o.reshape(bsz, t, W_A).astype(cols.dtype), new_buf, s.astype(cols.dtype)


def diff_lambda(lq1, lk1, lq2, lk2, lam_init):
    return (jnp.exp(jnp.sum(lq1.astype(F32) * lk1.astype(F32)))
            - jnp.exp(jnp.sum(lq2.astype(F32) * lk2.astype(F32))) + lam_init)


def diff_weights(s, lam):
    p = jax.nn.softmax(s, axis=-1)
    return p[:, :, 0] - lam * p[:, :, 1]


def diff_attn_prompt(q, k, v, lam):
    bsz, t = q.shape[:2]
    nb = t // Q_BLOCK
    kf = k.astype(F32)
    vf = v.astype(F32)
    qb = jnp.moveaxis(q.astype(F32).reshape(bsz, nb, Q_BLOCK, H_B, 2, DB), 1, 0)
    kpos = jnp.arange(t)

    def block(args):
        q_i, i = args
        s = jnp.einsum('bqhmd,bkhmd->bhmqk', q_i, kf) * (DB ** -0.5)
        qpos = i * Q_BLOCK + jnp.arange(Q_BLOCK)
        s = jnp.where(kpos[None, :] <= qpos[:, None], s, -jnp.inf)
        return jnp.einsum('bhqk,bkhe->bqhe', diff_weights(s, lam), vf)

    o = lax.map(block, (qb, jnp.arange(nb)))
    return jnp.moveaxis(o, 0, 1).reshape(bsz, t, H_B, 2 * DB)


def diff_attn_paged(q, k, v, lam, k_pool, v_pool, page_table):
    bsz, t = q.shape[:2]
    kp = k_pool[page_table].reshape(bsz, -1, H_B, 2, DB).astype(F32)
    vp = v_pool[page_table].reshape(bsz, -1, H_B, 2 * DB).astype(F32)
    past = kp.shape[1]
    qf = q.astype(F32)
    s_past = jnp.einsum('bqhmd,bkhmd->bhmqk', qf, kp)
    s_new = jnp.einsum('bqhmd,bkhmd->bhmqk', qf, k.astype(F32))
    s_new = jnp.where(jnp.tril(jnp.ones((t, t), bool)), s_new, -jnp.inf)
    s = jnp.concatenate([s_past, s_new], axis=-1) * (DB ** -0.5)
    wts = diff_weights(s, lam)
    return (jnp.einsum('bhqk,bkhe->bqhe', wts[..., :past], vp)
            + jnp.einsum('bhqk,bkhe->bqhe', wts[..., past:], v.astype(F32)))


def rwkv7_mixer(cols, prev, s0, mu, w0, w2, a0, a2, g2, k_k, k_a, r_k, lnx_w, lnx_b):
    bsz, t, _ = cols.shape
    shifted = jnp.concatenate([prev[:, None].astype(cols.dtype), cols[:, :-1]], axis=1)
    xs = cols + (shifted - cols) * mu
    r, k, v, wd, ad, gd = jnp.split(xs.astype(F32), [W_C, 2 * W_C, 3 * W_C, 3 * W_C + DECAY_LORA,
                                                     3 * W_C + DECAY_LORA + AAA_LORA], axis=-1)
    w = -jax.nn.softplus(-(w0 + jnp.tanh(wd) @ w2)) - 0.5
    decay = jnp.exp(-jnp.exp(w))
    a = jax.nn.sigmoid(a0 + ad @ a2)
    gate = jax.nn.sigmoid(gd) @ g2

    def heads(z):
        return z.reshape(bsz, t, H_C, HEAD_DIM)

    kk = heads(k * k_k)
    kk = kk * lax.rsqrt(jnp.sum(kk * kk, axis=-1, keepdims=True) + 1e-12)
    k = k * (1.0 + (a - 1.0) * k_a)
    r, k, v, decay, a = map(heads, (r, k, v, decay, a))

    def step(s, inp):
        r_t, k_t, v_t, d_t, kk_t, a_t = inp
        sa = jnp.einsum('bhvk,bhk->bhv', s, -kk_t)
        s = (s * d_t[:, :, None, :] + sa[..., None] * (kk_t * a_t)[:, :, None, :]
             + v_t[..., None] * k_t[:, :, None, :])
        return s, jnp.einsum('bhvk,bhk->bhv', s, r_t)

    seq = tuple(jnp.moveaxis(z, 1, 0) for z in (r, k, v, decay, kk, a))
    s, o = lax.scan(step, s0.astype(F32), seq)
    o = jnp.moveaxis(o, 0, 1)
    mean = jnp.mean(o, axis=-1, keepdims=True)
    var = jnp.mean(jnp.square(o - mean), axis=-1, keepdims=True)
    o = ((o - mean) * lax.rsqrt(var + RWKV_GN_EPS)).reshape(bsz, t, W_C) * lnx_w + lnx_b
    bonus = jnp.sum(r * k * r_k, axis=-1, keepdims=True) * v
    o = (o + bonus.reshape(bsz, t, W_C)) * gate
    return o.astype(cols.dtype), cols[:, -1], s.astype(cols.dtype)


def mem_project(mem, g, wk, wv):
    m = rmsnorm(mem, g)
    bsz = mem.shape[0]
    return ((m @ wk).reshape(bsz, -1, MEM_HEADS, MEM_HD), (m @ wv).reshape(bsz, -1, MEM_HEADS, MEM_HD))


def mem_attend(h, mem_k, mem_v, wq, wo):
    bsz, t, _ = h.shape
    q = (h @ wq).reshape(bsz, t, MEM_HEADS, MEM_HD)
    s = jnp.einsum('bthd,bmhd->bhtm', q.astype(F32), mem_k.astype(F32)) * (MEM_HD ** -0.5)
    p = jax.nn.softmax(s, axis=-1)
    o = jnp.einsum('bhtm,bmhd->bthd', p, mem_v.astype(F32)).reshape(bsz, t, MEM_INNER)
    return o.astype(h.dtype) @ wo


def conv_ffn(h, buf, w_in, conv_w, conv_b, w_out):
    gate, val = jnp.split(h @ w_in, 2, axis=-1)
    gate, new_buf = causal_dwconv(gate, buf, conv_w, conv_b)
    return (jax.nn.silu(gate) * val) @ w_out, new_buf


def trunk_layer(x, mem_k, mem_v, gdn_buf, gdn_s, rw_prev, rw_s, ffn_buf, attn_fn, lam_init, p):
    bsz, t, _ = x.shape
    h = rmsnorm(x, p['norm_mix'])
    cols = h @ p['w_in']
    a_cols, b_cols, c_cols = jnp.split(cols, [A_COLS, A_COLS + B_COLS], axis=-1)
    o_a, gdn_buf, gdn_s = gdn_mixer(a_cols, gdn_buf, gdn_s, p['gdn_conv_w'], p['gdn_a_log'],
                                    p['gdn_dt_bias'], p['gdn_norm'])
    q_b, k_b, v_b = jnp.split(b_cols, 3, axis=-1)
    q_b = q_b.reshape(bsz, t, H_B, 2, DB)
    k_b = k_b.reshape(bsz, t, H_B, 2, DB)
    v_b = v_b.reshape(bsz, t, H_B, 2 * DB)
    lam = diff_lambda(p['diff_lq1'], p['diff_lk1'], p['diff_lq2'], p['diff_lk2'], lam_init)
    o_b = attn_fn(q_b, k_b, v_b, lam)
    o_b = (rmsnorm(o_b, p['diff_subln']) * (1.0 - lam_init)).reshape(bsz, t, W_B).astype(x.dtype)
    o_c, rw_prev, rw_s = rwkv7_mixer(c_cols, rw_prev, rw_s, p['rw_mu'], p['rw_w0'], p['rw_w2'], p['rw_a0'],
                                     p['rw_a2'], p['rw_g2'], p['rw_k_k'], p['rw_k_a'], p['rw_r_k'],
                                     p['rw_lnx_w'], p['rw_lnx_b'])
    x = x + jnp.concatenate([o_a, o_b, o_c], axis=-1) @ p['w_out']
    x = x + mem_attend(rmsnorm(x, p['norm_mem']), mem_k, mem_v, p['mem_wq'], p['mem_wo'])
    f, ffn_buf = conv_ffn(rmsnorm(x, p['norm_ffn']), ffn_buf, p['ffn_w_in'], p['ffn_conv_w'],
                          p['ffn_conv_b'], p['ffn_w_out'])
    x = x + f
    return x, k_b.reshape(bsz, t, H_B, 2 * DB), v_b, gdn_buf, gdn_s, rw_prev, rw_s, ffn_buf


def setup_inputs(seed: int = 0) -> dict:
    key = jax.random.key(seed)
    ks = iter(jax.random.split(key, 64))

    def nrm(shape, scale=1.0):
        return jax.random.normal(next(ks), shape, F32) * scale

    def gain(shape, base=1.0):
        return base + 0.02 * jax.random.normal(next(ks), shape, F32)

    def unif(shape, lo, hi):
        return jax.random.uniform(next(ks), shape, F32, minval=lo, maxval=hi)

    n_pages = PAST_LEN // PAGE_SIZE
    n_used = DEC_BATCH * n_pages
    n_pool = n_used + n_used // 4
    page_table = jax.random.permutation(next(ks), n_pool)[:n_used].reshape(DEC_BATCH, n_pages).astype(jnp.int32)
    L = DEPTH
    return {
        'x_prompt': nrm((BATCH, SEQ, D_MODEL)),
        'x_sample': nrm((DEC_BATCH, DEC_SEQ, D_MODEL)),
        'cache_diff_k': nrm((L, n_pool, PAGE_SIZE, H_B, 2 * DB)),
        'cache_diff_v': nrm((L, n_pool, PAGE_SIZE, H_B, 2 * DB)),
        'state_gdn_conv': nrm((L, DEC_BATCH, GDN_CONV - 1, 3 * W_A)),
        'state_gdn': nrm((L, DEC_BATCH, H_A, HEAD_DIM, HEAD_DIM), 0.5),
        'state_rwkv_shift': nrm((L, DEC_BATCH, C_COLS)),
        'state_rwkv': nrm((L, DEC_BATCH, H_C, HEAD_DIM, HEAD_DIM), 0.5),
        'state_ffn_conv': nrm((L, DEC_BATCH, FFN_CONV - 1, D_FF)),
        'cache_mem_k': nrm((L, DEC_BATCH, MEM_TOKENS, MEM_HEADS, MEM_HD)),
        'cache_mem_v': nrm((L, DEC_BATCH, MEM_TOKENS, MEM_HEADS, MEM_HD)),
        'page_table': page_table,
        'mem_prompt': nrm((BATCH, MEM_TOKENS, D_MODEL)),
        'norm_mix': gain((L, D_MODEL)),
        'w_in': nrm((L, D_MODEL, N_IN_COLS), D_MODEL ** -0.5),
        'gdn_conv_w': nrm((L, GDN_CONV, 3 * W_A), GDN_CONV ** -0.5),
        'gdn_a_log': jnp.log(unif((L, H_A), 1.0, 16.0)),
        'gdn_dt_bias': nrm((L, H_A), 0.1),
        'gdn_norm': gain((L, HEAD_DIM)),
        'diff_lq1': nrm((L, DB), 0.1),
        'diff_lk1': nrm((L, DB), 0.1),
        'diff_lq2': nrm((L, DB), 0.1),
        'diff_lk2': nrm((L, DB), 0.1),
        'diff_subln': gain((L, 2 * DB)),
        'rw_mu': unif((L, C_COLS), 0.0, 1.0),
        'rw_w0': unif((L, W_C), -6.5, -1.5),
        'rw_w2': nrm((L, DECAY_LORA, W_C), 0.1 * DECAY_LORA ** -0.5),
        'rw_a0': nrm((L, W_C), 0.1),
        'rw_a2': nrm((L, AAA_LORA, W_C), AAA_LORA ** -0.5),
        'rw_g2': nrm((L, GATE_LORA, W_C), GATE_LORA ** -0.5),
        'rw_k_k': gain((L, W_C), 0.85),
        'rw_k_a': gain((L, W_C), 1.0),
        'rw_r_k': nrm((L, H_C, HEAD_DIM), 0.1),
        'rw_lnx_w': gain((L, W_C)),
        'rw_lnx_b': nrm((L, W_C), 0.02),
        'w_out': nrm((L, MIX_WIDTH, D_MODEL), MIX_WIDTH ** -0.5),
        'norm_mem': gain((L, D_MODEL)),
        'norm_mem_kv': gain((L, D_MODEL)),
        'mem_wq': nrm((L, D_MODEL, MEM_INNER), D_MODEL ** -0.5),
        'mem_wk': nrm((L, D_MODEL, MEM_INNER), D_MODEL ** -0.5),
        'mem_wv': nrm((L, D_MODEL, MEM_INNER), D_MODEL ** -0.5),
        'mem_wo': nrm((L, MEM_INNER, D_MODEL), MEM_INNER ** -0.5),
        'norm_ffn': gain((L, D_MODEL)),
        'ffn_w_in': nrm((L, D_MODEL, 2 * D_FF), D_MODEL ** -0.5),
        'ffn_conv_w': nrm((L, FFN_CONV, D_FF), FFN_CONV ** -0.5),
        'ffn_conv_b': nrm((L, D_FF), 0.02),
        'ffn_w_out': nrm((L, D_FF, D_MODEL), D_FF ** -0.5),
        'norm_final': gain((D_MODEL,)),
    }


def reference(x_prompt, x_sample, cache_diff_k, cache_diff_v, state_gdn_conv, state_gdn, state_rwkv_shift,
              state_rwkv, state_ffn_conv, cache_mem_k, cache_mem_v, page_table, mem_prompt, norm_mix, w_in,
              gdn_conv_w, gdn_a_log, gdn_dt_bias, gdn_norm, diff_lq1, diff_lk1, diff_lq2, diff_lk2, diff_subln,
              rw_mu, rw_w0, rw_w2, rw_a0, rw_a2, rw_g2, rw_k_k, rw_k_a, rw_r_k, rw_lnx_w, rw_lnx_b, w_out,
              norm_mem, norm_mem_kv, mem_wq, mem_wk, mem_wv, mem_wo, norm_ffn, ffn_w_in, ffn_conv_w, ffn_conv_b,
              ffn_w_out, norm_final):
    bp = x_prompt.shape[0]
    dt = x_prompt.dtype
    xp, xs = x_prompt, x_sample
    p_k, p_v, p_gc, p_gs, p_rp, p_rs, p_fc, p_mk, p_mv = [], [], [], [], [], [], [], [], []
    s_k, s_v, s_gc, s_gs, s_rp, s_rs, s_fc = [], [], [], [], [], [], []
    for l in range(DEPTH):
        lam_init = 0.8 - 0.6 * math.exp(-0.3 * l)
        p = dict(norm_mix=norm_mix[l], w_in=w_in[l], gdn_conv_w=gdn_conv_w[l], gdn_a_log=gdn_a_log[l],
                 gdn_dt_bias=gdn_dt_bias[l], gdn_norm=gdn_norm[l], diff_lq1=diff_lq1[l], diff_lk1=diff_lk1[l],
                 diff_lq2=diff_lq2[l], diff_lk2=diff_lk2[l], diff_subln=diff_subln[l], rw_mu=rw_mu[l],
                 rw_w0=rw_w0[l], rw_w2=rw_w2[l], rw_a0=rw_a0[l], rw_a2=rw_a2[l], rw_g2=rw_g2[l],
                 rw_k_k=rw_k_k[l], rw_k_a=rw_k_a[l], rw_r_k=rw_r_k[l], rw_lnx_w=rw_lnx_w[l],
                 rw_lnx_b=rw_lnx_b[l], w_out=w_out[l], norm_mem=norm_mem[l], mem_wq=mem_wq[l],
                 mem_wo=mem_wo[l], norm_ffn=norm_ffn[l], ffn_w_in=ffn_w_in[l], ffn_conv_w=ffn_conv_w[l],
                 ffn_conv_b=ffn_conv_b[l], ffn_w_out=ffn_w_out[l])
        mk, mv = mem_project(mem_prompt, norm_mem_kv[l], mem_wk[l], mem_wv[l])
        xp, k_new, v_new, gc, gs, rp, rs, fc = trunk_layer(
            xp, mk, mv,
            jnp.zeros((bp, GDN_CONV - 1, 3 * W_A), dt), jnp.zeros((bp, H_A, HEAD_DIM, HEAD_DIM), F32),
            jnp.zeros((bp, C_COLS), dt), jnp.zeros((bp, H_C, HEAD_DIM, HEAD_DIM), F32),
            jnp.zeros((bp, FFN_CONV - 1, D_FF), dt), diff_attn_prompt, lam_init, p)
        p_k.append(k_new); p_v.append(v_new); p_gc.append(gc); p_gs.append(gs)
        p_rp.append(rp); p_rs.append(rs); p_fc.append(fc); p_mk.append(mk); p_mv.append(mv)
        attn_s = functools.partial(diff_attn_paged, k_pool=cache_diff_k[l], v_pool=cache_diff_v[l],
                                   page_table=page_table)
        xs, k_new, v_new, gc, gs, rp, rs, fc = trunk_layer(
            xs, cache_mem_k[l], cache_mem_v[l], state_gdn_conv[l], state_gdn[l], state_rwkv_shift[l],
            state_rwkv[l], state_ffn_conv[l], attn_s, lam_init, p)
        s_k.append(k_new); s_v.append(v_new); s_gc.append(gc); s_gs.append(gs)
        s_rp.append(rp); s_rs.append(rs); s_fc.append(fc)
    y_prompt = rmsnorm(xp, norm_final)
    y_sample = rmsnorm(xs, norm_final)
    return (y_prompt, y_sample,
            jnp.stack(p_k), jnp.stack(p_v), jnp.stack(p_gc), jnp.stack(p_gs), jnp.stack(p_rp),
            jnp.stack(p_rs), jnp.stack(p_fc), jnp.stack(p_mk), jnp.stack(p_mv),
            jnp.stack(s_k), jnp.stack(s_v), jnp.stack(s_gc), jnp.stack(s_gs), jnp.stack(s_rp),
            jnp.stack(s_rs), jnp.stack(s_fc))
```

```python
import functools
import math

import jax
import jax.numpy as jnp
from jax import lax
from jax.experimental import pallas as pl
from jax.experimental.pallas import tpu as pltpu

F32 = jnp.float32
BF16 = jnp.bfloat16

D_MODEL = 1024
HEAD_DIM = 64
W_A = 384
W_B = 256
W_C = 384
H_A = W_A // HEAD_DIM
H_B = W_B // HEAD_DIM
H_C = W_C // HEAD_DIM
DB = HEAD_DIM // 2
GDN_CONV = 4
DECAY_LORA = 64
AAA_LORA = 64
GATE_LORA = 128
RWKV_GN_EPS = 64e-5
MEM_HEADS = 4
MEM_HD = 128
MEM_INNER = MEM_HEADS * MEM_HD
D_FF = 2816
FFN_CONV = 3
NORM_EPS = 1e-6
A_COLS = 4 * W_A + 2 * H_A
B_COLS = 3 * W_B
C_COLS = 3 * W_C + DECAY_LORA + AAA_LORA + GATE_LORA
GATE_PAD = 128
A_PAD = 4 * W_A + GATE_PAD

SUBLANES = 8
LANES = 128
VMEM_LIMIT = 56 * 1024 * 1024
GROUP_B = 8
PROMPT_CHUNK = 64

_NN = (((1,), (0,)), ((), ()))
_NT = (((1,), (1,)), ((), ()))
_BNN = (((2,), (1,)), ((0,), (0,)))
_BNT = (((2,), (2,)), ((0,), (0,)))
_BTN = (((1,), (1,)), ((0,), (0,)))


def _dot(a, b, dn):
    return lax.dot_general(a.astype(BF16), b.astype(BF16), dn, preferred_element_type=F32)


def _dot_f32(a, b, dn):
    return lax.dot_general(a, b, dn, precision=lax.Precision.HIGHEST, preferred_element_type=F32)


def _sigmoid(x):
    return 1.0 / (1.0 + jnp.exp(-x))


def _softplus(x):
    return jnp.maximum(x, 0.0) + jnp.log(1.0 + jnp.exp(-jnp.abs(x)))


def _rms(x, g, eps=NORM_EPS):
    return x * lax.rsqrt(jnp.mean(x * x, axis=-1, keepdims=True) + eps) * g


def _pick_tile(n, pref):
    if n <= pref:
        return n
    t = pref - pref % SUBLANES
    while n % t:
        t -= SUBLANES
    return t


def _params(*sem):
    return pltpu.CompilerParams(dimension_semantics=sem, vmem_limit_bytes=VMEM_LIMIT)


def _const_spec(shape):
    nd = len(shape)
    return pl.BlockSpec(shape, lambda *_: (0,) * nd, pipeline_mode=pl.Buffered(1))


def _proj_in_kernel(x_ref, g_ref, wa_ref, wb_ref, wc_ref, oa_ref, oq_ref, ok_ref, ov_ref, oc_ref):
    h = _rms(x_ref[...], g_ref[...]).astype(BF16)
    oa_ref[...] = jnp.dot(h, wa_ref[...], preferred_element_type=F32)
    oq_ref[...] = jnp.dot(h, wb_ref[:, 0:W_B], preferred_element_type=F32)
    ok_ref[...] = jnp.dot(h, wb_ref[:, W_B:2 * W_B], preferred_element_type=F32)
    ov_ref[...] = jnp.dot(h, wb_ref[:, 2 * W_B:3 * W_B], preferred_element_type=F32)
    oc_ref[...] = jnp.dot(h, wc_ref[...], preferred_element_type=F32)


def _proj_in(x2d, g, wa, wb, wc):
    n = x2d.shape[0]
    tm = _pick_tile(n, 512)
    row = lambda w: pl.BlockSpec((tm, w), lambda i: (i, 0))
    return pl.pallas_call(
        _proj_in_kernel,
        grid=(n // tm,),
        in_specs=[row(D_MODEL), _const_spec((1, D_MODEL)), _const_spec(wa.shape), _const_spec(wb.shape),
                  _const_spec(wc.shape)],
        out_specs=[row(A_PAD), row(W_B), row(W_B), row(W_B), row(C_COLS)],
        out_shape=[jax.ShapeDtypeStruct((n, w), F32) for w in (A_PAD, W_B, W_B, W_B, C_COLS)],
        compiler_params=_params("arbitrary"),
        name="proj_in",
    )(x2d, g, wa, wb, wc)


def _tri(c, inclusive):
    r = lax.broadcasted_iota(jnp.int32, (c, c), 0)
    q = lax.broadcasted_iota(jnp.int32, (c, c), 1)
    return (q <= r) if inclusive else (q < r)


def _chunk_cumsum(x, c):
    tri = jnp.broadcast_to(_tri(c, True).astype(F32), (x.shape[0], c, c))
    return _dot_f32(tri, x, _BNN)


def _neumann_solve(l, x, c):
    x = x + _dot(l, x, _BNN)
    p = l
    for _ in range(int(math.log2(c)) - 1):
        p = _dot(p, p, _BNN)
        x = x + _dot(p, x, _BNN)
    return x


def _gdn_kernel(a_ref, buf_ref, s0_ref, cw_ref, gp_ref, ng_ref, o_ref, s_ref, xp_ref, *, c, t_valid):
    ci = pl.program_id(1)
    g_b = a_ref.shape[0]
    hist = SUBLANES - (GDN_CONV - 1)

    @pl.when(ci == 0)
    def _():
        s_ref[...] = s0_ref[...]
        xp_ref[:, hist:SUBLANES, :] = buf_ref[...]

    xp_ref[:, SUBLANES:SUBLANES + c, :] = a_ref[:, :, 0:3 * W_A]
    y = xp_ref[:, hist:hist + c, :] * cw_ref[0:1, :]
    for j in range(1, GDN_CONV):
        y = y + xp_ref[:, hist + j:hist + j + c, :] * cw_ref[j:j + 1, :]
    xp_ref[:, hist:SUBLANES, :] = xp_ref[:, c + hist:c + SUBLANES, :]
    qkv = y * _sigmoid(y)

    gates = a_ref[:, :, 4 * W_A:A_PAD]
    tpos = ci * c + lax.broadcasted_iota(jnp.int32, (1, c, 1), 1)
    valid = tpos < t_valid
    beta_all = jnp.where(valid, _sigmoid(gates), 0.0)
    g_all = jnp.where(valid, -jnp.exp(gp_ref[0:1, :]) * _softplus(gates + gp_ref[1:2, :]), 0.0)
    gam_all = _chunk_cumsum(g_all, c)
    gam_t = jnp.swapaxes(gam_all, 1, 2)
    strict = _tri(c, False)
    incl = _tri(c, True)

    for h in range(H_A):
        lo = h * HEAD_DIM
        q = qkv[:, :, lo:lo + HEAD_DIM]
        k = qkv[:, :, W_A + lo:W_A + lo + HEAD_DIM]
        v = qkv[:, :, 2 * W_A + lo:2 * W_A + lo + HEAD_DIM]
        q = q * lax.rsqrt(jnp.sum(q * q, axis=-1, keepdims=True) + 1e-6) * (HEAD_DIM ** -0.5)
        k = k * lax.rsqrt(jnp.sum(k * k, axis=-1, keepdims=True) + 1e-6)
        beta = beta_all[:, :, h:h + 1]
        gcol = gam_all[:, :, H_A + h:H_A + h + 1]
        grow = gam_t[:, H_A + h:H_A + h + 1, :]
        glast = gcol[:, c - 1:c, :]
        diff = gcol - grow
        eg = jnp.exp(gcol)
        kb = k * beta
        a_mat = _dot(kb, k, _BNT) * jnp.exp(jnp.where(strict, diff, -jnp.inf))
        x = jnp.concatenate([v * beta, kb * eg], axis=-1)
        x = _neumann_solve(-a_mat, x, c)
        u = x[:, :, 0:HEAD_DIM]
        w = x[:, :, HEAD_DIM:2 * HEAD_DIM]
        qk = _dot(q, k, _BNT) * jnp.exp(jnp.where(incl, diff, -jnp.inf))
        s = s_ref[:, h]
        v_new = u - _dot(w, s, _BNN)
        o = _dot(q * eg, s, _BNN) + _dot(qk, v_new, _BNN)
        k_dec = k * jnp.exp(glast - gcol)
        s_ref[:, h] = s * jnp.exp(glast) + _dot(k_dec, v_new, _BTN)
        z = a_ref[:, :, 3 * W_A + lo:3 * W_A + lo + HEAD_DIM]
        o_ref[:, :, lo:lo + HEAD_DIM] = _rms(o, ng_ref[...]) * (z * _sigmoid(z))


def _gdn(a_cols, conv_buf, s0, conv_w, gate_par, norm_g, c, t_valid):
    b, t, _ = a_cols.shape
    g_b = min(GROUP_B, b)
    kern = functools.partial(_gdn_kernel, c=c, t_valid=t_valid)
    state = pl.BlockSpec((g_b, H_A, HEAD_DIM, HEAD_DIM), lambda bi, ci: (bi, 0, 0, 0))
    return pl.pallas_call(
        kern,
        grid=(b // g_b, t // c),
        in_specs=[pl.BlockSpec((g_b, c, A_PAD), lambda bi, ci: (bi, ci, 0)),
                  pl.BlockSpec((g_b, GDN_CONV - 1, 3 * W_A), lambda bi, ci: (bi, 0, 0)),
                  state, _const_spec(conv_w.shape), _const_spec(gate_par.shape), _const_spec(norm_g.shape)],
        out_specs=[pl.BlockSpec((g_b, c, W_A), lambda bi, ci: (bi, ci, 0)), state],
        out_shape=[jax.ShapeDtypeStruct((b, t, W_A), F32),
                   jax.ShapeDtypeStruct((b, H_A, HEAD_DIM, HEAD_DIM), F32)],
        scratch_shapes=[pltpu.VMEM((g_b, c + SUBLANES, 3 * W_A), F32)],
        compiler_params=_params("arbitrary", "arbitrary"),
        name="gdn_mixer",
    )(a_cols, conv_buf, s0, conv_w, gate_par, norm_g)


def _rwkv_kernel(c_ref, prev_ref, s0_ref, mu_ref, vec_ref, w2_ref, a2_ref, g2_ref, o_ref, s_ref, xp_ref,
                 *, c, t_valid):
    ci = pl.program_id(1)
    g_b = c_ref.shape[0]

    @pl.when(ci == 0)
    def _():
        s_ref[...] = s0_ref[...]
        xp_ref[:, SUBLANES - 1:SUBLANES, :] = prev_ref[...]

    x = c_ref[...]
    xp_ref[:, SUBLANES:SUBLANES + c, :] = x
    shifted = xp_ref[:, SUBLANES - 1:SUBLANES - 1 + c, :]
    xp_ref[:, SUBLANES - 1:SUBLANES, :] = xp_ref[:, c + SUBLANES - 1:c + SUBLANES, :]
    xs = (x + (shifted - x) * mu_ref[...]).reshape(g_b * c, C_COLS)

    w0, a0, k_k, k_a, r_k, lnx_w, lnx_b = (vec_ref[i:i + 1, :] for i in range(7))
    r = xs[:, 0:W_C]
    k = xs[:, W_C:2 * W_C]
    v = xs[:, 2 * W_C:3 * W_C]
    lora = xs[:, 3 * W_C:3 * W_C + DECAY_LORA + AAA_LORA]
    gd = xs[:, 3 * W_C + DECAY_LORA + AAA_LORA:C_COLS]
    lane = lax.broadcasted_iota(jnp.int32, (1, DECAY_LORA + AAA_LORA), 1)
    lora = jnp.where(lane < DECAY_LORA, jnp.tanh(lora), lora)
    w = -_softplus(-(w0 + _dot(lora, w2_ref[...], _NN))) - 0.5
    a = _sigmoid(a0 + _dot(lora, a2_ref[...], _NN))
    gate = _dot(_sigmoid(gd), g2_ref[...], _NN)
    kk = k * k_k
    k = k * (1.0 + (a - 1.0) * k_a)

    tpos = ci * c + lax.broadcasted_iota(jnp.int32, (1, c, 1), 1)
    valid = tpos < t_valid
    shape3 = (g_b, c, W_C)
    lw = jnp.where(valid, (-jnp.exp(w)).reshape(shape3), 0.0)
    cl = _chunk_cumsum(lw, c)
    p_in = jnp.exp(cl)
    p_inv = jnp.exp(-cl)
    p_ex = jnp.exp(cl - lw)
    r3, k3, v3, kk3, a3 = (z.reshape(shape3) for z in (r, k, v, kk, a))
    k3 = jnp.where(valid, k3, 0.0)
    kk3 = jnp.where(valid, kk3, 0.0)

    row = lax.broadcasted_iota(jnp.int32, (2 * c, 2 * c), 0)
    col = lax.broadcasted_iota(jnp.int32, (2 * c, 2 * c), 1)
    colc = jnp.where(col >= c, col - c, col)
    mask = colc < jnp.where(row < c, row, row - c + 1)
    second = lax.broadcasted_iota(jnp.int32, (1, 1, 2 * c), 2) >= c

    outs = []
    for h in range(H_C):
        lo = h * HEAD_DIM
        sl = slice(lo, lo + HEAD_DIM)
        rh, kh, vh, ah = r3[:, :, sl], k3[:, :, sl], v3[:, :, sl], a3[:, :, sl]
        kkh = kk3[:, :, sl]
        kkh = kkh * lax.rsqrt(jnp.sum(kkh * kkh, axis=-1, keepdims=True) + 1e-12)
        at = -kkh * p_ex[:, :, sl]
        bt = kkh * ah * p_inv[:, :, sl]
        kt = kh * p_inv[:, :, sl]
        rt = rh * p_in[:, :, sl]
        lhs = jnp.concatenate([at, rt], axis=1)
        rhs = jnp.concatenate([bt, kt], axis=1)
        gm = jnp.where(mask, _dot(lhs, rhs, _BNT), 0.0)
        top = gm[:, 0:c, :]
        bot = gm[:, c:2 * c, :]
        l_ab = top[:, :, 0:c]
        vv = jnp.concatenate([vh, vh], axis=1)
        s = s_ref[:, h]
        rhs_u = _dot(at, s, _BNT) + _dot(jnp.where(second, top, 0.0), vv, _BNN)
        u = _neumann_solve(l_ab, rhs_u, c)
        uv = jnp.concatenate([u, vh], axis=1)
        o = _dot(rt, s, _BNT) + _dot(bot, uv, _BNN)
        s_ref[:, h] = (s + _dot(uv, rhs, _BTN)) * p_in[:, c - 1:c, sl]
        mean = jnp.mean(o, axis=-1, keepdims=True)
        var = jnp.mean(jnp.square(o - mean), axis=-1, keepdims=True)
        o = (o - mean) * lax.rsqrt(var + RWKV_GN_EPS) * lnx_w[:, sl] + lnx_b[:, sl]
        bonus = jnp.sum(rh * k3[:, :, sl] * r_k[:, sl], axis=-1, keepdims=True) * vh
        outs.append(o + bonus)
    o_all = jnp.concatenate(outs, axis=-1)
    o_ref[...] = o_all * gate.reshape(shape3)


def _rwkv(c_cols, prev, s0, mu, vecs, w2p, a2p, g2, c, t_valid):
    b, t, _ = c_cols.shape
    g_b = min(GROUP_B, b)
    kern = functools.partial(_rwkv_kernel, c=c, t_valid=t_valid)
    state = pl.BlockSpec((g_b, H_C, HEAD_DIM, HEAD_DIM), lambda bi, ci: (bi, 0, 0, 0))
    return pl.pallas_call(
        kern,
        grid=(b // g_b, t // c),
        in_specs=[pl.BlockSpec((g_b, c, C_COLS), lambda bi, ci: (bi, ci, 0)),
                  pl.BlockSpec((g_b, 1, C_COLS), lambda bi, ci: (bi, 0, 0)),
                  state, _const_spec(mu.shape), _const_spec(vecs.shape), _const_spec(w2p.shape),
                  _const_spec(a2p.shape), _const_spec(g2.shape)],
        out_specs=[pl.BlockSpec((g_b, c, W_C), lambda bi, ci: (bi, ci, 0)), state],
        out_shape=[jax.ShapeDtypeStruct((b, t, W_C), F32),
                   jax.ShapeDtypeStruct((b, H_C, HEAD_DIM, HEAD_DIM), F32)],
        scratch_shapes=[pltpu.VMEM((g_b, c + SUBLANES, C_COLS), F32)],
        compiler_params=_params("arbitrary", "arbitrary"),
        name="rwkv_mixer",
    )(c_cols, prev, s0, mu, vecs, w2p, a2p, g2)


def _lane_group(width, group):
    return lax.broadcasted_iota(jnp.int32, (1, width), 1) // group


def _diff_lambda(lp_ref, lam_init):
    l1 = jnp.exp(jnp.sum(lp_ref[0:1, :] * lp_ref[1:2, :], axis=-1, keepdims=True))
    l2 = jnp.exp(jnp.sum(lp_ref[2:3, :] * lp_ref[3:4, :], axis=-1, keepdims=True))
    return l1 - l2 + lam_init


def _softmax_step(s, v, m_ref, l_ref, acc_ref, idx):
    m_old = m_ref[idx]
    m_new = jnp.maximum(m_old, jnp.max(s, axis=-1, keepdims=True))
    alpha = jnp.exp(m_old - m_new)
    p = jnp.exp(s - m_new)
    l_ref[idx] = alpha * l_ref[idx] + jnp.sum(p, axis=-1, keepdims=True)
    acc_ref[idx] = alpha * acc_ref[idx] + _dot(p, v, _NN)
    m_ref[idx] = m_new


def _subln(o, sub_ref, lam_init):
    r = lax.broadcasted_iota(jnp.int32, (W_B, W_B), 0) // HEAD_DIM
    q = lax.broadcasted_iota(jnp.int32, (W_B, W_B), 1) // HEAD_DIM
    group_mean = jnp.where(r == q, 1.0 / HEAD_DIM, 0.0).astype(F32)
    ms = _dot_f32(o * o, group_mean, _NN)
    return o * lax.rsqrt(ms + NORM_EPS) * sub_ref[...] * (1.0 - lam_init)


def _dattn_kernel(lp_ref, q_ref, k_ref, v_ref, sub_ref, o_ref, m_ref, l_ref, acc_ref, *, lam_init):
    qi = pl.program_id(1)
    ki = pl.program_id(2)
    tq = q_ref.shape[1]
    maps = _lane_group(W_B, DB)

    @pl.when(ki == 0)
    def _():
        m_ref[...] = jnp.full(m_ref.shape, -jnp.inf, F32)
        l_ref[...] = jnp.zeros(l_ref.shape, F32)
        acc_ref[...] = jnp.zeros(acc_ref.shape, F32)

    def block(diagonal):
        q = q_ref[0]
        k = k_ref[0].astype(BF16)
        v = v_ref[0].astype(BF16)
        if diagonal:
            causal = _tri(tq, True)
        for hm in range(2 * H_B):
            s = _dot(jnp.where(maps == hm, q, 0.0), k, _NT) * (DB ** -0.5)
            if diagonal:
                s = jnp.where(causal, s, -jnp.inf)
            _softmax_step(s, v, m_ref, l_ref, acc_ref, hm)

    pl.when(ki < qi)(lambda: block(False))

    @pl.when(ki == qi)
    def _():
        block(True)
        lam = _diff_lambda(lp_ref, lam_init)
        heads = _lane_group(W_B, HEAD_DIM)
        o = jnp.zeros((tq, W_B), F32)
        for h in range(H_B):
            oh = acc_ref[2 * h] / l_ref[2 * h] - lam * (acc_ref[2 * h + 1] / l_ref[2 * h + 1])
            o = jnp.where(heads == h, oh, o)
        o_ref[0] = _subln(o, sub_ref, lam_init)


def _dattn_prompt(q, k, v, lam_par, subln, lam_init):
    b, t, _ = q.shape
    tq = _pick_tile(t, 512)
    nq = t // tq
    kern = functools.partial(_dattn_kernel, lam_init=lam_init)
    kv = pl.BlockSpec((1, tq, W_B), lambda bi, qi, ki: (bi, jnp.minimum(ki, qi), 0))
    qo = pl.BlockSpec((1, tq, W_B), lambda bi, qi, ki: (bi, qi, 0))
    return pl.pallas_call(
        kern,
        grid=(b, nq, nq),
        in_specs=[_const_spec(lam_par.shape), qo, kv, kv, _const_spec(subln.shape)],
        out_specs=qo,
        out_shape=jax.ShapeDtypeStruct((b, t, W_B), F32),
        scratch_shapes=[pltpu.VMEM((2 * H_B, tq, 1), F32), pltpu.VMEM((2 * H_B, tq, 1), F32),
                        pltpu.VMEM((2 * H_B, tq, W_B), F32)],
        compiler_params=_params("arbitrary", "arbitrary", "arbitrary"),
        name="diff_attn_prompt",
    )(lam_par, q, k, v, subln)


def _paged_kernel(pt_ref, lp_ref, q_ref, kn_ref, vn_ref, sub_ref, *refs, pages, lam_init, t_valid):
    k_pages = refs[0:pages]
    v_pages = refs[pages:2 * pages]
    o_ref, m_ref, l_ref, acc_ref = refs[2 * pages:]
    j = pl.program_id(1)
    tp = q_ref.shape[1]
    maps = _lane_group(W_B, DB)

    @pl.when(j == 0)
    def _():
        m_ref[...] = jnp.full(m_ref.shape, -jnp.inf, F32)
        l_ref[...] = jnp.zeros(l_ref.shape, F32)
        acc_ref[...] = jnp.zeros(acc_ref.shape, F32)

    q = q_ref[0]
    qm = jnp.concatenate([jnp.where(maps == hm, q, 0.0) for hm in range(2 * H_B)], axis=0).astype(BF16)
    scale = DB ** -0.5

    s = jnp.concatenate([_dot(qm, kp[0, 0], _NT) for kp in k_pages], axis=-1) * scale
    v = jnp.concatenate([vp[0, 0] for vp in v_pages], axis=0)
    _softmax_step(s, v, m_ref, l_ref, acc_ref, 0)

    @pl.when(j == pl.num_programs(1) - 1)
    def _():
        rows = 2 * H_B * tp
        tq_pos = lax.broadcasted_iota(jnp.int32, (rows, tp), 0) % tp
        tk_pos = lax.broadcasted_iota(jnp.int32, (rows, tp), 1)
        s_new = _dot(qm, kn_ref[0], _NT) * scale
        s_new = jnp.where((tk_pos <= tq_pos) & (tk_pos < t_valid), s_new, -jnp.inf)
        _softmax_step(s_new, vn_ref[0], m_ref, l_ref, acc_ref, 0)
        lam = _diff_lambda(lp_ref, lam_init)
        heads = _lane_group(W_B, HEAD_DIM)
        res = acc_ref[0] / l_ref[0]
        o = jnp.zeros((tp, W_B), F32)
        for h in range(H_B):
            oh = res[2 * h * tp:(2 * h + 1) * tp] - lam * res[(2 * h + 1) * tp:(2 * h + 2) * tp]
            o = jnp.where(heads == h, oh, o)
        o_ref[0] = _subln(o, sub_ref, lam_init)


def _dattn_paged(q, k, v, k_pool, v_pool, layer, page_table, lam_par, subln, lam_init, t_valid):
    b, tp, _ = q.shape
    n_pages = page_table.shape[1]
    page = k_pool.shape[2]
    pages = math.gcd(n_pages, 8)
    rows = 2 * H_B * tp
    kern = functools.partial(_paged_kernel, pages=pages, lam_init=lam_init, t_valid=t_valid)

    def page_spec(i):
        return pl.BlockSpec((1, 1, page, W_B), lambda bi, j, pt: (layer, pt[bi, j * pages + i], 0, 0))

    tok = pl.BlockSpec((1, tp, W_B), lambda bi, j, pt: (bi, 0, 0))
    const = lambda shape: pl.BlockSpec(shape, lambda bi, j, pt: (0,) * len(shape))
    grid_spec = pltpu.PrefetchScalarGridSpec(
        num_scalar_prefetch=1,
        grid=(b, n_pages // pages),
        in_specs=[const(lam_par.shape), tok, tok, tok, const(subln.shape)]
        + [page_spec(i) for i in range(pages)] * 2,
        out_specs=tok,
        scratch_shapes=[pltpu.VMEM((1, rows, 1), F32), pltpu.VMEM((1, rows, 1), F32),
                        pltpu.VMEM((1, rows, W_B), F32)],
    )
    return pl.pallas_call(
        kern,
        grid_spec=grid_spec,
        out_shape=jax.ShapeDtypeStruct((b, tp, W_B), F32),
        compiler_params=_params("arbitrary", "arbitrary"),
        name="diff_attn_paged",
    )(page_table, lam_par, q, k, v, subln, *([k_pool] * pages), *([v_pool] * pages))


def _mem_project_kernel(m_ref, g_ref, wk_ref, wv_ref, k_ref, v_ref):
    h = _rms(m_ref[...], g_ref[...]).astype(BF16)
    k_ref[...] = jnp.dot(h, wk_ref[...], preferred_element_type=F32)
    v_ref[...] = jnp.dot(h, wv_ref[...], preferred_element_type=F32)


def _mem_project(mem2d, g, wk, wv):
    n = mem2d.shape[0]
    tm = _pick_tile(n, 512)
    return pl.pallas_call(
        _mem_project_kernel,
        grid=(n // tm,),
        in_specs=[pl.BlockSpec((tm, D_MODEL), lambda i: (i, 0)), _const_spec(g.shape), _const_spec(wk.shape),
                  _const_spec(wv.shape)],
        out_specs=[pl.BlockSpec((tm, MEM_INNER), lambda i: (i, 0))] * 2,
        out_shape=[jax.ShapeDtypeStruct((n, MEM_INNER), F32)] * 2,
        compiler_params=_params("arbitrary"),
        name="mem_project",
    )(mem2d, g, wk, wv)


def _mix_mem_kernel(x_ref, oa_ref, ob_ref, oc_ref, wout_ref, g_ref, wq_ref, mk_ref, mv_ref, wo_ref, out_ref):
    bb, tt, _ = x_ref.shape
    m = bb * tt
    x = x_ref[...].reshape(m, D_MODEL)
    mix = (_dot(oa_ref[...].reshape(m, W_A), wout_ref[0:W_A, :], _NN)
           + _dot(ob_ref[...].reshape(m, W_B), wout_ref[W_A:W_A + W_B, :], _NN)
           + _dot(oc_ref[...].reshape(m, W_C), wout_ref[W_A + W_B:D_MODEL, :], _NN))
    x = x + mix
    q = _dot(_rms(x, g_ref[...]), wq_ref[...], _NN)
    heads = []
    for h in range(MEM_HEADS):
        sl = slice(h * MEM_HD, (h + 1) * MEM_HD)
        qh = q[:, sl].reshape(bb, tt, MEM_HD)
        s = _dot(qh, mk_ref[:, :, sl], _BNT) * (MEM_HD ** -0.5)
        p = jnp.exp(s - jnp.max(s, axis=-1, keepdims=True))
        p = p / jnp.sum(p, axis=-1, keepdims=True)
        heads.append(_dot(p, mv_ref[:, :, sl], _BNN).reshape(m, MEM_HD))
    o = jnp.concatenate(heads, axis=-1)
    out_ref[...] = (x + _dot(o, wo_ref[...], _NN)).reshape(bb, tt, D_MODEL)


def _mix_mem(x, o_a, o_b, o_c, w_out, g, wq, mem_k, mem_v, wo, bb, tt):
    b, t, _ = x.shape
    mt = mem_k.shape[1]
    blk = lambda w: pl.BlockSpec((bb, tt, w), lambda bi, ti: (bi, ti, 0))
    mem = pl.BlockSpec((bb, mt, MEM_INNER), lambda bi, ti: (bi, 0, 0))
    return pl.pallas_call(
        _mix_mem_kernel,
        grid=(b // bb, t // tt),
        in_specs=[blk(D_MODEL), blk(W_A), blk(W_B), blk(W_C), _const_spec(w_out.shape), _const_spec(g.shape),
                  _const_spec(wq.shape), mem, mem, _const_spec(wo.shape)],
        out_specs=blk(D_MODEL),
        out_shape=jax.ShapeDtypeStruct((b, t, D_MODEL), F32),
        compiler_params=_params("arbitrary", "arbitrary"),
        name="mix_out_mem_attn",
    )(x, o_a, o_b, o_c, w_out, g, wq, mem_k, mem_v, wo)


FF_CHUNK = 256


def _ffn_kernel(x_ref, buf_ref, g_ref, wg_ref, wv_ref, cw_ref, cb_ref, wo_ref, gf_ref, out_ref, tail_ref,
                carry_ref, gp_ref, acc_ref, *, final_norm):
    ti = pl.program_id(1)
    bb, tt, _ = x_ref.shape
    m = bb * tt
    hist = SUBLANES - (FFN_CONV - 1)

    @pl.when(ti == 0)
    def _():
        carry_ref[:, hist:SUBLANES, :] = buf_ref[...]

    x = x_ref[...].reshape(m, D_MODEL)
    h = _rms(x, g_ref[...]).astype(BF16)
    for ck in range(D_FF // FF_CHUNK):
        sl = slice(ck * FF_CHUNK, (ck + 1) * FF_CHUNK)
        gate = jnp.dot(h, wg_ref[:, sl], preferred_element_type=F32).reshape(bb, tt, FF_CHUNK)
        val = jnp.dot(h, wv_ref[:, sl], preferred_element_type=F32)
        gp_ref[:, hist:SUBLANES, :] = carry_ref[:, hist:SUBLANES, sl]
        gp_ref[:, SUBLANES:SUBLANES + tt, :] = gate
        y = cb_ref[:, sl] + gate * cw_ref[FFN_CONV - 1:FFN_CONV, sl]
        for j in range(FFN_CONV - 1):
            y = y + gp_ref[:, hist + j:hist + j + tt, :] * cw_ref[j:j + 1, sl]
        carry_ref[:, :, sl] = gate[:, tt - SUBLANES:tt, :]
        y = y.reshape(m, FF_CHUNK)
        act = (y * _sigmoid(y) * val).astype(BF16)
        part = jnp.dot(act, wo_ref[sl, :], preferred_element_type=F32)
        if ck == 0:
            acc_ref[...] = x + part
        else:
            acc_ref[...] += part
    tail_ref[...] = carry_ref[...]
    y = acc_ref[...]
    if final_norm:
        y = _rms(y, gf_ref[...])
    out_ref[...] = y.reshape(bb, tt, D_MODEL)


def _ffn(x, buf, g, wg, wv, conv_w, conv_b, wo, g_final, bb, tt, final_norm):
    b, t, _ = x.shape
    kern = functools.partial(_ffn_kernel, final_norm=final_norm)
    blk = pl.BlockSpec((bb, tt, D_MODEL), lambda bi, ti: (bi, ti, 0))
    return pl.pallas_call(
        kern,
        grid=(b // bb, t // tt),
        in_specs=[blk, pl.BlockSpec((bb, FFN_CONV - 1, D_FF), lambda bi, ti: (bi, 0, 0)), _const_spec(g.shape),
                  _const_spec(wg.shape), _const_spec(wv.shape), _const_spec(conv_w.shape),
                  _const_spec(conv_b.shape), _const_spec(wo.shape), _const_spec(g_final.shape)],
        out_specs=[blk, pl.BlockSpec((bb, SUBLANES, D_FF), lambda bi, ti: (bi, 0, 0))],
        out_shape=[jax.ShapeDtypeStruct((b, t, D_MODEL), F32), jax.ShapeDtypeStruct((b, SUBLANES, D_FF), F32)],
        scratch_shapes=[pltpu.VMEM((bb, SUBLANES, D_FF), F32), pltpu.VMEM((bb, tt + SUBLANES, FF_CHUNK), F32),
                        pltpu.VMEM((bb * tt, D_MODEL), F32)],
        compiler_params=_params("arbitrary", "arbitrary"),
        name="conv_glu_ffn",
    )(x, buf, g, wg, wv, conv_w, conv_b, wo, g_final)


def _layer_weights(l, w):
    bf = lambda z: z.astype(BF16)
    row = lambda z: z.reshape(1, -1)
    w_in = w['w_in'][l]
    gates = jnp.pad(w_in[:, 4 * W_A:A_COLS], ((0, 0), (0, GATE_PAD - 2 * H_A)))
    pad_gate = lambda z: jnp.pad(z, (H_A, GATE_PAD - 2 * H_A)).reshape(1, GATE_PAD)
    zeros_lora = jnp.zeros((DECAY_LORA, W_C), F32)
    return dict(
        norm_mix=row(w['norm_mix'][l]),
        wa=bf(jnp.concatenate([w_in[:, 0:4 * W_A], gates], axis=1)),
        wb=bf(w_in[:, A_COLS:A_COLS + B_COLS]),
        wc=bf(w_in[:, A_COLS + B_COLS:]),
        gdn_conv_w=w['gdn_conv_w'][l],
        gdn_gate=jnp.concatenate([pad_gate(w['gdn_a_log'][l]), pad_gate(w['gdn_dt_bias'][l])], axis=0),
        gdn_norm=row(w['gdn_norm'][l]),
        lam_par=jnp.stack([w['diff_lq1'][l], w['diff_lk1'][l], w['diff_lq2'][l], w['diff_lk2'][l]]),
        subln=row(jnp.tile(w['diff_subln'][l], H_B)),
        rw_mu=row(w['rw_mu'][l]),
        rw_vecs=jnp.stack([w['rw_w0'][l], w['rw_a0'][l], w['rw_k_k'][l], w['rw_k_a'][l],
                           w['rw_r_k'][l].reshape(-1), w['rw_lnx_w'][l], w['rw_lnx_b'][l],
                           jnp.zeros((W_C,), F32)]),
        rw_w2=bf(jnp.concatenate([w['rw_w2'][l], zeros_lora], axis=0)),
        rw_a2=bf(jnp.concatenate([zeros_lora, w['rw_a2'][l]], axis=0)),
        rw_g2=bf(w['rw_g2'][l]),
        w_out=bf(w['w_out'][l]),
        norm_mem=row(w['norm_mem'][l]),
        norm_mem_kv=row(w['norm_mem_kv'][l]),
        mem_wq=bf(w['mem_wq'][l]), mem_wk=bf(w['mem_wk'][l]), mem_wv=bf(w['mem_wv'][l]),
        mem_wo=bf(w['mem_wo'][l]),
        norm_ffn=row(w['norm_ffn'][l]),
        ffn_wg=bf(w['ffn_w_in'][l][:, 0:D_FF]), ffn_wv=bf(w['ffn_w_in'][l][:, D_FF:]),
        ffn_conv_w=w['ffn_conv_w'][l], ffn_conv_b=row(w['ffn_conv_b'][l]),
        ffn_wo=bf(w['ffn_w_out'][l]),
        norm_final=row(w['norm_final']),
    )


def _trunk_layer(x, t_valid, chunk, mem_k, mem_v, gdn_buf, gdn_s, rw_prev, rw_s, ffn_buf, attn_fn, p,
                 mem_bb, ffn_bb, tile_t, final_norm):
    b, t, _ = x.shape
    a_cols, q_b, k_b, v_b, c_cols = _proj_in(x.reshape(b * t, D_MODEL), p['norm_mix'], p['wa'], p['wb'], p['wc'])
    a_cols = a_cols.reshape(b, t, A_PAD)
    c_cols = c_cols.reshape(b, t, C_COLS)
    q_b, k_b, v_b = (z.reshape(b, t, W_B) for z in (q_b, k_b, v_b))
    o_a, gdn_s_new = _gdn(a_cols, gdn_buf, gdn_s, p['gdn_conv_w'], p['gdn_gate'], p['gdn_norm'], chunk, t_valid)
    o_b = attn_fn(q_b, k_b, v_b)
    o_c, rw_s_new = _rwkv(c_cols, rw_prev.reshape(b, 1, C_COLS), rw_s, p['rw_mu'], p['rw_vecs'], p['rw_w2'],
                          p['rw_a2'], p['rw_g2'], chunk, t_valid)
    x = _mix_mem(x, o_a, o_b, o_c, p['w_out'], p['norm_mem'], p['mem_wq'], mem_k, mem_v, p['mem_wo'],
                 mem_bb, tile_t)
    x, tail = _ffn(x, ffn_buf, p['norm_ffn'], p['ffn_wg'], p['ffn_wv'], p['ffn_conv_w'], p['ffn_conv_b'],
                   p['ffn_wo'], p['norm_final'], ffn_bb, tile_t, final_norm)
    keep = min(t_valid, GDN_CONV - 1)
    gdn_buf_new = jnp.concatenate([gdn_buf[:, keep:], a_cols[:, t_valid - keep:t_valid, 0:3 * W_A]], axis=1)
    last = t_valid - (t - SUBLANES)
    ffn_buf_new = tail[:, last - (FFN_CONV - 1):last]
    return (x, k_b[:, 0:t_valid], v_b[:, 0:t_valid], gdn_buf_new, gdn_s_new, c_cols[:, t_valid - 1], rw_s_new,
            ffn_buf_new)


def kernel(x_prompt, x_sample, cache_diff_k, cache_diff_v, state_gdn_conv, state_gdn, state_rwkv_shift, state_rwkv, state_ffn_conv, cache_mem_k, cache_mem_v, page_table, mem_prompt, norm_mix, w_in, gdn_conv_w, gdn_a_log, gdn_dt_bias, gdn_norm, diff_lq1, diff_lk1, diff_lq2, diff_lk2, diff_subln, rw_mu, rw_w0, rw_w2, rw_a0, rw_a2, rw_g2, rw_k_k, rw_k_a, rw_r_k, rw_lnx_w, rw_lnx_b, w_out, norm_mem, norm_mem_kv, mem_wq, mem_wk, mem_wv, mem_wo, norm_ffn, ffn_w_in, ffn_conv_w, ffn_conv_b, ffn_w_out, norm_final):
    w = dict(norm_mix=norm_mix, w_in=w_in, gdn_conv_w=gdn_conv_w, gdn_a_log=gdn_a_log, gdn_dt_bias=gdn_dt_bias,
             gdn_norm=gdn_norm, diff_lq1=diff_lq1, diff_lk1=diff_lk1, diff_lq2=diff_lq2, diff_lk2=diff_lk2,
             diff_subln=diff_subln, rw_mu=rw_mu, rw_w0=rw_w0, rw_w2=rw_w2, rw_a0=rw_a0, rw_a2=rw_a2, rw_g2=rw_g2,
             rw_k_k=rw_k_k, rw_k_a=rw_k_a, rw_r_k=rw_r_k, rw_lnx_w=rw_lnx_w, rw_lnx_b=rw_lnx_b, w_out=w_out,
             norm_mem=norm_mem, norm_mem_kv=norm_mem_kv, mem_wq=mem_wq, mem_wk=mem_wk, mem_wv=mem_wv,
             mem_wo=mem_wo, norm_ffn=norm_ffn, ffn_w_in=ffn_w_in, ffn_conv_w=ffn_conv_w, ffn_conv_b=ffn_conv_b,
             ffn_w_out=ffn_w_out, norm_final=norm_final)
    depth = w_in.shape[0]
    bp, tp, _ = x_prompt.shape
    bs, ts, _ = x_sample.shape
    mt = mem_prompt.shape[1]
    ts_pad = -(-ts // SUBLANES) * SUBLANES
    n_pool, page = cache_diff_k.shape[1], cache_diff_k.shape[2]
    k_pool = cache_diff_k.reshape(depth, n_pool, page, W_B)
    v_pool = cache_diff_v.reshape(depth, n_pool, page, W_B)

    xp = x_prompt
    xs = jnp.pad(x_sample, ((0, 0), (0, ts_pad - ts), (0, 0)))
    prompt_chunk = math.gcd(tp, PROMPT_CHUNK)
    prompt_tile = _pick_tile(tp, 512)
    outs_p = [[] for _ in range(9)]
    outs_s = [[] for _ in range(7)]
    for l in range(depth):
        lam_init = 0.8 - 0.6 * math.exp(-0.3 * l)
        last = l == depth - 1
        p = _layer_weights(l, w)
        mk, mv = _mem_project(mem_prompt.reshape(bp * mt, D_MODEL), p['norm_mem_kv'], p['mem_wk'], p['mem_wv'])
        mk = mk.reshape(bp, mt, MEM_INNER)
        mv = mv.reshape(bp, mt, MEM_INNER)
        attn_p = functools.partial(_dattn_prompt, lam_par=p['lam_par'], subln=p['subln'], lam_init=lam_init)
        res = _trunk_layer(
            xp, tp, prompt_chunk, mk, mv,
            jnp.zeros((bp, GDN_CONV - 1, 3 * W_A), F32), jnp.zeros((bp, H_A, HEAD_DIM, HEAD_DIM), F32),
            jnp.zeros((bp, C_COLS), F32), jnp.zeros((bp, H_C, HEAD_DIM, HEAD_DIM), F32),
            jnp.zeros((bp, FFN_CONV - 1, D_FF), F32), attn_p, p, 1, 1, prompt_tile, last)
        xp = res[0]
        for acc, val in zip(outs_p, res[1:] + (mk, mv)):
            acc.append(val)
        attn_s = functools.partial(_dattn_paged, k_pool=k_pool, v_pool=v_pool, layer=l, page_table=page_table,
                                   lam_par=p['lam_par'], subln=p['subln'], lam_init=lam_init, t_valid=ts)
        res = _trunk_layer(
            xs, ts, ts_pad, cache_mem_k[l].reshape(bs, mt, MEM_INNER), cache_mem_v[l].reshape(bs, mt, MEM_INNER),
            state_gdn_conv[l], state_gdn[l], state_rwkv_shift[l], state_rwkv[l], state_ffn_conv[l], attn_s, p,
            min(GROUP_B, bs), bs, ts_pad, last)
        xs = res[0]
        for acc, val in zip(outs_s, res[1:]):
            acc.append(val)

    st = lambda vals, shape: jnp.stack(vals).reshape((depth,) + shape)
    p_k, p_v, p_gc, p_gs, p_rp, p_rs, p_fc, p_mk, p_mv = outs_p
    s_k, s_v, s_gc, s_gs, s_rp, s_rs, s_fc = outs_s
    kv_p = (bp, tp, H_B, 2 * DB)
    kv_s = (bs, ts, H_B, 2 * DB)
    mem_shape = (bp, mt, MEM_HEADS, MEM_HD)
    return (xp, xs[:, 0:ts],
            st(p_k, kv_p), st(p_v, kv_p), jnp.stack(p_gc), jnp.stack(p_gs), jnp.stack(p_rp), jnp.stack(p_rs),
            jnp.stack(p_fc), st(p_mk, mem_shape), st(p_mv, mem_shape),
            st(s_k, kv_s), st(s_v, kv_s), jnp.stack(s_gc), jnp.stack(s_gs), jnp.stack(s_rp), jnp.stack(s_rs),
            jnp.stack(s_fc))
```

```python
import functools
import math

import numpy as np
import jax
import jax.numpy as jnp
from jax import lax
from jax.experimental import pallas as pl
from jax.experimental.pallas import tpu as pltpu

F32 = jnp.float32
BF16 = jnp.bfloat16

D_MODEL = 1024
HEAD_DIM = 64
W_A = 384
W_B = 256
W_C = 384
H_A = W_A // HEAD_DIM
H_B = W_B // HEAD_DIM
H_C = W_C // HEAD_DIM
DB = HEAD_DIM // 2
GDN_CONV = 4
DECAY_LORA = 64
AAA_LORA = 64
GATE_LORA = 128
RWKV_GN_EPS = 64e-5
MEM_HEADS = 4
MEM_HD = 128
MEM_INNER = MEM_HEADS * MEM_HD
D_FF = 2816
FFN_CONV = 3
NORM_EPS = 1e-6
A_COLS = 4 * W_A + 2 * H_A
B_COLS = 3 * W_B
C_COLS = 3 * W_C + DECAY_LORA + AAA_LORA + GATE_LORA
GATE_PAD = 128
A_PAD = 4 * W_A + GATE_PAD

SUBLANES = 8
LANES = 128
VMEM_LIMIT = 56 * 1024 * 1024
GROUP_B = 8
PROMPT_CHUNK = 64
ROW_TILE = 512
ATT_TILE = 512
ATT_STRIP = 512
ATT_KEYS = 512
PAGES_PER_STEP = 32

_NN = (((1,), (0,)), ((), ()))
_NT = (((1,), (1,)), ((), ()))
_BNN = (((2,), (1,)), ((0,), (0,)))
_BNT = (((2,), (2,)), ((0,), (0,)))
_BTN = (((1,), (1,)), ((0,), (0,)))


def _dot(a, b, dn):
    return lax.dot_general(a.astype(BF16), b.astype(BF16), dn, preferred_element_type=F32)


def _dot_f32(a, b, dn):
    return lax.dot_general(a, b, dn, precision=lax.Precision.HIGHEST, preferred_element_type=F32)


def _sigmoid(x):
    return 1.0 / (1.0 + jnp.exp(-x))


def _softplus(x):
    return jnp.maximum(x, 0.0) + jnp.log(1.0 + jnp.exp(-jnp.abs(x)))


def _rms(x, g, eps=NORM_EPS):
    return x * lax.rsqrt(jnp.mean(x * x, axis=-1, keepdims=True) + eps) * g


def _pick_tile(n, pref):
    if n <= pref:
        return n
    t = pref - pref % SUBLANES
    while n % t:
        t -= SUBLANES
    return t


def _params(*sem):
    return pltpu.CompilerParams(dimension_semantics=sem, vmem_limit_bytes=VMEM_LIMIT)


def _const_spec(shape):
    nd = len(shape)
    return pl.BlockSpec(shape, lambda *_: (0,) * nd, pipeline_mode=pl.Buffered(1))


def _proj_in_kernel(x_ref, g_ref, wa_ref, wb_ref, wbt_ref, wc_ref, *outs, feature_major):
    bb, tt, _ = x_ref.shape
    m = bb * tt
    h = _rms(x_ref[...].reshape(m, D_MODEL), g_ref[...]).astype(BF16)
    cols = lambda w_ref, lo, n: jnp.dot(h, w_ref[:, lo:lo + n], preferred_element_type=F32)
    if feature_major:
        oa_ref, oqt_ref, ok_ref, okt_ref, ovt_ref, ovtb_ref, oc_ref = outs
        rows = lambda i: lax.dot_general(wbt_ref[i * W_B:(i + 1) * W_B, :], h, _NT, preferred_element_type=F32)
        oqt_ref[0] = (rows(0) * (DB ** -0.5)).astype(BF16)
        ok_ref[0] = cols(wb_ref, W_B, W_B).astype(BF16)
        okt_ref[0] = rows(1)
        vt = rows(2)
        ovt_ref[0] = vt
        ovtb_ref[0] = vt.astype(BF16)
    else:
        oa_ref, oq_ref, ok_ref, ov_ref, oc_ref = outs
        oq_ref[...] = cols(wb_ref, 0, W_B).reshape(bb, tt, W_B)
        ok_ref[...] = cols(wb_ref, W_B, W_B).reshape(bb, tt, W_B)
        ov_ref[...] = cols(wb_ref, 2 * W_B, W_B).reshape(bb, tt, W_B)
    oa_ref[...] = cols(wa_ref, 0, A_PAD).reshape(bb, tt, A_PAD)
    oc_ref[...] = cols(wc_ref, 0, C_COLS).reshape(bb, tt, C_COLS)


def _proj_in(x, g, wa, wb, wbt, wc, bb, tt, feature_major):
    b, t, _ = x.shape
    tok = lambda w: pl.BlockSpec((bb, tt, w), lambda bi, ti: (bi, ti, 0))
    feat = pl.BlockSpec((1, W_B, tt), lambda bi, ti: (bi, 0, ti))
    sds = jax.ShapeDtypeStruct
    if feature_major:
        assert bb == 1
        out_specs = [tok(A_PAD), feat, tok(W_B), feat, feat, feat, tok(C_COLS)]
        out_shape = [sds((b, t, A_PAD), F32), sds((b, W_B, t), BF16), sds((b, t, W_B), BF16),
                     sds((b, W_B, t), F32), sds((b, W_B, t), F32), sds((b, W_B, t), BF16), sds((b, t, C_COLS), F32)]
    else:
        out_specs = [tok(A_PAD), tok(W_B), tok(W_B), tok(W_B), tok(C_COLS)]
        out_shape = [sds((b, t, w), F32) for w in (A_PAD, W_B, W_B, W_B, C_COLS)]
    return pl.pallas_call(
        functools.partial(_proj_in_kernel, feature_major=feature_major),
        grid=(b // bb, t // tt),
        in_specs=[tok(D_MODEL), _const_spec(g.shape), _const_spec(wa.shape), _const_spec(wb.shape),
                  _const_spec(wbt.shape), _const_spec(wc.shape)],
        out_specs=out_specs,
        out_shape=out_shape,
        compiler_params=_params("arbitrary", "arbitrary"),
        name="proj_in",
    )(x, g, wa, wb, wbt, wc)


def _tri(c, inclusive):
    r = lax.broadcasted_iota(jnp.int32, (c, c), 0)
    q = lax.broadcasted_iota(jnp.int32, (c, c), 1)
    return (q <= r) if inclusive else (q < r)


def _chunk_cumsum(x, c):
    tri = jnp.broadcast_to(_tri(c, True).astype(F32), (x.shape[0], c, c))
    return _dot_f32(tri, x, _BNN)


def _neumann_solve(l, x, c):
    x = x + _dot(l, x, _BNN)
    p = l
    for _ in range(int(math.log2(c)) - 1):
        p = _dot(p, p, _BNN)
        x = x + _dot(p, x, _BNN)
    return x


def _gdn_kernel(a_ref, buf_ref, s0_ref, cw_ref, gp_ref, ng_ref, o_ref, s_ref, xp_ref, *, c, t_valid):
    ci = pl.program_id(1)
    hist = SUBLANES - (GDN_CONV - 1)

    @pl.when(ci == 0)
    def _():
        s_ref[...] = s0_ref[...]
        xp_ref[:, hist:SUBLANES, :] = buf_ref[...]

    xp_ref[:, SUBLANES:SUBLANES + c, :] = a_ref[:, :, 0:3 * W_A]
    y = xp_ref[:, hist:hist + c, :] * cw_ref[0:1, :]
    for j in range(1, GDN_CONV):
        y = y + xp_ref[:, hist + j:hist + j + c, :] * cw_ref[j:j + 1, :]
    xp_ref[:, hist:SUBLANES, :] = xp_ref[:, c + hist:c + SUBLANES, :]
    qkv = y * _sigmoid(y)

    gates = a_ref[:, :, 4 * W_A:A_PAD]
    tpos = ci * c + lax.broadcasted_iota(jnp.int32, (1, c, 1), 1)
    valid = tpos < t_valid
    beta_all = jnp.where(valid, _sigmoid(gates), 0.0)
    g_all = jnp.where(valid, -jnp.exp(gp_ref[0:1, :]) * _softplus(gates + gp_ref[1:2, :]), 0.0)
    gam_all = _chunk_cumsum(g_all, c)
    gam_t = jnp.swapaxes(gam_all, 1, 2)
    strict = _tri(c, False)
    incl = _tri(c, True)

    for h in range(H_A):
        lo = h * HEAD_DIM
        q = qkv[:, :, lo:lo + HEAD_DIM]
        k = qkv[:, :, W_A + lo:W_A + lo + HEAD_DIM]
        v = qkv[:, :, 2 * W_A + lo:2 * W_A + lo + HEAD_DIM]
        q = q * lax.rsqrt(jnp.sum(q * q, axis=-1, keepdims=True) + 1e-6) * (HEAD_DIM ** -0.5)
        k = k * lax.rsqrt(jnp.sum(k * k, axis=-1, keepdims=True) + 1e-6)
        beta = beta_all[:, :, h:h + 1]
        gcol = gam_all[:, :, H_A + h:H_A + h + 1]
        grow = gam_t[:, H_A + h:H_A + h + 1, :]
        glast = gcol[:, c - 1:c, :]
        diff = gcol - grow
        eg = jnp.exp(gcol)
        kb = k * beta
        a_mat = _dot(kb, k, _BNT) * jnp.exp(jnp.where(strict, diff, -jnp.inf))
        x = jnp.concatenate([v * beta, kb * eg], axis=-1)
        x = _neumann_solve(-a_mat, x, c)
        u = x[:, :, 0:HEAD_DIM]
        w = x[:, :, HEAD_DIM:2 * HEAD_DIM]
        qk = _dot(q, k, _BNT) * jnp.exp(jnp.where(incl, diff, -jnp.inf))
        s = s_ref[:, h]
        v_new = u - _dot(w, s, _BNN)
        o = _dot(q * eg, s, _BNN) + _dot(qk, v_new, _BNN)
        k_dec = k * jnp.exp(glast - gcol)
        s_ref[:, h] = s * jnp.exp(glast) + _dot(k_dec, v_new, _BTN)
        z = a_ref[:, :, 3 * W_A + lo:3 * W_A + lo + HEAD_DIM]
        o_ref[:, :, lo:lo + HEAD_DIM] = _rms(o, ng_ref[...]) * (z * _sigmoid(z))


def _gdn(a_cols, conv_buf, s0, conv_w, gate_par, norm_g, c, t_valid):
    b, t, _ = a_cols.shape
    g_b = min(GROUP_B, b)
    kern = functools.partial(_gdn_kernel, c=c, t_valid=t_valid)
    state = pl.BlockSpec((g_b, H_A, HEAD_DIM, HEAD_DIM), lambda bi, ci: (bi, 0, 0, 0))
    return pl.pallas_call(
        kern,
        grid=(b // g_b, t // c),
        in_specs=[pl.BlockSpec((g_b, c, A_PAD), lambda bi, ci: (bi, ci, 0)),
                  pl.BlockSpec((g_b, GDN_CONV - 1, 3 * W_A), lambda bi, ci: (bi, 0, 0)),
                  state, _const_spec(conv_w.shape), _const_spec(gate_par.shape), _const_spec(norm_g.shape)],
        out_specs=[pl.BlockSpec((g_b, c, W_A), lambda bi, ci: (bi, ci, 0)), state],
        out_shape=[jax.ShapeDtypeStruct((b, t, W_A), F32),
                   jax.ShapeDtypeStruct((b, H_A, HEAD_DIM, HEAD_DIM), F32)],
        scratch_shapes=[pltpu.VMEM((g_b, c + SUBLANES, 3 * W_A), F32)],
        compiler_params=_params("arbitrary", "arbitrary"),
        name="gdn_mixer",
    )(a_cols, conv_buf, s0, conv_w, gate_par, norm_g)


def _rwkv_kernel(c_ref, prev_ref, s0_ref, mu_ref, vec_ref, w2_ref, a2_ref, g2_ref, o_ref, s_ref, xp_ref,
                 *, c, t_valid):
    ci = pl.program_id(1)
    g_b = c_ref.shape[0]

    @pl.when(ci == 0)
    def _():
        s_ref[...] = s0_ref[...]
        xp_ref[:, SUBLANES - 1:SUBLANES, :] = prev_ref[...]

    x = c_ref[...]
    xp_ref[:, SUBLANES:SUBLANES + c, :] = x
    shifted = xp_ref[:, SUBLANES - 1:SUBLANES - 1 + c, :]
    xp_ref[:, SUBLANES - 1:SUBLANES, :] = xp_ref[:, c + SUBLANES - 1:c + SUBLANES, :]
    xs = (x + (shifted - x) * mu_ref[...]).reshape(g_b * c, C_COLS)

    w0, a0, k_k, k_a, r_k, lnx_w, lnx_b = (vec_ref[i:i + 1, :] for i in range(7))
    r = xs[:, 0:W_C]
    k = xs[:, W_C:2 * W_C]
    v = xs[:, 2 * W_C:3 * W_C]
    lora = xs[:, 3 * W_C:3 * W_C + DECAY_LORA + AAA_LORA]
    gd = xs[:, 3 * W_C + DECAY_LORA + AAA_LORA:C_COLS]
    lane = lax.broadcasted_iota(jnp.int32, (1, DECAY_LORA + AAA_LORA), 1)
    lora = jnp.where(lane < DECAY_LORA, jnp.tanh(lora), lora)
    w = -_softplus(-(w0 + _dot(lora, w2_ref[...], _NN))) - 0.5
    a = _sigmoid(a0 + _dot(lora, a2_ref[...], _NN))
    gate = _dot(_sigmoid(gd), g2_ref[...], _NN)
    kk = k * k_k
    k = k * (1.0 + (a - 1.0) * k_a)

    tpos = ci * c + lax.broadcasted_iota(jnp.int32, (1, c, 1), 1)
    valid = tpos < t_valid
    shape3 = (g_b, c, W_C)
    lw = jnp.where(valid, (-jnp.exp(w)).reshape(shape3), 0.0)
    cl = _chunk_cumsum(lw, c)
    p_in = jnp.exp(cl)
    p_inv = jnp.exp(-cl)
    p_ex = jnp.exp(cl - lw)
    r3, k3, v3, kk3, a3 = (z.reshape(shape3) for z in (r, k, v, kk, a))
    k3 = jnp.where(valid, k3, 0.0)
    kk3 = jnp.where(valid, kk3, 0.0)

    row = lax.broadcasted_iota(jnp.int32, (2 * c, 2 * c), 0)
    col = lax.broadcasted_iota(jnp.int32, (2 * c, 2 * c), 1)
    colc = jnp.where(col >= c, col - c, col)
    mask = colc < jnp.where(row < c, row, row - c + 1)
    second = lax.broadcasted_iota(jnp.int32, (1, 1, 2 * c), 2) >= c

    outs = []
    for h in range(H_C):
        lo = h * HEAD_DIM
        sl = slice(lo, lo + HEAD_DIM)
        rh, kh, vh, ah = r3[:, :, sl], k3[:, :, sl], v3[:, :, sl], a3[:, :, sl]
        kkh = kk3[:, :, sl]
        kkh = kkh * lax.rsqrt(jnp.sum(kkh * kkh, axis=-1, keepdims=True) + 1e-12)
        at = -kkh * p_ex[:, :, sl]
        bt = kkh * ah * p_inv[:, :, sl]
        kt = kh * p_inv[:, :, sl]
        rt = rh * p_in[:, :, sl]
        lhs = jnp.concatenate([at, rt], axis=1)
        rhs = jnp.concatenate([bt, kt], axis=1)
        gm = jnp.where(mask, _dot(lhs, rhs, _BNT), 0.0)
        top = gm[:, 0:c, :]
        bot = gm[:, c:2 * c, :]
        l_ab = top[:, :, 0:c]
        vv = jnp.concatenate([vh, vh], axis=1)
        s = s_ref[:, h]
        rhs_u = _dot(at, s, _BNT) + _dot(jnp.where(second, top, 0.0), vv, _BNN)
        u = _neumann_solve(l_ab, rhs_u, c)
        uv = jnp.concatenate([u, vh], axis=1)
        o = _dot(rt, s, _BNT) + _dot(bot, uv, _BNN)
        s_ref[:, h] = (s + _dot(uv, rhs, _BTN)) * p_in[:, c - 1:c, sl]
        mean = jnp.mean(o, axis=-1, keepdims=True)
        var = jnp.mean(jnp.square(o - mean), axis=-1, keepdims=True)
        o = (o - mean) * lax.rsqrt(var + RWKV_GN_EPS) * lnx_w[:, sl] + lnx_b[:, sl]
        bonus = jnp.sum(rh * k3[:, :, sl] * r_k[:, sl], axis=-1, keepdims=True) * vh
        outs.append(o + bonus)
    o_all = jnp.concatenate(outs, axis=-1)
    o_ref[...] = o_all * gate.reshape(shape3)


def _rwkv(c_cols, prev, s0, mu, vecs, w2p, a2p, g2, c, t_valid):
    b, t, _ = c_cols.shape
    g_b = min(GROUP_B, b)
    kern = functools.partial(_rwkv_kernel, c=c, t_valid=t_valid)
    state = pl.BlockSpec((g_b, H_C, HEAD_DIM, HEAD_DIM), lambda bi, ci: (bi, 0, 0, 0))
    return pl.pallas_call(
        kern,
        grid=(b // g_b, t // c),
        in_specs=[pl.BlockSpec((g_b, c, C_COLS), lambda bi, ci: (bi, ci, 0)),
                  pl.BlockSpec((g_b, 1, C_COLS), lambda bi, ci: (bi, 0, 0)),
                  state, _const_spec(mu.shape), _const_spec(vecs.shape), _const_spec(w2p.shape),
                  _const_spec(a2p.shape), _const_spec(g2.shape)],
        out_specs=[pl.BlockSpec((g_b, c, W_C), lambda bi, ci: (bi, ci, 0)), state],
        out_shape=[jax.ShapeDtypeStruct((b, t, W_C), F32),
                   jax.ShapeDtypeStruct((b, H_C, HEAD_DIM, HEAD_DIM), F32)],
        scratch_shapes=[pltpu.VMEM((g_b, c + SUBLANES, C_COLS), F32)],
        compiler_params=_params("arbitrary", "arbitrary"),
        name="rwkv_mixer",
    )(c_cols, prev, s0, mu, vecs, w2p, a2p, g2)


def _diff_lambda(lp_ref, lam_init):
    l1 = jnp.exp(jnp.sum(lp_ref[0:1, :] * lp_ref[1:2, :], axis=-1, keepdims=True))
    l2 = jnp.exp(jnp.sum(lp_ref[2:3, :] * lp_ref[3:4, :], axis=-1, keepdims=True))
    return l1 - l2 + lam_init


def _dattn_kernel(qi_ref, ki_ref, lp_ref, qt_ref, k_ref, vt_ref, sub_ref, o_ref, qm_ref, m_ref, l_ref, acc_ref,
                  *, lam_init):
    step = pl.program_id(1)
    qi = qi_ref[step]
    ki = ki_ref[step]
    tq = qt_ref.shape[2]
    tk = k_ref.shape[1]

    @pl.when(ki == 0)
    def _():
        m_ref[...] = jnp.full(m_ref.shape, -jnp.inf, F32)
        l_ref[...] = jnp.zeros(l_ref.shape, F32)
        acc_ref[...] = jnp.zeros(acc_ref.shape, F32)
        fmap = lax.broadcasted_iota(jnp.int32, (W_B, 1), 0) // DB
        qt = qt_ref[0]
        for hm in range(2 * H_B):
            qm_ref[hm] = jnp.where(fmap == hm, qt, jnp.zeros_like(qt))

    def tile(diagonal):
        for hm in range(2 * H_B):
            h = hm // 2
            for s in range(tq // ATT_STRIP):
                q0 = s * ATT_STRIP
                cs = slice(q0, q0 + ATT_STRIP)
                m = m_ref[hm, :, cs]
                l = l_ref[hm, :, cs]
                acc = acc_ref[hm, :, cs]
                for kb in range(tk // ATT_KEYS):
                    k0 = kb * ATT_KEYS
                    if diagonal and k0 > q0 + ATT_STRIP - 1:
                        continue
                    st = jnp.dot(k_ref[0, k0:k0 + ATT_KEYS, :], qm_ref[hm, :, cs], preferred_element_type=F32)
                    if diagonal and k0 + ATT_KEYS - 1 > q0:
                        kpos = k0 + lax.broadcasted_iota(jnp.int32, (ATT_KEYS, ATT_STRIP), 0)
                        qpos = q0 + lax.broadcasted_iota(jnp.int32, (ATT_KEYS, ATT_STRIP), 1)
                        st = jnp.where(kpos <= qpos, st, -jnp.inf)
                    m_new = jnp.maximum(m, jnp.max(st, axis=0, keepdims=True))
                    alpha = jnp.exp(m - m_new)
                    p = jnp.exp(st - m_new)
                    l = alpha * l + jnp.sum(p, axis=0, keepdims=True)
                    vth = vt_ref[0, h * HEAD_DIM:(h + 1) * HEAD_DIM, k0:k0 + ATT_KEYS]
                    acc = alpha * acc + jnp.dot(vth, p.astype(BF16), preferred_element_type=F32)
                    m = m_new
                m_ref[hm, :, cs] = m
                l_ref[hm, :, cs] = l
                acc_ref[hm, :, cs] = acc

    pl.when(ki < qi)(lambda: tile(False))

    @pl.when(ki == qi)
    def _():
        tile(True)
        lam = _diff_lambda(lp_ref, lam_init)
        heads = []
        for h in range(H_B):
            oh = acc_ref[2 * h] / l_ref[2 * h] - lam * (acc_ref[2 * h + 1] / l_ref[2 * h + 1])
            heads.append(oh * lax.rsqrt(jnp.mean(oh * oh, axis=0, keepdims=True) + NORM_EPS))
        o = jnp.concatenate(heads, axis=0).T
        o_ref[0] = o * sub_ref[...] * (1.0 - lam_init)


def _dattn_prompt(qt, k, vt, lam_par, subln, lam_init):
    b, _, t = qt.shape
    tile = _pick_tile(t, ATT_TILE)
    n = t // tile
    pairs = [(qi, ki) for qi in range(n) for ki in range(qi + 1)]
    qi_tab = jnp.asarray(np.array([p[0] for p in pairs], np.int32))
    ki_tab = jnp.asarray(np.array([p[1] for p in pairs], np.int32))
    const = lambda shape: pl.BlockSpec(shape, lambda bi, s, qi, ki: (0,) * len(shape))
    grid_spec = pltpu.PrefetchScalarGridSpec(
        num_scalar_prefetch=2,
        grid=(b, len(pairs)),
        in_specs=[const(lam_par.shape),
                  pl.BlockSpec((1, W_B, tile), lambda bi, s, qi, ki: (bi, 0, qi[s])),
                  pl.BlockSpec((1, tile, W_B), lambda bi, s, qi, ki: (bi, ki[s], 0)),
                  pl.BlockSpec((1, W_B, tile), lambda bi, s, qi, ki: (bi, 0, ki[s])),
                  const(subln.shape)],
        out_specs=pl.BlockSpec((1, tile, W_B), lambda bi, s, qi, ki: (bi, qi[s], 0)),
        scratch_shapes=[pltpu.VMEM((2 * H_B, W_B, tile), BF16), pltpu.VMEM((2 * H_B, 1, tile), F32),
                        pltpu.VMEM((2 * H_B, 1, tile), F32), pltpu.VMEM((2 * H_B, HEAD_DIM, tile), F32)],
    )
    return pl.pallas_call(
        functools.partial(_dattn_kernel, lam_init=lam_init),
        grid_spec=grid_spec,
        out_shape=jax.ShapeDtypeStruct((b, t, W_B), F32),
        compiler_params=_params("arbitrary", "arbitrary"),
        name="diff_attn_prompt",
    )(qi_tab, ki_tab, lam_par, qt, k, vt, subln)


def _softmax_rows(s, m_ref, l_ref):
    m_old = m_ref[...]
    m_new = jnp.maximum(m_old, jnp.max(s, axis=-1, keepdims=True))
    alpha = jnp.exp(m_old - m_new)
    p = jnp.exp(s - m_new)
    l_ref[...] = alpha * l_ref[...] + jnp.sum(p, axis=-1, keepdims=True)
    m_ref[...] = m_new
    return p, alpha


def _paged_kernel(pt_ref, lp_ref, q_ref, kn_ref, vn_ref, sub_ref, *refs, pages, lam_init, t_valid):
    k_pages = refs[0:pages]
    v_pages = refs[pages:2 * pages]
    o_ref, m_ref, l_ref, acc_ref = refs[2 * pages:]
    j = pl.program_id(1)
    tp = q_ref.shape[1]
    maps = lax.broadcasted_iota(jnp.int32, (1, W_B), 1) // DB

    @pl.when(j == 0)
    def _():
        m_ref[...] = jnp.full(m_ref.shape, -jnp.inf, F32)
        l_ref[...] = jnp.zeros(l_ref.shape, F32)
        acc_ref[...] = jnp.zeros(acc_ref.shape, F32)

    q = q_ref[0]
    qm = jnp.concatenate([jnp.where(maps == hm, q, 0.0) for hm in range(2 * H_B)], axis=0).astype(BF16)
    scale = DB ** -0.5

    kt = jnp.concatenate([kp[0, 0].astype(BF16) for kp in k_pages], axis=1)
    vt = jnp.concatenate([vp[0, 0].astype(BF16) for vp in v_pages], axis=1)
    s = jnp.dot(qm, kt, preferred_element_type=F32) * scale
    p, alpha = _softmax_rows(s, m_ref, l_ref)
    acc_ref[...] = alpha * acc_ref[...] + _dot(p, vt, _NT)

    @pl.when(j == pl.num_programs(1) - 1)
    def _():
        rows = 2 * H_B * tp
        tq_pos = lax.broadcasted_iota(jnp.int32, (rows, tp), 0) % tp
        tk_pos = lax.broadcasted_iota(jnp.int32, (rows, tp), 1)
        s_new = _dot(qm, kn_ref[0], _NT) * scale
        s_new = jnp.where((tk_pos <= tq_pos) & (tk_pos < t_valid), s_new, -jnp.inf)
        p_new, alpha_new = _softmax_rows(s_new, m_ref, l_ref)
        res = (alpha_new * acc_ref[...] + _dot(p_new, vn_ref[0], _NN)) / l_ref[...]
        lam = _diff_lambda(lp_ref, lam_init)
        heads = lax.broadcasted_iota(jnp.int32, (1, W_B), 1) // HEAD_DIM
        o = jnp.zeros((tp, W_B), F32)
        for h in range(H_B):
            oh = res[2 * h * tp:(2 * h + 1) * tp] - lam * res[(2 * h + 1) * tp:(2 * h + 2) * tp]
            o = jnp.where(heads == h, oh, o)
        r = lax.broadcasted_iota(jnp.int32, (W_B, W_B), 0) // HEAD_DIM
        c = lax.broadcasted_iota(jnp.int32, (W_B, W_B), 1) // HEAD_DIM
        ms = _dot_f32(o * o, jnp.where(r == c, 1.0 / HEAD_DIM, 0.0).astype(F32), _NN)
        o_ref[0] = o * lax.rsqrt(ms + NORM_EPS) * sub_ref[...] * (1.0 - lam_init)


def _dattn_paged(q, k, v, kt_pool, vt_pool, layer, page_table, lam_par, subln, lam_init, t_valid):
    b, tp, _ = q.shape
    n_pages = page_table.shape[1]
    page = kt_pool.shape[3]
    pages = math.gcd(n_pages, PAGES_PER_STEP)
    rows = 2 * H_B * tp
    kern = functools.partial(_paged_kernel, pages=pages, lam_init=lam_init, t_valid=t_valid)

    def page_spec(i):
        return pl.BlockSpec((1, 1, W_B, page), lambda bi, j, pt: (layer, pt[bi, j * pages + i], 0, 0))

    tok = pl.BlockSpec((1, tp, W_B), lambda bi, j, pt: (bi, 0, 0))
    const = lambda shape: pl.BlockSpec(shape, lambda bi, j, pt: (0,) * len(shape))
    grid_spec = pltpu.PrefetchScalarGridSpec(
        num_scalar_prefetch=1,
        grid=(b, n_pages // pages),
        in_specs=[const(lam_par.shape), tok, tok, tok, const(subln.shape)]
        + [page_spec(i) for i in range(pages)] * 2,
        out_specs=tok,
        scratch_shapes=[pltpu.VMEM((rows, 1), F32), pltpu.VMEM((rows, 1), F32), pltpu.VMEM((rows, W_B), F32)],
    )
    return pl.pallas_call(
        kern,
        grid_spec=grid_spec,
        out_shape=jax.ShapeDtypeStruct((b, tp, W_B), F32),
        compiler_params=_params("arbitrary", "arbitrary"),
        name="diff_attn_paged",
    )(page_table, lam_par, q, k, v, subln, *([kt_pool] * pages), *([vt_pool] * pages))


def _mem_project_kernel(m_ref, g_ref, wk_ref, wv_ref, k_ref, v_ref):
    h = _rms(m_ref[...], g_ref[...]).astype(BF16)
    k_ref[...] = jnp.dot(h, wk_ref[...], preferred_element_type=F32)
    v_ref[...] = jnp.dot(h, wv_ref[...], preferred_element_type=F32)


def _mem_project(mem2d, g, wk, wv):
    n = mem2d.shape[0]
    tm = _pick_tile(n, ROW_TILE)
    return pl.pallas_call(
        _mem_project_kernel,
        grid=(n // tm,),
        in_specs=[pl.BlockSpec((tm, D_MODEL), lambda i: (i, 0)), _const_spec(g.shape), _const_spec(wk.shape),
                  _const_spec(wv.shape)],
        out_specs=[pl.BlockSpec((tm, MEM_INNER), lambda i: (i, 0))] * 2,
        out_shape=[jax.ShapeDtypeStruct((n, MEM_INNER), F32)] * 2,
        compiler_params=_params("arbitrary"),
        name="mem_project",
    )(mem2d, g, wk, wv)


def _mix_mem_kernel(x_ref, oa_ref, ob_ref, oc_ref, wout_ref, g_ref, wq_ref, mk_ref, mv_ref, wo_ref, out_ref):
    bb, tt, _ = x_ref.shape
    m = bb * tt
    x = x_ref[...].reshape(m, D_MODEL)
    mix = (_dot(oa_ref[...].reshape(m, W_A), wout_ref[0:W_A, :], _NN)
           + _dot(ob_ref[...].reshape(m, W_B), wout_ref[W_A:W_A + W_B, :], _NN)
           + _dot(oc_ref[...].reshape(m, W_C), wout_ref[W_A + W_B:D_MODEL, :], _NN))
    x = x + mix
    q = _dot(_rms(x, g_ref[...]), wq_ref[...], _NN)
    heads = []
    for h in range(MEM_HEADS):
        sl = slice(h * MEM_HD, (h + 1) * MEM_HD)
        qh = q[:, sl].reshape(bb, tt, MEM_HD)
        s = _dot(qh, mk_ref[:, :, sl], _BNT) * (MEM_HD ** -0.5)
        p = jnp.exp(s - jnp.max(s, axis=-1, keepdims=True))
        p = p / jnp.sum(p, axis=-1, keepdims=True)
        heads.append(_dot(p, mv_ref[:, :, sl], _BNN).reshape(m, MEM_HD))
    o = jnp.concatenate(heads, axis=-1)
    out_ref[...] = (x + _dot(o, wo_ref[...], _NN)).reshape(bb, tt, D_MODEL)


def _mix_mem(x, o_a, o_b, o_c, w_out, g, wq, mem_k, mem_v, wo, bb, tt):
    b, t, _ = x.shape
    mt = mem_k.shape[1]
    blk = lambda w: pl.BlockSpec((bb, tt, w), lambda bi, ti: (bi, ti, 0))
    mem = pl.BlockSpec((bb, mt, MEM_INNER), lambda bi, ti: (bi, 0, 0))
    return pl.pallas_call(
        _mix_mem_kernel,
        grid=(b // bb, t // tt),
        in_specs=[blk(D_MODEL), blk(W_A), blk(W_B), blk(W_C), _const_spec(w_out.shape), _const_spec(g.shape),
                  _const_spec(wq.shape), mem, mem, _const_spec(wo.shape)],
        out_specs=blk(D_MODEL),
        out_shape=jax.ShapeDtypeStruct((b, t, D_MODEL), F32),
        compiler_params=_params("arbitrary", "arbitrary"),
        name="mix_out_mem_attn",
    )(x, o_a, o_b, o_c, w_out, g, wq, mem_k, mem_v, wo)


FF_CHUNK = 256


def _ffn_kernel(x_ref, buf_ref, g_ref, wg_ref, wv_ref, cw_ref, cb_ref, wo_ref, gf_ref, out_ref, tail_ref,
                carry_ref, gp_ref, acc_ref, *, final_norm):
    ti = pl.program_id(1)
    bb, tt, _ = x_ref.shape
    m = bb * tt
    hist = SUBLANES - (FFN_CONV - 1)

    @pl.when(ti == 0)
    def _():
        carry_ref[:, hist:SUBLANES, :] = buf_ref[...]

    x = x_ref[...].reshape(m, D_MODEL)
    h = _rms(x, g_ref[...]).astype(BF16)
    for ck in range(D_FF // FF_CHUNK):
        sl = slice(ck * FF_CHUNK, (ck + 1) * FF_CHUNK)
        gate = jnp.dot(h, wg_ref[:, sl], preferred_element_type=F32).reshape(bb, tt, FF_CHUNK)
        val = jnp.dot(h, wv_ref[:, sl], preferred_element_type=F32)
        gp_ref[:, hist:SUBLANES, :] = carry_ref[:, hist:SUBLANES, sl]
        gp_ref[:, SUBLANES:SUBLANES + tt, :] = gate
        y = cb_ref[:, sl] + gate * cw_ref[FFN_CONV - 1:FFN_CONV, sl]
        for j in range(FFN_CONV - 1):
            y = y + gp_ref[:, hist + j:hist + j + tt, :] * cw_ref[j:j + 1, sl]
        carry_ref[:, :, sl] = gate[:, tt - SUBLANES:tt, :]
        y = y.reshape(m, FF_CHUNK)
        act = (y * _sigmoid(y) * val).astype(BF16)
        part = jnp.dot(act, wo_ref[sl, :], preferred_element_type=F32)
        if ck == 0:
            acc_ref[...] = x + part
        else:
            acc_ref[...] += part
    tail_ref[...] = carry_ref[...]
    y = acc_ref[...]
    if final_norm:
        y = _rms(y, gf_ref[...])
    out_ref[...] = y.reshape(bb, tt, D_MODEL)


def _ffn(x, buf, g, wg, wv, conv_w, conv_b, wo, g_final, bb, tt, final_norm):
    b, t, _ = x.shape
    kern = functools.partial(_ffn_kernel, final_norm=final_norm)
    blk = pl.BlockSpec((bb, tt, D_MODEL), lambda bi, ti: (bi, ti, 0))
    return pl.pallas_call(
        kern,
        grid=(b // bb, t // tt),
        in_specs=[blk, pl.BlockSpec((bb, FFN_CONV - 1, D_FF), lambda bi, ti: (bi, 0, 0)), _const_spec(g.shape),
                  _const_spec(wg.shape), _const_spec(wv.shape), _const_spec(conv_w.shape),
                  _const_spec(conv_b.shape), _const_spec(wo.shape), _const_spec(g_final.shape)],
        out_specs=[blk, pl.BlockSpec((bb, SUBLANES, D_FF), lambda bi, ti: (bi, 0, 0))],
        out_shape=[jax.ShapeDtypeStruct((b, t, D_MODEL), F32), jax.ShapeDtypeStruct((b, SUBLANES, D_FF), F32)],
        scratch_shapes=[pltpu.VMEM((bb, SUBLANES, D_FF), F32), pltpu.VMEM((bb, tt + SUBLANES, FF_CHUNK), F32),
                        pltpu.VMEM((bb * tt, D_MODEL), F32)],
        compiler_params=_params("arbitrary", "arbitrary"),
        name="conv_glu_ffn",
    )(x, buf, g, wg, wv, conv_w, conv_b, wo, g_final)


def _layer_weights(l, w):
    bf = lambda z: z.astype(BF16)
    row = lambda z: z.reshape(1, -1)
    w_in = w['w_in'][l]
    gates = jnp.pad(w_in[:, 4 * W_A:A_COLS], ((0, 0), (0, GATE_PAD - 2 * H_A)))
    pad_gate = lambda z: jnp.pad(z, (H_A, GATE_PAD - 2 * H_A)).reshape(1, GATE_PAD)
    zeros_lora = jnp.zeros((DECAY_LORA, W_C), F32)
    wb = w_in[:, A_COLS:A_COLS + B_COLS]
    return dict(
        norm_mix=row(w['norm_mix'][l]),
        wa=bf(jnp.concatenate([w_in[:, 0:4 * W_A], gates], axis=1)),
        wb=bf(wb), wbt=bf(wb.T),
        wc=bf(w_in[:, A_COLS + B_COLS:]),
        gdn_conv_w=w['gdn_conv_w'][l],
        gdn_gate=jnp.concatenate([pad_gate(w['gdn_a_log'][l]), pad_gate(w['gdn_dt_bias'][l])], axis=0),
        gdn_norm=row(w['gdn_norm'][l]),
        lam_par=jnp.stack([w['diff_lq1'][l], w['diff_lk1'][l], w['diff_lq2'][l], w['diff_lk2'][l]]),
        subln=row(jnp.tile(w['diff_subln'][l], H_B)),
        rw_mu=row(w['rw_mu'][l]),
        rw_vecs=jnp.stack([w['rw_w0'][l], w['rw_a0'][l], w['rw_k_k'][l], w['rw_k_a'][l],
                           w['rw_r_k'][l].reshape(-1), w['rw_lnx_w'][l], w['rw_lnx_b'][l],
                           jnp.zeros((W_C,), F32)]),
        rw_w2=bf(jnp.concatenate([w['rw_w2'][l], zeros_lora], axis=0)),
        rw_a2=bf(jnp.concatenate([zeros_lora, w['rw_a2'][l]], axis=0)),
        rw_g2=bf(w['rw_g2'][l]),
        w_out=bf(w['w_out'][l]),
        norm_mem=row(w['norm_mem'][l]),
        norm_mem_kv=row(w['norm_mem_kv'][l]),
        mem_wq=bf(w['mem_wq'][l]), mem_wk=bf(w['mem_wk'][l]), mem_wv=bf(w['mem_wv'][l]),
        mem_wo=bf(w['mem_wo'][l]),
        norm_ffn=row(w['norm_ffn'][l]),
        ffn_wg=bf(w['ffn_w_in'][l][:, 0:D_FF]), ffn_wv=bf(w['ffn_w_in'][l][:, D_FF:]),
        ffn_conv_w=w['ffn_conv_w'][l], ffn_conv_b=row(w['ffn_conv_b'][l]),
        ffn_wo=bf(w['ffn_w_out'][l]),
        norm_final=row(w['norm_final']),
    )


def _trunk_layer(x, t_valid, chunk, mem_k, mem_v, gdn_buf, gdn_s, rw_prev, rw_s, ffn_buf, attn_fn, p,
                 proj_bb, mem_bb, ffn_bb, tile_t, final_norm, feature_major):
    b, t, _ = x.shape
    proj = _proj_in(x, p['norm_mix'], p['wa'], p['wb'], p['wbt'], p['wc'], proj_bb, tile_t, feature_major)
    if feature_major:
        a_cols, qt, k_tok, k_new, v_new, vt, c_cols = proj
        o_b = attn_fn(qt, k_tok, vt)
    else:
        a_cols, q_b, k_b, v_b, c_cols = proj
        o_b = attn_fn(q_b, k_b, v_b)
        k_new, v_new = k_b[:, 0:t_valid], v_b[:, 0:t_valid]
    o_a, gdn_s_new = _gdn(a_cols, gdn_buf, gdn_s, p['gdn_conv_w'], p['gdn_gate'], p['gdn_norm'], chunk, t_valid)
    o_c, rw_s_new = _rwkv(c_cols, rw_prev.reshape(b, 1, C_COLS), rw_s, p['rw_mu'], p['rw_vecs'], p['rw_w2'],
                          p['rw_a2'], p['rw_g2'], chunk, t_valid)
    x = _mix_mem(x, o_a, o_b, o_c, p['w_out'], p['norm_mem'], p['mem_wq'], mem_k, mem_v, p['mem_wo'],
                 mem_bb, tile_t)
    x, tail = _ffn(x, ffn_buf, p['norm_ffn'], p['ffn_wg'], p['ffn_wv'], p['ffn_conv_w'], p['ffn_conv_b'],
                   p['ffn_wo'], p['norm_final'], ffn_bb, tile_t, final_norm)
    keep = min(t_valid, GDN_CONV - 1)
    gdn_buf_new = jnp.concatenate([gdn_buf[:, keep:], a_cols[:, t_valid - keep:t_valid, 0:3 * W_A]], axis=1)
    last = t_valid - (t - SUBLANES)
    ffn_buf_new = tail[:, last - (FFN_CONV - 1):last]
    return (x, k_new, v_new, gdn_buf_new, gdn_s_new, c_cols[:, t_valid - 1], rw_s_new, ffn_buf_new)


def kernel(x_prompt, x_sample, cache_diff_k, cache_diff_v, state_gdn_conv, state_gdn, state_rwkv_shift, state_rwkv, state_ffn_conv, cache_mem_k, cache_mem_v, page_table, mem_prompt, norm_mix, w_in, gdn_conv_w, gdn_a_log, gdn_dt_bias, gdn_norm, diff_lq1, diff_lk1, diff_lq2, diff_lk2, diff_subln, rw_mu, rw_w0, rw_w2, rw_a0, rw_a2, rw_g2, rw_k_k, rw_k_a, rw_r_k, rw_lnx_w, rw_lnx_b, w_out, norm_mem, norm_mem_kv, mem_wq, mem_wk, mem_wv, mem_wo, norm_ffn, ffn_w_in, ffn_conv_w, ffn_conv_b, ffn_w_out, norm_final):
    w = dict(norm_mix=norm_mix, w_in=w_in, gdn_conv_w=gdn_conv_w, gdn_a_log=gdn_a_log, gdn_dt_bias=gdn_dt_bias,
             gdn_norm=gdn_norm, diff_lq1=diff_lq1, diff_lk1=diff_lk1, diff_lq2=diff_lq2, diff_lk2=diff_lk2,
             diff_subln=diff_subln, rw_mu=rw_mu, rw_w0=rw_w0, rw_w2=rw_w2, rw_a0=rw_a0, rw_a2=rw_a2, rw_g2=rw_g2,
             rw_k_k=rw_k_k, rw_k_a=rw_k_a, rw_r_k=rw_r_k, rw_lnx_w=rw_lnx_w, rw_lnx_b=rw_lnx_b, w_out=w_out,
             norm_mem=norm_mem, norm_mem_kv=norm_mem_kv, mem_wq=mem_wq, mem_wk=mem_wk, mem_wv=mem_wv,
             mem_wo=mem_wo, norm_ffn=norm_ffn, ffn_w_in=ffn_w_in, ffn_conv_w=ffn_conv_w, ffn_conv_b=ffn_conv_b,
             ffn_w_out=ffn_w_out, norm_final=norm_final)
    depth = w_in.shape[0]
    bp, tp, _ = x_prompt.shape
    bs, ts, _ = x_sample.shape
    mt = mem_prompt.shape[1]
    ts_pad = -(-ts // SUBLANES) * SUBLANES
    n_pool, page = cache_diff_k.shape[1], cache_diff_k.shape[2]
    pool_view = lambda z: jnp.transpose(z, (0, 1, 3, 4, 2)).reshape(depth, n_pool, W_B, page)
    kt_pool = pool_view(cache_diff_k)
    vt_pool = pool_view(cache_diff_v)

    xp = x_prompt
    xs = jnp.pad(x_sample, ((0, 0), (0, ts_pad - ts), (0, 0)))
    prompt_chunk = math.gcd(tp, PROMPT_CHUNK)
    prompt_tile = _pick_tile(tp, ROW_TILE)
    outs_p = [[] for _ in range(9)]
    outs_s = [[] for _ in range(7)]
    for l in range(depth):
        lam_init = 0.8 - 0.6 * math.exp(-0.3 * l)
        last = l == depth - 1
        p = _layer_weights(l, w)
        mk, mv = _mem_project(mem_prompt.reshape(bp * mt, D_MODEL), p['norm_mem_kv'], p['mem_wk'], p['mem_wv'])
        mk = mk.reshape(bp, mt, MEM_INNER)
        mv = mv.reshape(bp, mt, MEM_INNER)
        attn_p = functools.partial(_dattn_prompt, lam_par=p['lam_par'], subln=p['subln'], lam_init=lam_init)
        res = _trunk_layer(
            xp, tp, prompt_chunk, mk, mv,
            jnp.zeros((bp, GDN_CONV - 1, 3 * W_A), F32), jnp.zeros((bp, H_A, HEAD_DIM, HEAD_DIM), F32),
            jnp.zeros((bp, C_COLS), F32), jnp.zeros((bp, H_C, HEAD_DIM, HEAD_DIM), F32),
            jnp.zeros((bp, FFN_CONV - 1, D_FF), F32), attn_p, p, 1, 1, 1, prompt_tile, last, True)
        xp = res[0]
        for acc, val in zip(outs_p, res[1:] + (mk, mv)):
            acc.append(val)
        attn_s = functools.partial(_dattn_paged, kt_pool=kt_pool, vt_pool=vt_pool, layer=l, page_table=page_table,
                                   lam_par=p['lam_par'], subln=p['subln'], lam_init=lam_init, t_valid=ts)
        res = _trunk_layer(
            xs, ts, ts_pad, cache_mem_k[l].reshape(bs, mt, MEM_INNER), cache_mem_v[l].reshape(bs, mt, MEM_INNER),
            state_gdn_conv[l], state_gdn[l], state_rwkv_shift[l], state_rwkv[l], state_ffn_conv[l], attn_s, p,
            bs, min(GROUP_B, bs), bs, ts_pad, last, False)
        xs = res[0]
        for acc, val in zip(outs_s, res[1:]):
            acc.append(val)

    st = lambda vals, shape: jnp.stack(vals).reshape((depth,) + shape)
    p_k, p_v, p_gc, p_gs, p_rp, p_rs, p_fc, p_mk, p_mv = outs_p
    s_k, s_v, s_gc, s_gs, s_rp, s_rs, s_fc = outs_s
    kv_p = lambda vals: jnp.transpose(jnp.stack(vals).reshape(depth, bp, H_B, 2 * DB, tp), (0, 1, 4, 2, 3))
    kv_s = (bs, ts, H_B, 2 * DB)
    mem_shape = (bp, mt, MEM_HEADS, MEM_HD)
    return (xp, xs[:, 0:ts],
            kv_p(p_k), kv_p(p_v), jnp.stack(p_gc), jnp.stack(p_gs), jnp.stack(p_rp), jnp.stack(p_rs),
            jnp.stack(p_fc), st(p_mk, mem_shape), st(p_mv, mem_shape),
            st(s_k, kv_s), st(s_v, kv_s), jnp.stack(s_gc), jnp.stack(s_gs), jnp.stack(s_rp), jnp.stack(s_rs),
            jnp.stack(s_fc))
```

```python
import functools
import math

import numpy as np
import jax
import jax.numpy as jnp
from jax import lax
from jax.experimental import pallas as pl
from jax.experimental.pallas import tpu as pltpu

F32 = jnp.float32
BF16 = jnp.bfloat16

D_MODEL = 1024
HEAD_DIM = 64
W_A = 384
W_B = 256
W_C = 384
H_A = W_A // HEAD_DIM
H_B = W_B // HEAD_DIM
H_C = W_C // HEAD_DIM
DB = HEAD_DIM // 2
GDN_CONV = 4
DECAY_LORA = 64
AAA_LORA = 64
GATE_LORA = 128
RWKV_GN_EPS = 64e-5
MEM_HEADS = 4
MEM_HD = 128
MEM_INNER = MEM_HEADS * MEM_HD
D_FF = 2816
FFN_CONV = 3
NORM_EPS = 1e-6
A_COLS = 4 * W_A + 2 * H_A
B_COLS = 3 * W_B
C_COLS = 3 * W_C + DECAY_LORA + AAA_LORA + GATE_LORA
GATE_PAD = 128
A_PAD = 4 * W_A + GATE_PAD

SUBLANES = 8
LANES = 128
VMEM_LIMIT = 56 * 1024 * 1024
GROUP_B = 8
PROMPT_CHUNK = 64
ROW_TILE = 512
ATT_TILE = 1024
ATT_STRIP = 1024
ATT_KEYS = 1024
PAGES_PER_STEP = 32

_NN = (((1,), (0,)), ((), ()))
_NT = (((1,), (1,)), ((), ()))
_BNN = (((2,), (1,)), ((0,), (0,)))
_BNT = (((2,), (2,)), ((0,), (0,)))
_BTN = (((1,), (1,)), ((0,), (0,)))


def _dot(a, b, dn):
    return lax.dot_general(a.astype(BF16), b.astype(BF16), dn, preferred_element_type=F32)


def _dot_f32(a, b, dn):
    return lax.dot_general(a, b, dn, precision=lax.Precision.HIGHEST, preferred_element_type=F32)


def _sigmoid(x):
    return 1.0 / (1.0 + jnp.exp(-x))


def _softplus(x):
    return jnp.maximum(x, 0.0) + jnp.log(1.0 + jnp.exp(-jnp.abs(x)))


def _rms(x, g, eps=NORM_EPS):
    return x * lax.rsqrt(jnp.mean(x * x, axis=-1, keepdims=True) + eps) * g


def _pick_tile(n, pref):
    if n <= pref:
        return n
    t = pref - pref % SUBLANES
    while n % t:
        t -= SUBLANES
    return t


def _params(*sem):
    return pltpu.CompilerParams(dimension_semantics=sem, vmem_limit_bytes=VMEM_LIMIT)


def _const_spec(shape):
    nd = len(shape)
    return pl.BlockSpec(shape, lambda *_: (0,) * nd, pipeline_mode=pl.Buffered(1))


def _proj_in_kernel(x_ref, g_ref, wa32_ref, wb32_ref, wbt32_ref, wc32_ref, *refs, feature_major):
    outs, (wa_ref, wb_ref, wbt_ref, wc_ref) = refs[:-4], refs[-4:]

    @pl.when((pl.program_id(0) == 0) & (pl.program_id(1) == 0))
    def _():
        wa_ref[...] = wa32_ref[...].astype(BF16)
        wb_ref[...] = wb32_ref[...].astype(BF16)
        wbt_ref[...] = wbt32_ref[...].astype(BF16)
        wc_ref[...] = wc32_ref[...].astype(BF16)

    bb, tt, _ = x_ref.shape
    m = bb * tt
    h = _rms(x_ref[...].reshape(m, D_MODEL), g_ref[...]).astype(BF16)
    cols = lambda w_ref, lo, n: jnp.dot(h, w_ref[:, lo:lo + n], preferred_element_type=F32)
    if feature_major:
        oa_ref, oqt_ref, ok_ref, okt_ref, ovt_ref, ovtb_ref, oc_ref = outs
        rows = lambda i: lax.dot_general(wbt_ref[i * W_B:(i + 1) * W_B, :], h, _NT, preferred_element_type=F32)
        oqt_ref[0] = (rows(0) * (DB ** -0.5)).astype(BF16)
        ok_ref[0] = cols(wb_ref, W_B, W_B).astype(BF16)
        okt_ref[0] = rows(1)
        vt = rows(2)
        ovt_ref[0] = vt
        ovtb_ref[0] = vt.astype(BF16)
    else:
        oa_ref, oq_ref, ok_ref, ov_ref, oc_ref = outs
        oq_ref[...] = cols(wb_ref, 0, W_B).reshape(bb, tt, W_B)
        ok_ref[...] = cols(wb_ref, W_B, W_B).reshape(bb, tt, W_B)
        ov_ref[...] = cols(wb_ref, 2 * W_B, W_B).reshape(bb, tt, W_B)
    oa_ref[...] = cols(wa_ref, 0, A_PAD).reshape(bb, tt, A_PAD)
    oc_ref[...] = cols(wc_ref, 0, C_COLS).reshape(bb, tt, C_COLS)


def _proj_in(x, g, wa, wb, wbt, wc, bb, tt, feature_major):
    b, t, _ = x.shape
    tok = lambda w: pl.BlockSpec((bb, tt, w), lambda bi, ti: (bi, ti, 0))
    feat = pl.BlockSpec((1, W_B, tt), lambda bi, ti: (bi, 0, ti))
    sds = jax.ShapeDtypeStruct
    if feature_major:
        assert bb == 1
        out_specs = [tok(A_PAD), feat, tok(W_B), feat, feat, feat, tok(C_COLS)]
        out_shape = [sds((b, t, A_PAD), F32), sds((b, W_B, t), BF16), sds((b, t, W_B), BF16),
                     sds((b, W_B, t), F32), sds((b, W_B, t), F32), sds((b, W_B, t), BF16), sds((b, t, C_COLS), F32)]
    else:
        out_specs = [tok(A_PAD), tok(W_B), tok(W_B), tok(W_B), tok(C_COLS)]
        out_shape = [sds((b, t, w), F32) for w in (A_PAD, W_B, W_B, W_B, C_COLS)]
    return pl.pallas_call(
        functools.partial(_proj_in_kernel, feature_major=feature_major),
        grid=(b // bb, t // tt),
        in_specs=[tok(D_MODEL), _const_spec(g.shape), _const_spec(wa.shape), _const_spec(wb.shape),
                  _const_spec(wbt.shape), _const_spec(wc.shape)],
        out_specs=out_specs,
        out_shape=out_shape,
        scratch_shapes=[pltpu.VMEM(z.shape, BF16) for z in (wa, wb, wbt, wc)],
        compiler_params=_params("arbitrary", "arbitrary"),
        name="proj_in",
    )(x, g, wa, wb, wbt, wc)


def _tri(c, inclusive):
    r = lax.broadcasted_iota(jnp.int32, (c, c), 0)
    q = lax.broadcasted_iota(jnp.int32, (c, c), 1)
    return (q <= r) if inclusive else (q < r)


def _chunk_cumsum(x, c):
    tri = jnp.broadcast_to(_tri(c, True).astype(F32), (x.shape[0], c, c))
    return _dot_f32(tri, x, _BNN)


def _neumann_solve(l, x, c):
    x = x + _dot(l, x, _BNN)
    p = l
    for _ in range(int(math.log2(c)) - 1):
        p = _dot(p, p, _BNN)
        x = x + _dot(p, x, _BNN)
    return x


def _gdn_kernel(a_ref, buf_ref, s0_ref, cw_ref, gp_ref, ng_ref, o_ref, s_ref, xp_ref, *, c, t_valid):
    ci = pl.program_id(1)
    hist = SUBLANES - (GDN_CONV - 1)

    @pl.when(ci == 0)
    def _():
        s_ref[...] = s0_ref[...]
        xp_ref[:, hist:SUBLANES, :] = buf_ref[...]

    xp_ref[:, SUBLANES:SUBLANES + c, :] = a_ref[:, :, 0:3 * W_A]
    y = xp_ref[:, hist:hist + c, :] * cw_ref[0:1, :]
    for j in range(1, GDN_CONV):
        y = y + xp_ref[:, hist + j:hist + j + c, :] * cw_ref[j:j + 1, :]
    xp_ref[:, hist:SUBLANES, :] = xp_ref[:, c + hist:c + SUBLANES, :]
    qkv = y * _sigmoid(y)

    gates = a_ref[:, :, 4 * W_A:A_PAD]
    tpos = ci * c + lax.broadcasted_iota(jnp.int32, (1, c, 1), 1)
    valid = tpos < t_valid
    beta_all = jnp.where(valid, _sigmoid(gates), 0.0)
    g_all = jnp.where(valid, -jnp.exp(gp_ref[0:1, :]) * _softplus(gates + gp_ref[1:2, :]), 0.0)
    gam_all = _chunk_cumsum(g_all, c)
    gam_t = jnp.swapaxes(gam_all, 1, 2)
    strict = _tri(c, False)
    incl = _tri(c, True)

    for h in range(H_A):
        lo = h * HEAD_DIM
        q = qkv[:, :, lo:lo + HEAD_DIM]
        k = qkv[:, :, W_A + lo:W_A + lo + HEAD_DIM]
        v = qkv[:, :, 2 * W_A + lo:2 * W_A + lo + HEAD_DIM]
        q = q * lax.rsqrt(jnp.sum(q * q, axis=-1, keepdims=True) + 1e-6) * (HEAD_DIM ** -0.5)
        k = k * lax.rsqrt(jnp.sum(k * k, axis=-1, keepdims=True) + 1e-6)
        beta = beta_all[:, :, h:h + 1]
        gcol = gam_all[:, :, H_A + h:H_A + h + 1]
        grow = gam_t[:, H_A + h:H_A + h + 1, :]
        glast = gcol[:, c - 1:c, :]
        diff = gcol - grow
        eg = jnp.exp(gcol)
        kb = k * beta
        a_mat = _dot(kb, k, _BNT) * jnp.exp(jnp.where(strict, diff, -jnp.inf))
        x = jnp.concatenate([v * beta, kb * eg], axis=-1)
        x = _neumann_solve(-a_mat, x, c)
        u = x[:, :, 0:HEAD_DIM]
        w = x[:, :, HEAD_DIM:2 * HEAD_DIM]
        qk = _dot(q, k, _BNT) * jnp.exp(jnp.where(incl, diff, -jnp.inf))
        s = s_ref[:, h]
        v_new = u - _dot(w, s, _BNN)
        o = _dot(q * eg, s, _BNN) + _dot(qk, v_new, _BNN)
        k_dec = k * jnp.exp(glast - gcol)
        s_ref[:, h] = s * jnp.exp(glast) + _dot(k_dec, v_new, _BTN)
        z = a_ref[:, :, 3 * W_A + lo:3 * W_A + lo + HEAD_DIM]
        o_ref[:, :, lo:lo + HEAD_DIM] = _rms(o, ng_ref[...]) * (z * _sigmoid(z))


def _gdn(a_cols, conv_buf, s0, conv_w, gate_par, norm_g, c, t_valid):
    b, t, _ = a_cols.shape
    g_b = min(GROUP_B, b)
    kern = functools.partial(_gdn_kernel, c=c, t_valid=t_valid)
    state = pl.BlockSpec((g_b, H_A, HEAD_DIM, HEAD_DIM), lambda bi, ci: (bi, 0, 0, 0))
    return pl.pallas_call(
        kern,
        grid=(b // g_b, t // c),
        in_specs=[pl.BlockSpec((g_b, c, A_PAD), lambda bi, ci: (bi, ci, 0)),
                  pl.BlockSpec((g_b, GDN_CONV - 1, 3 * W_A), lambda bi, ci: (bi, 0, 0)),
                  state, _const_spec(conv_w.shape), _const_spec(gate_par.shape), _const_spec(norm_g.shape)],
        out_specs=[pl.BlockSpec((g_b, c, W_A), lambda bi, ci: (bi, ci, 0)), state],
        out_shape=[jax.ShapeDtypeStruct((b, t, W_A), F32),
                   jax.ShapeDtypeStruct((b, H_A, HEAD_DIM, HEAD_DIM), F32)],
        scratch_shapes=[pltpu.VMEM((g_b, c + SUBLANES, 3 * W_A), F32)],
        compiler_params=_params("arbitrary", "arbitrary"),
        name="gdn_mixer",
    )(a_cols, conv_buf, s0, conv_w, gate_par, norm_g)


def _rwkv_kernel(c_ref, prev_ref, s0_ref, mu_ref, vec_ref, w2_ref, a2_ref, g2_ref, o_ref, s_ref, xp_ref,
                 *, c, t_valid):
    ci = pl.program_id(1)
    g_b = c_ref.shape[0]

    @pl.when(ci == 0)
    def _():
        s_ref[...] = s0_ref[...]
        xp_ref[:, SUBLANES - 1:SUBLANES, :] = prev_ref[...]

    x = c_ref[...]
    xp_ref[:, SUBLANES:SUBLANES + c, :] = x
    shifted = xp_ref[:, SUBLANES - 1:SUBLANES - 1 + c, :]
    xp_ref[:, SUBLANES - 1:SUBLANES, :] = xp_ref[:, c + SUBLANES - 1:c + SUBLANES, :]
    xs = (x + (shifted - x) * mu_ref[...]).reshape(g_b * c, C_COLS)

    w0, a0, k_k, k_a, r_k, lnx_w, lnx_b = (vec_ref[i:i + 1, :] for i in range(7))
    r = xs[:, 0:W_C]
    k = xs[:, W_C:2 * W_C]
    v = xs[:, 2 * W_C:3 * W_C]
    lora = xs[:, 3 * W_C:3 * W_C + DECAY_LORA + AAA_LORA]
    gd = xs[:, 3 * W_C + DECAY_LORA + AAA_LORA:C_COLS]
    lane = lax.broadcasted_iota(jnp.int32, (1, DECAY_LORA + AAA_LORA), 1)
    lora = jnp.where(lane < DECAY_LORA, jnp.tanh(lora), lora)
    w = -_softplus(-(w0 + _dot(lora, w2_ref[...], _NN))) - 0.5
    a = _sigmoid(a0 + _dot(lora, a2_ref[...], _NN))
    gate = _dot(_sigmoid(gd), g2_ref[...], _NN)
    kk = k * k_k
    k = k * (1.0 + (a - 1.0) * k_a)

    tpos = ci * c + lax.broadcasted_iota(jnp.int32, (1, c, 1), 1)
    valid = tpos < t_valid
    shape3 = (g_b, c, W_C)
    lw = jnp.where(valid, (-jnp.exp(w)).reshape(shape3), 0.0)
    cl = _chunk_cumsum(lw, c)
    p_in = jnp.exp(cl)
    p_inv = jnp.exp(-cl)
    p_ex = jnp.exp(cl - lw)
    r3, k3, v3, kk3, a3 = (z.reshape(shape3) for z in (r, k, v, kk, a))
    k3 = jnp.where(valid, k3, 0.0)
    kk3 = jnp.where(valid, kk3, 0.0)

    row = lax.broadcasted_iota(jnp.int32, (2 * c, 2 * c), 0)
    col = lax.broadcasted_iota(jnp.int32, (2 * c, 2 * c), 1)
    colc = jnp.where(col >= c, col - c, col)
    mask = colc < jnp.where(row < c, row, row - c + 1)
    second = lax.broadcasted_iota(jnp.int32, (1, 1, 2 * c), 2) >= c

    outs = []
    for h in range(H_C):
        lo = h * HEAD_DIM
        sl = slice(lo, lo + HEAD_DIM)
        rh, kh, vh, ah = r3[:, :, sl], k3[:, :, sl], v3[:, :, sl], a3[:, :, sl]
        kkh = kk3[:, :, sl]
        kkh = kkh * lax.rsqrt(jnp.sum(kkh * kkh, axis=-1, keepdims=True) + 1e-12)
        at = -kkh * p_ex[:, :, sl]
        bt = kkh * ah * p_inv[:, :, sl]
        kt = kh * p_inv[:, :, sl]
        rt = rh * p_in[:, :, sl]
        lhs = jnp.concatenate([at, rt], axis=1)
        rhs = jnp.concatenate([bt, kt], axis=1)
        gm = jnp.where(mask, _dot(lhs, rhs, _BNT), 0.0)
        top = gm[:, 0:c, :]
        bot = gm[:, c:2 * c, :]
        l_ab = top[:, :, 0:c]
        vv = jnp.concatenate([vh, vh], axis=1)
        s = s_ref[:, h]
        rhs_u = _dot(at, s, _BNT) + _dot(jnp.where(second, top, 0.0), vv, _BNN)
        u = _neumann_solve(l_ab, rhs_u, c)
        uv = jnp.concatenate([u, vh], axis=1)
        o = _dot(rt, s, _BNT) + _dot(bot, uv, _BNN)
        s_ref[:, h] = (s + _dot(uv, rhs, _BTN)) * p_in[:, c - 1:c, sl]
        mean = jnp.mean(o, axis=-1, keepdims=True)
        var = jnp.mean(jnp.square(o - mean), axis=-1, keepdims=True)
        o = (o - mean) * lax.rsqrt(var + RWKV_GN_EPS) * lnx_w[:, sl] + lnx_b[:, sl]
        bonus = jnp.sum(rh * k3[:, :, sl] * r_k[:, sl], axis=-1, keepdims=True) * vh
        outs.append(o + bonus)
    o_all = jnp.concatenate(outs, axis=-1)
    o_ref[...] = o_all * gate.reshape(shape3)


def _rwkv(c_cols, prev, s0, mu, vecs, w2p, a2p, g2, c, t_valid):
    b, t, _ = c_cols.shape
    g_b = min(GROUP_B, b)
    kern = functools.partial(_rwkv_kernel, c=c, t_valid=t_valid)
    state = pl.BlockSpec((g_b, H_C, HEAD_DIM, HEAD_DIM), lambda bi, ci: (bi, 0, 0, 0))
    return pl.pallas_call(
        kern,
        grid=(b // g_b, t // c),
        in_specs=[pl.BlockSpec((g_b, c, C_COLS), lambda bi, ci: (bi, ci, 0)),
                  pl.BlockSpec((g_b, 1, C_COLS), lambda bi, ci: (bi, 0, 0)),
                  state, _const_spec(mu.shape), _const_spec(vecs.shape), _const_spec(w2p.shape),
                  _const_spec(a2p.shape), _const_spec(g2.shape)],
        out_specs=[pl.BlockSpec((g_b, c, W_C), lambda bi, ci: (bi, ci, 0)), state],
        out_shape=[jax.ShapeDtypeStruct((b, t, W_C), F32),
                   jax.ShapeDtypeStruct((b, H_C, HEAD_DIM, HEAD_DIM), F32)],
        scratch_shapes=[pltpu.VMEM((g_b, c + SUBLANES, C_COLS), F32)],
        compiler_params=_params("arbitrary", "arbitrary"),
        name="rwkv_mixer",
    )(c_cols, prev, s0, mu, vecs, w2p, a2p, g2)


def _diff_lambda(lp_ref, lam_init):
    l1 = jnp.exp(jnp.sum(lp_ref[0:1, :] * lp_ref[1:2, :], axis=-1, keepdims=True))
    l2 = jnp.exp(jnp.sum(lp_ref[2:3, :] * lp_ref[3:4, :], axis=-1, keepdims=True))
    return l1 - l2 + lam_init


def _dattn_kernel(qi_ref, ki_ref, lp_ref, qt_ref, k_ref, vt_ref, sub_ref, o_ref, qm_ref, m_ref, l_ref, acc_ref,
                  *, lam_init):
    step = pl.program_id(1)
    qi = qi_ref[step]
    ki = ki_ref[step]
    tq = qt_ref.shape[2]
    tk = k_ref.shape[1]

    @pl.when(ki == 0)
    def _():
        m_ref[...] = jnp.full(m_ref.shape, -jnp.inf, F32)
        l_ref[...] = jnp.zeros(l_ref.shape, F32)
        acc_ref[...] = jnp.zeros(acc_ref.shape, F32)
        fmap = lax.broadcasted_iota(jnp.int32, (W_B, 1), 0) // DB
        qt = qt_ref[0]
        for hm in range(2 * H_B):
            qm_ref[hm] = jnp.where(fmap == hm, qt, jnp.zeros_like(qt))

    def tile(diagonal):
        for hm in range(2 * H_B):
            h = hm // 2
            for s in range(tq // ATT_STRIP):
                q0 = s * ATT_STRIP
                cs = slice(q0, q0 + ATT_STRIP)
                m = m_ref[hm, :, cs]
                l = l_ref[hm, :, cs]
                acc = acc_ref[hm, :, cs]
                for kb in range(tk // ATT_KEYS):
                    k0 = kb * ATT_KEYS
                    if diagonal and k0 > q0 + ATT_STRIP - 1:
                        continue
                    st = jnp.dot(k_ref[0, k0:k0 + ATT_KEYS, :], qm_ref[hm, :, cs], preferred_element_type=F32)
                    if diagonal and k0 + ATT_KEYS - 1 > q0:
                        kpos = k0 + lax.broadcasted_iota(jnp.int32, (ATT_KEYS, ATT_STRIP), 0)
                        qpos = q0 + lax.broadcasted_iota(jnp.int32, (ATT_KEYS, ATT_STRIP), 1)
                        st = jnp.where(kpos <= qpos, st, -jnp.inf)
                    m_new = jnp.maximum(m, jnp.max(st, axis=0, keepdims=True))
                    alpha = jnp.exp(m - m_new)
                    p = jnp.exp(st - m_new)
                    l = alpha * l + jnp.sum(p, axis=0, keepdims=True)
                    vth = vt_ref[0, h * HEAD_DIM:(h + 1) * HEAD_DIM, k0:k0 + ATT_KEYS]
                    acc = alpha * acc + jnp.dot(vth, p.astype(BF16), preferred_element_type=F32)
                    m = m_new
                m_ref[hm, :, cs] = m
                l_ref[hm, :, cs] = l
                acc_ref[hm, :, cs] = acc

    pl.when(ki < qi)(lambda: tile(False))

    @pl.when(ki == qi)
    def _():
        tile(True)
        lam = _diff_lambda(lp_ref, lam_init)
        heads = []
        for h in range(H_B):
            oh = acc_ref[2 * h] / l_ref[2 * h] - lam * (acc_ref[2 * h + 1] / l_ref[2 * h + 1])
            heads.append(oh * lax.rsqrt(jnp.mean(oh * oh, axis=0, keepdims=True) + NORM_EPS))
        o = jnp.concatenate(heads, axis=0).T
        o_ref[0] = o * sub_ref[...] * (1.0 - lam_init)


def _dattn_prompt(qt, k, vt, lam_par, subln, lam_init):
    b, _, t = qt.shape
    tile = _pick_tile(t, ATT_TILE)
    n = t // tile
    pairs = [(qi, ki) for qi in range(n) for ki in range(qi + 1)]
    qi_tab = jnp.asarray(np.array([p[0] for p in pairs], np.int32))
    ki_tab = jnp.asarray(np.array([p[1] for p in pairs], np.int32))
    const = lambda shape: pl.BlockSpec(shape, lambda bi, s, qi, ki: (0,) * len(shape))
    grid_spec = pltpu.PrefetchScalarGridSpec(
        num_scalar_prefetch=2,
        grid=(b, len(pairs)),
        in_specs=[const(lam_par.shape),
                  pl.BlockSpec((1, W_B, tile), lambda bi, s, qi, ki: (bi, 0, qi[s])),
                  pl.BlockSpec((1, tile, W_B), lambda bi, s, qi, ki: (bi, ki[s], 0)),
                  pl.BlockSpec((1, W_B, tile), lambda bi, s, qi, ki: (bi, 0, ki[s])),
                  const(subln.shape)],
        out_specs=pl.BlockSpec((1, tile, W_B), lambda bi, s, qi, ki: (bi, qi[s], 0)),
        scratch_shapes=[pltpu.VMEM((2 * H_B, W_B, tile), BF16), pltpu.VMEM((2 * H_B, 1, tile), F32),
                        pltpu.VMEM((2 * H_B, 1, tile), F32), pltpu.VMEM((2 * H_B, HEAD_DIM, tile), F32)],
    )
    return pl.pallas_call(
        functools.partial(_dattn_kernel, lam_init=lam_init),
        grid_spec=grid_spec,
        out_shape=jax.ShapeDtypeStruct((b, t, W_B), F32),
        compiler_params=_params("arbitrary", "arbitrary"),
        name="diff_attn_prompt",
    )(qi_tab, ki_tab, lam_par, qt, k, vt, subln)


def _softmax_rows(s, m_ref, l_ref):
    m_old = m_ref[...]
    m_new = jnp.maximum(m_old, jnp.max(s, axis=-1, keepdims=True))
    alpha = jnp.exp(m_old - m_new)
    p = jnp.exp(s - m_new)
    l_ref[...] = alpha * l_ref[...] + jnp.sum(p, axis=-1, keepdims=True)
    m_ref[...] = m_new
    return p, alpha


def _paged_kernel(pt_ref, lp_ref, q_ref, kn_ref, vn_ref, sub_ref, *refs, pages, lam_init, t_valid):
    k_pages = refs[0:pages]
    v_pages = refs[pages:2 * pages]
    o_ref, m_ref, l_ref, acc_ref = refs[2 * pages:]
    j = pl.program_id(1)
    tp = q_ref.shape[1]
    maps = lax.broadcasted_iota(jnp.int32, (1, W_B), 1) // DB

    @pl.when(j == 0)
    def _():
        m_ref[...] = jnp.full(m_ref.shape, -jnp.inf, F32)
        l_ref[...] = jnp.zeros(l_ref.shape, F32)
        acc_ref[...] = jnp.zeros(acc_ref.shape, F32)

    q = q_ref[0]
    qm = jnp.concatenate([jnp.where(maps == hm, q, 0.0) for hm in range(2 * H_B)], axis=0).astype(BF16)
    scale = DB ** -0.5

    kt = jnp.concatenate([kp[0, 0].astype(BF16) for kp in k_pages], axis=1)
    vt = jnp.concatenate([vp[0, 0].astype(BF16) for vp in v_pages], axis=1)
    s = jnp.dot(qm, kt, preferred_element_type=F32) * scale
    p, alpha = _softmax_rows(s, m_ref, l_ref)
    acc_ref[...] = alpha * acc_ref[...] + _dot(p, vt, _NT)

    @pl.when(j == pl.num_programs(1) - 1)
    def _():
        rows = 2 * H_B * tp
        tq_pos = lax.broadcasted_iota(jnp.int32, (rows, tp), 0) % tp
        tk_pos = lax.broadcasted_iota(jnp.int32, (rows, tp), 1)
        s_new = _dot(qm, kn_ref[0], _NT) * scale
        s_new = jnp.where((tk_pos <= tq_pos) & (tk_pos < t_valid), s_new, -jnp.inf)
        p_new, alpha_new = _softmax_rows(s_new, m_ref, l_ref)
        res = (alpha_new * acc_ref[...] + _dot(p_new, vn_ref[0], _NN)) / l_ref[...]
        lam = _diff_lambda(lp_ref, lam_init)
        heads = lax.broadcasted_iota(jnp.int32, (1, W_B), 1) // HEAD_DIM
        o = jnp.zeros((tp, W_B), F32)
        for h in range(H_B):
            oh = res[2 * h * tp:(2 * h + 1) * tp] - lam * res[(2 * h + 1) * tp:(2 * h + 2) * tp]
            o = jnp.where(heads == h, oh, o)
        r = lax.broadcasted_iota(jnp.int32, (W_B, W_B), 0) // HEAD_DIM
        c = lax.broadcasted_iota(jnp.int32, (W_B, W_B), 1) // HEAD_DIM
        ms = _dot_f32(o * o, jnp.where(r == c, 1.0 / HEAD_DIM, 0.0).astype(F32), _NN)
        o_ref[0] = o * lax.rsqrt(ms + NORM_EPS) * sub_ref[...] * (1.0 - lam_init)


def _dattn_paged(q, k, v, kt_pool, vt_pool, layer, page_table, lam_par, subln, lam_init, t_valid):
    b, tp, _ = q.shape
    n_pages = page_table.shape[1]
    page = kt_pool.shape[3]
    pages = math.gcd(n_pages, PAGES_PER_STEP)
    rows = 2 * H_B * tp
    kern = functools.partial(_paged_kernel, pages=pages, lam_init=lam_init, t_valid=t_valid)

    def page_spec(i):
        return pl.BlockSpec((1, 1, W_B, page), lambda bi, j, pt: (layer, pt[bi, j * pages + i], 0, 0))

    tok = pl.BlockSpec((1, tp, W_B), lambda bi, j, pt: (bi, 0, 0))
    const = lambda shape: pl.BlockSpec(shape, lambda bi, j, pt: (0,) * len(shape))
    grid_spec = pltpu.PrefetchScalarGridSpec(
        num_scalar_prefetch=1,
        grid=(b, n_pages // pages),
        in_specs=[const(lam_par.shape), tok, tok, tok, const(subln.shape)]
        + [page_spec(i) for i in range(pages)] * 2,
        out_specs=tok,
        scratch_shapes=[pltpu.VMEM((rows, 1), F32), pltpu.VMEM((rows, 1), F32), pltpu.VMEM((rows, W_B), F32)],
    )
    return pl.pallas_call(
        kern,
        grid_spec=grid_spec,
        out_shape=jax.ShapeDtypeStruct((b, tp, W_B), F32),
        compiler_params=_params("arbitrary", "arbitrary"),
        name="diff_attn_paged",
    )(page_table, lam_par, q, k, v, subln, *([kt_pool] * pages), *([vt_pool] * pages))


def _mem_project_kernel(m_ref, g_ref, wk_ref, wv_ref, k_ref, v_ref):
    h = _rms(m_ref[...], g_ref[...]).astype(BF16)
    k_ref[...] = jnp.dot(h, wk_ref[...], preferred_element_type=F32)
    v_ref[...] = jnp.dot(h, wv_ref[...], preferred_element_type=F32)


def _mem_project(mem2d, g, wk, wv):
    n = mem2d.shape[0]
    tm = _pick_tile(n, ROW_TILE)
    return pl.pallas_call(
        _mem_project_kernel,
        grid=(n // tm,),
        in_specs=[pl.BlockSpec((tm, D_MODEL), lambda i: (i, 0)), _const_spec(g.shape), _const_spec(wk.shape),
                  _const_spec(wv.shape)],
        out_specs=[pl.BlockSpec((tm, MEM_INNER), lambda i: (i, 0))] * 2,
        out_shape=[jax.ShapeDtypeStruct((n, MEM_INNER), F32)] * 2,
        compiler_params=_params("arbitrary"),
        name="mem_project",
    )(mem2d, g, wk, wv)


def _mix_mem_kernel(x_ref, oa_ref, ob_ref, oc_ref, wout_ref, g_ref, wq_ref, mk_ref, mv_ref, wo_ref, out_ref):
    bb, tt, _ = x_ref.shape
    m = bb * tt
    x = x_ref[...].reshape(m, D_MODEL)
    mix = (_dot(oa_ref[...].reshape(m, W_A), wout_ref[0:W_A, :], _NN)
           + _dot(ob_ref[...].reshape(m, W_B), wout_ref[W_A:W_A + W_B, :], _NN)
           + _dot(oc_ref[...].reshape(m, W_C), wout_ref[W_A + W_B:D_MODEL, :], _NN))
    x = x + mix
    q = _dot(_rms(x, g_ref[...]), wq_ref[...], _NN)
    heads = []
    for h in range(MEM_HEADS):
        sl = slice(h * MEM_HD, (h + 1) * MEM_HD)
        qh = q[:, sl].reshape(bb, tt, MEM_HD)
        s = _dot(qh, mk_ref[:, :, sl], _BNT) * (MEM_HD ** -0.5)
        p = jnp.exp(s - jnp.max(s, axis=-1, keepdims=True))
        p = p / jnp.sum(p, axis=-1, keepdims=True)
        heads.append(_dot(p, mv_ref[:, :, sl], _BNN).reshape(m, MEM_HD))
    o = jnp.concatenate(heads, axis=-1)
    out_ref[...] = (x + _dot(o, wo_ref[...], _NN)).reshape(bb, tt, D_MODEL)


def _mix_mem(x, o_a, o_b, o_c, w_out, g, wq, mem_k, mem_v, wo, bb, tt):
    b, t, _ = x.shape
    mt = mem_k.shape[1]
    blk = lambda w: pl.BlockSpec((bb, tt, w), lambda bi, ti: (bi, ti, 0))
    mem = pl.BlockSpec((bb, mt, MEM_INNER), lambda bi, ti: (bi, 0, 0))
    return pl.pallas_call(
        _mix_mem_kernel,
        grid=(b // bb, t // tt),
        in_specs=[blk(D_MODEL), blk(W_A), blk(W_B), blk(W_C), _const_spec(w_out.shape), _const_spec(g.shape),
                  _const_spec(wq.shape), mem, mem, _const_spec(wo.shape)],
        out_specs=blk(D_MODEL),
        out_shape=jax.ShapeDtypeStruct((b, t, D_MODEL), F32),
        compiler_params=_params("arbitrary", "arbitrary"),
        name="mix_out_mem_attn",
    )(x, o_a, o_b, o_c, w_out, g, wq, mem_k, mem_v, wo)


FF_CHUNK = 256


def _ffn_kernel(x_ref, buf_ref, g_ref, wg_ref, wv_ref, cw_ref, cb_ref, wo_ref, gf_ref, out_ref, tail_ref,
                carry_ref, gp_ref, acc_ref, *, final_norm):
    ti = pl.program_id(1)
    bb, tt, _ = x_ref.shape
    m = bb * tt
    hist = SUBLANES - (FFN_CONV - 1)

    @pl.when(ti == 0)
    def _():
        carry_ref[:, hist:SUBLANES, :] = buf_ref[...]

    x = x_ref[...].reshape(m, D_MODEL)
    h = _rms(x, g_ref[...]).astype(BF16)
    for ck in range(D_FF // FF_CHUNK):
        sl = slice(ck * FF_CHUNK, (ck + 1) * FF_CHUNK)
        gate = jnp.dot(h, wg_ref[:, sl], preferred_element_type=F32).reshape(bb, tt, FF_CHUNK)
        val = jnp.dot(h, wv_ref[:, sl], preferred_element_type=F32)
        gp_ref[:, hist:SUBLANES, :] = carry_ref[:, hist:SUBLANES, sl]
        gp_ref[:, SUBLANES:SUBLANES + tt, :] = gate
        y = cb_ref[:, sl] + gate * cw_ref[FFN_CONV - 1:FFN_CONV, sl]
        for j in range(FFN_CONV - 1):
            y = y + gp_ref[:, hist + j:hist + j + tt, :] * cw_ref[j:j + 1, sl]
        carry_ref[:, :, sl] = gate[:, tt - SUBLANES:tt, :]
        y = y.reshape(m, FF_CHUNK)
        act = (y * _sigmoid(y) * val).astype(BF16)
        part = jnp.dot(act, wo_ref[sl, :], preferred_element_type=F32)
        if ck == 0:
            acc_ref[...] = x + part
        else:
            acc_ref[...] += part
    tail_ref[...] = carry_ref[...]
    y = acc_ref[...]
    if final_norm:
        y = _rms(y, gf_ref[...])
    out_ref[...] = y.reshape(bb, tt, D_MODEL)


def _ffn(x, buf, g, wg, wv, conv_w, conv_b, wo, g_final, bb, tt, final_norm):
    b, t, _ = x.shape
    kern = functools.partial(_ffn_kernel, final_norm=final_norm)
    blk = pl.BlockSpec((bb, tt, D_MODEL), lambda bi, ti: (bi, ti, 0))
    return pl.pallas_call(
        kern,
        grid=(b // bb, t // tt),
        in_specs=[blk, pl.BlockSpec((bb, FFN_CONV - 1, D_FF), lambda bi, ti: (bi, 0, 0)), _const_spec(g.shape),
                  _const_spec(wg.shape), _const_spec(wv.shape), _const_spec(conv_w.shape),
                  _const_spec(conv_b.shape), _const_spec(wo.shape), _const_spec(g_final.shape)],
        out_specs=[blk, pl.BlockSpec((bb, SUBLANES, D_FF), lambda bi, ti: (bi, 0, 0))],
        out_shape=[jax.ShapeDtypeStruct((b, t, D_MODEL), F32), jax.ShapeDtypeStruct((b, SUBLANES, D_FF), F32)],
        scratch_shapes=[pltpu.VMEM((bb, SUBLANES, D_FF), F32), pltpu.VMEM((bb, tt + SUBLANES, FF_CHUNK), F32),
                        pltpu.VMEM((bb * tt, D_MODEL), F32)],
        compiler_params=_params("arbitrary", "arbitrary"),
        name="conv_glu_ffn",
    )(x, buf, g, wg, wv, conv_w, conv_b, wo, g_final)


def _layer_weights(l, w):
    bf = lambda z: z.astype(BF16)
    row = lambda z: z.reshape(1, -1)
    w_in = w['w_in'][l]
    gates = jnp.pad(w_in[:, 4 * W_A:A_COLS], ((0, 0), (0, GATE_PAD - 2 * H_A)))
    pad_gate = lambda z: jnp.pad(z, (H_A, GATE_PAD - 2 * H_A)).reshape(1, GATE_PAD)
    zeros_lora = jnp.zeros((DECAY_LORA, W_C), F32)
    wb = w_in[:, A_COLS:A_COLS + B_COLS]
    return dict(
        norm_mix=row(w['norm_mix'][l]),
        wa=jnp.concatenate([w_in[:, 0:4 * W_A], gates], axis=1),
        wb=wb, wbt=wb.T,
        wc=w_in[:, A_COLS + B_COLS:],
        gdn_conv_w=w['gdn_conv_w'][l],
        gdn_gate=jnp.concatenate([pad_gate(w['gdn_a_log'][l]), pad_gate(w['gdn_dt_bias'][l])], axis=0),
        gdn_norm=row(w['gdn_norm'][l]),
        lam_par=jnp.stack([w['diff_lq1'][l], w['diff_lk1'][l], w['diff_lq2'][l], w['diff_lk2'][l]]),
        subln=row(jnp.tile(w['diff_subln'][l], H_B)),
        rw_mu=row(w['rw_mu'][l]),
        rw_vecs=jnp.stack([w['rw_w0'][l], w['rw_a0'][l], w['rw_k_k'][l], w['rw_k_a'][l],
                           w['rw_r_k'][l].reshape(-1), w['rw_lnx_w'][l], w['rw_lnx_b'][l],
                           jnp.zeros((W_C,), F32)]),
        rw_w2=bf(jnp.concatenate([w['rw_w2'][l], zeros_lora], axis=0)),
        rw_a2=bf(jnp.concatenate([zeros_lora, w['rw_a2'][l]], axis=0)),
        rw_g2=bf(w['rw_g2'][l]),
        w_out=bf(w['w_out'][l]),
        norm_mem=row(w['norm_mem'][l]),
        norm_mem_kv=row(w['norm_mem_kv'][l]),
        mem_wq=bf(w['mem_wq'][l]), mem_wk=bf(w['mem_wk'][l]), mem_wv=bf(w['mem_wv'][l]),
        mem_wo=bf(w['mem_wo'][l]),
        norm_ffn=row(w['norm_ffn'][l]),
        ffn_wg=bf(w['ffn_w_in'][l][:, 0:D_FF]), ffn_wv=bf(w['ffn_w_in'][l][:, D_FF:]),
        ffn_conv_w=w['ffn_conv_w'][l], ffn_conv_b=row(w['ffn_conv_b'][l]),
        ffn_wo=bf(w['ffn_w_out'][l]),
        norm_final=row(w['norm_final']),
    )


def _trunk_layer(x, t_valid, chunk, mem_k, mem_v, gdn_buf, gdn_s, rw_prev, rw_s, ffn_buf, attn_fn, p,
                 proj_bb, mem_bb, ffn_bb, tile_t, final_norm, feature_major):
    b, t, _ = x.shape
    proj = _proj_in(x, p['norm_mix'], p['wa'], p['wb'], p['wbt'], p['wc'], proj_bb, tile_t, feature_major)
    if feature_major:
        a_cols, qt, k_tok, k_new, v_new, vt, c_cols = proj
        o_b = attn_fn(qt, k_tok, vt)
    else:
        a_cols, q_b, k_b, v_b, c_cols = proj
        o_b = attn_fn(q_b, k_b, v_b)
        k_new, v_new = k_b[:, 0:t_valid], v_b[:, 0:t_valid]
    o_a, gdn_s_new = _gdn(a_cols, gdn_buf, gdn_s, p['gdn_conv_w'], p['gdn_gate'], p['gdn_norm'], chunk, t_valid)
    o_c, rw_s_new = _rwkv(c_cols, rw_prev.reshape(b, 1, C_COLS), rw_s, p['rw_mu'], p['rw_vecs'], p['rw_w2'],
                          p['rw_a2'], p['rw_g2'], chunk, t_valid)
    x = _mix_mem(x, o_a, o_b, o_c, p['w_out'], p['norm_mem'], p['mem_wq'], mem_k, mem_v, p['mem_wo'],
                 mem_bb, tile_t)
    x, tail = _ffn(x, ffn_buf, p['norm_ffn'], p['ffn_wg'], p['ffn_wv'], p['ffn_conv_w'], p['ffn_conv_b'],
                   p['ffn_wo'], p['norm_final'], ffn_bb, tile_t, final_norm)
    keep = min(t_valid, GDN_CONV - 1)
    gdn_buf_new = jnp.concatenate([gdn_buf[:, keep:], a_cols[:, t_valid - keep:t_valid, 0:3 * W_A]], axis=1)
    last = t_valid - (t - SUBLANES)
    ffn_buf_new = tail[:, last - (FFN_CONV - 1):last]
    return (x, k_new, v_new, gdn_buf_new, gdn_s_new, c_cols[:, t_valid - 1], rw_s_new, ffn_buf_new)


def kernel(x_prompt, x_sample, cache_diff_k, cache_diff_v, state_gdn_conv, state_gdn, state_rwkv_shift, state_rwkv, state_ffn_conv, cache_mem_k, cache_mem_v, page_table, mem_prompt, norm_mix, w_in, gdn_conv_w, gdn_a_log, gdn_dt_bias, gdn_norm, diff_lq1, diff_lk1, diff_lq2, diff_lk2, diff_subln, rw_mu, rw_w0, rw_w2, rw_a0, rw_a2, rw_g2, rw_k_k, rw_k_a, rw_r_k, rw_lnx_w, rw_lnx_b, w_out, norm_mem, norm_mem_kv, mem_wq, mem_wk, mem_wv, mem_wo, norm_ffn, ffn_w_in, ffn_conv_w, ffn_conv_b, ffn_w_out, norm_final):
    w = dict(norm_mix=norm_mix, w_in=w_in, gdn_conv_w=gdn_conv_w, gdn_a_log=gdn_a_log, gdn_dt_bias=gdn_dt_bias,
             gdn_norm=gdn_norm, diff_lq1=diff_lq1, diff_lk1=diff_lk1, diff_lq2=diff_lq2, diff_lk2=diff_lk2,
             diff_subln=diff_subln, rw_mu=rw_mu, rw_w0=rw_w0, rw_w2=rw_w2, rw_a0=rw_a0, rw_a2=rw_a2, rw_g2=rw_g2,
             rw_k_k=rw_k_k, rw_k_a=rw_k_a, rw_r_k=rw_r_k, rw_lnx_w=rw_lnx_w, rw_lnx_b=rw_lnx_b, w_out=w_out,
             norm_mem=norm_mem, norm_mem_kv=norm_mem_kv, mem_wq=mem_wq, mem_wk=mem_wk, mem_wv=mem_wv,
             mem_wo=mem_wo, norm_ffn=norm_ffn, ffn_w_in=ffn_w_in, ffn_conv_w=ffn_conv_w, ffn_conv_b=ffn_conv_b,
             ffn_w_out=ffn_w_out, norm_final=norm_final)
    depth = w_in.shape[0]
    bp, tp, _ = x_prompt.shape
    bs, ts, _ = x_sample.shape
    mt = mem_prompt.shape[1]
    ts_pad = -(-ts // SUBLANES) * SUBLANES
    n_pool, page = cache_diff_k.shape[1], cache_diff_k.shape[2]
    pool_view = lambda z: jnp.transpose(z, (0, 1, 3, 4, 2)).reshape(depth, n_pool, W_B, page)
    kt_pool = pool_view(cache_diff_k)
    vt_pool = pool_view(cache_diff_v)

    xp = x_prompt
    xs = jnp.pad(x_sample, ((0, 0), (0, ts_pad - ts), (0, 0)))
    prompt_chunk = math.gcd(tp, PROMPT_CHUNK)
    prompt_tile = _pick_tile(tp, ROW_TILE)
    outs_p = [[] for _ in range(9)]
    outs_s = [[] for _ in range(7)]
    for l in range(depth):
        lam_init = 0.8 - 0.6 * math.exp(-0.3 * l)
        last = l == depth - 1
        p = _layer_weights(l, w)
        mk, mv = _mem_project(mem_prompt.reshape(bp * mt, D_MODEL), p['norm_mem_kv'], p['mem_wk'], p['mem_wv'])
        mk = mk.reshape(bp, mt, MEM_INNER)
        mv = mv.reshape(bp, mt, MEM_INNER)
        attn_p = functools.partial(_dattn_prompt, lam_par=p['lam_par'], subln=p['subln'], lam_init=lam_init)
        res = _trunk_layer(
            xp, tp, prompt_chunk, mk, mv,
            jnp.zeros((bp, GDN_CONV - 1, 3 * W_A), F32), jnp.zeros((bp, H_A, HEAD_DIM, HEAD_DIM), F32),
            jnp.zeros((bp, C_COLS), F32), jnp.zeros((bp, H_C, HEAD_DIM, HEAD_DIM), F32),
            jnp.zeros((bp, FFN_CONV - 1, D_FF), F32), attn_p, p, 1, 1, 1, prompt_tile, last, True)
        xp = res[0]
        for acc, val in zip(outs_p, res[1:] + (mk, mv)):
            acc.append(val)
        attn_s = functools.partial(_dattn_paged, kt_pool=kt_pool, vt_pool=vt_pool, layer=l, page_table=page_table,
                                   lam_par=p['lam_par'], subln=p['subln'], lam_init=lam_init, t_valid=ts)
        res = _trunk_layer(
            xs, ts, ts_pad, cache_mem_k[l].reshape(bs, mt, MEM_INNER), cache_mem_v[l].reshape(bs, mt, MEM_INNER),
            state_gdn_conv[l], state_gdn[l], state_rwkv_shift[l], state_rwkv[l], state_ffn_conv[l], attn_s, p,
            bs, min(GROUP_B, bs), bs, ts_pad, last, False)
        xs = res[0]
        for acc, val in zip(outs_s, res[1:]):
            acc.append(val)

    st = lambda vals, shape: jnp.stack(vals).reshape((depth,) + shape)
    p_k, p_v, p_gc, p_gs, p_rp, p_rs, p_fc, p_mk, p_mv = outs_p
    s_k, s_v, s_gc, s_gs, s_rp, s_rs, s_fc = outs_s
    kv_p = lambda vals: jnp.transpose(jnp.stack(vals).reshape(depth, bp, H_B, 2 * DB, tp), (0, 1, 4, 2, 3))
    kv_s = (bs, ts, H_B, 2 * DB)
    mem_shape = (bp, mt, MEM_HEADS, MEM_HD)
    return (xp, xs[:, 0:ts],
            kv_p(p_k), kv_p(p_v), jnp.stack(p_gc), jnp.stack(p_gs), jnp.stack(p_rp), jnp.stack(p_rs),
            jnp.stack(p_fc), st(p_mk, mem_shape), st(p_mv, mem_shape),
            st(s_k, kv_s), st(s_v, kv_s), jnp.stack(s_gc), jnp.stack(s_gs), jnp.stack(s_rp), jnp.stack(s_rs),
            jnp.stack(s_fc))
```

```python
import functools
import math

import numpy as np
import jax
import jax.numpy as jnp
from jax import lax
from jax.experimental import pallas as pl
from jax.experimental.pallas import tpu as pltpu

F32 = jnp.float32
BF16 = jnp.bfloat16

D_MODEL = 1024
HEAD_DIM = 64
W_A = 384
W_B = 256
W_C = 384
H_A = W_A // HEAD_DIM
H_B = W_B // HEAD_DIM
H_C = W_C // HEAD_DIM
DB = HEAD_DIM // 2
GDN_CONV = 4
DECAY_LORA = 64
AAA_LORA = 64
GATE_LORA = 128
RWKV_GN_EPS = 64e-5
MEM_HEADS = 4
MEM_HD = 128
MEM_INNER = MEM_HEADS * MEM_HD
D_FF = 2816
FFN_CONV = 3
NORM_EPS = 1e-6
A_COLS = 4 * W_A + 2 * H_A
B_COLS = 3 * W_B
C_COLS = 3 * W_C + DECAY_LORA + AAA_LORA + GATE_LORA
GATE_PAD = 128
A_PAD = 4 * W_A + GATE_PAD

SUBLANES = 8
LANES = 128
VMEM_LIMIT = 56 * 1024 * 1024
GROUP_B = 8
ROW_TILE = 512
ATT_TILE = 1024
PAGES_PER_STEP = 32

_NN = (((1,), (0,)), ((), ()))
_NT = (((1,), (1,)), ((), ()))
_BNN = (((2,), (1,)), ((0,), (0,)))
_BNT = (((2,), (2,)), ((0,), (0,)))
_BTN = (((1,), (1,)), ((0,), (0,)))


def _dot(a, b, dn):
    return lax.dot_general(a.astype(BF16), b.astype(BF16), dn, preferred_element_type=F32)


def _dot_f32(a, b, dn):
    return lax.dot_general(a, b, dn, precision=lax.Precision.HIGHEST, preferred_element_type=F32)


def _sigmoid(x):
    return 1.0 / (1.0 + jnp.exp(-x))


def _softplus(x):
    return jnp.maximum(x, 0.0) + jnp.log(1.0 + jnp.exp(-jnp.abs(x)))


def _rms(x, g, eps=NORM_EPS):
    return x * lax.rsqrt(jnp.mean(x * x, axis=-1, keepdims=True) + eps) * g


def _pick_tile(n, pref):
    if n <= pref:
        return n
    t = pref - pref % SUBLANES
    while n % t:
        t -= SUBLANES
    return t


def _params(*sem):
    return pltpu.CompilerParams(dimension_semantics=sem, vmem_limit_bytes=VMEM_LIMIT)


def _const_spec(shape):
    nd = len(shape)
    return pl.BlockSpec(shape, lambda *_: (0,) * nd, pipeline_mode=pl.Buffered(1))


def _proj_in_kernel(x_ref, g_ref, wa32_ref, wb32_ref, wbt32_ref, wc32_ref, *refs, feature_major):
    outs, (wa_ref, wb_ref, wbt_ref, wc_ref) = refs[:-4], refs[-4:]

    @pl.when((pl.program_id(0) == 0) & (pl.program_id(1) == 0))
    def _():
        wa_ref[...] = wa32_ref[...].astype(BF16)
        wb_ref[...] = wb32_ref[...].astype(BF16)
        wbt_ref[...] = wbt32_ref[...].astype(BF16)
        wc_ref[...] = wc32_ref[...].astype(BF16)

    bb, tt, _ = x_ref.shape
    m = bb * tt
    h = _rms(x_ref[...].reshape(m, D_MODEL), g_ref[...]).astype(BF16)
    cols = lambda w_ref, lo, n: jnp.dot(h, w_ref[:, lo:lo + n], preferred_element_type=F32)
    if feature_major:
        oa_ref, oqt_ref, ok_ref, okt_ref, ovt_ref, ovtb_ref, oc_ref = outs
        rows = lambda i: lax.dot_general(wbt_ref[i * W_B:(i + 1) * W_B, :], h, _NT, preferred_element_type=F32)
        oqt_ref[0] = (rows(0) * (DB ** -0.5)).astype(BF16)
        ok_ref[0] = cols(wb_ref, W_B, W_B).astype(BF16)
        okt_ref[0] = rows(1)
        vt = rows(2)
        ovt_ref[0] = vt
        ovtb_ref[0] = vt.astype(BF16)
    else:
        oa_ref, oq_ref, ok_ref, ov_ref, oc_ref = outs
        oq_ref[...] = cols(wb_ref, 0, W_B).reshape(bb, tt, W_B)
        ok_ref[...] = cols(wb_ref, W_B, W_B).reshape(bb, tt, W_B)
        ov_ref[...] = cols(wb_ref, 2 * W_B, W_B).reshape(bb, tt, W_B)
    oa_ref[...] = cols(wa_ref, 0, A_PAD).reshape(bb, tt, A_PAD)
    oc_ref[...] = cols(wc_ref, 0, C_COLS).reshape(bb, tt, C_COLS)


def _proj_in(x, g, wa, wb, wbt, wc, bb, tt, feature_major):
    b, t, _ = x.shape
    tok = lambda w: pl.BlockSpec((bb, tt, w), lambda bi, ti: (bi, ti, 0))
    feat = pl.BlockSpec((1, W_B, tt), lambda bi, ti: (bi, 0, ti))
    sds = jax.ShapeDtypeStruct
    if feature_major:
        assert bb == 1
        out_specs = [tok(A_PAD), feat, tok(W_B), feat, feat, feat, tok(C_COLS)]
        out_shape = [sds((b, t, A_PAD), F32), sds((b, W_B, t), BF16), sds((b, t, W_B), BF16),
                     sds((b, W_B, t), F32), sds((b, W_B, t), F32), sds((b, W_B, t), BF16), sds((b, t, C_COLS), F32)]
    else:
        out_specs = [tok(A_PAD), tok(W_B), tok(W_B), tok(W_B), tok(C_COLS)]
        out_shape = [sds((b, t, w), F32) for w in (A_PAD, W_B, W_B, W_B, C_COLS)]
    return pl.pallas_call(
        functools.partial(_proj_in_kernel, feature_major=feature_major),
        grid=(b // bb, t // tt),
        in_specs=[tok(D_MODEL), _const_spec(g.shape), _const_spec(wa.shape), _const_spec(wb.shape),
                  _const_spec(wbt.shape), _const_spec(wc.shape)],
        out_specs=out_specs,
        out_shape=out_shape,
        scratch_shapes=[pltpu.VMEM(z.shape, BF16) for z in (wa, wb, wbt, wc)],
        compiler_params=_params("arbitrary", "arbitrary"),
        name="proj_in",
    )(x, g, wa, wb, wbt, wc)


CHUNK = 64
PAIR = 2 * HEAD_DIM


def _tri(c, inclusive):
    r = lax.broadcasted_iota(jnp.int32, (c, c), 0)
    q = lax.broadcasted_iota(jnp.int32, (c, c), 1)
    return (q <= r) if inclusive else (q < r)


def _chunk_cumsum(x, c):
    tri = jnp.broadcast_to(_tri(c, True).astype(F32), (x.shape[0], c, c))
    return _dot_f32(tri, x, _BNN)


def _select_dot(x, sel, terms):
    acc = None
    rest = x
    for _ in range(terms):
        piece = rest.astype(BF16)
        part = jnp.dot(piece, sel, preferred_element_type=F32)
        acc = part if acc is None else acc + part
        rest = rest - piece.astype(F32)
    return acc


def _head_sum_matrix(width):
    r = lax.broadcasted_iota(jnp.int32, (width, width), 0) // HEAD_DIM
    q = lax.broadcasted_iota(jnp.int32, (width, width), 1) // HEAD_DIM
    return jnp.where(r == q, 1.0, 0.0).astype(BF16)


def _head_spread_matrix(first_lane, width):
    r = lax.broadcasted_iota(jnp.int32, (LANES, width), 0)
    q = lax.broadcasted_iota(jnp.int32, (LANES, width), 1) // HEAD_DIM
    return jnp.where(r == q + first_lane, 1.0, 0.0).astype(BF16)


def _pair_rows(z):
    first = (lax.broadcasted_iota(jnp.int32, (1, 1, z.shape[-1]), 2) % PAIR) < HEAD_DIM
    return jnp.concatenate([jnp.where(first, z, 0.0), jnp.where(first, 0.0, z)], axis=1)


def _pair_solve(l, x, c):
    x = x + _dot(l, _pair_rows(x), _BNN)
    p = l
    for _ in range(int(math.log2(c)) - 1):
        p = _dot(p, _pair_rows(p), _BNN)
        x = x + _dot(p, _pair_rows(x), _BNN)
    return x


def _load_pair_state(sp_ref, s0_ref, heads):
    sp_ref[...] = jnp.zeros(sp_ref.shape, F32)
    for p in range(heads // 2):
        sp_ref[:, p, 0:HEAD_DIM, 0:HEAD_DIM] = s0_ref[:, 2 * p]
        sp_ref[:, p, HEAD_DIM:PAIR, HEAD_DIM:PAIR] = s0_ref[:, 2 * p + 1]


def _store_pair_state(s_ref, sp_ref, heads):
    for p in range(heads // 2):
        s_ref[:, 2 * p] = sp_ref[:, p, 0:HEAD_DIM, 0:HEAD_DIM]
        s_ref[:, 2 * p + 1] = sp_ref[:, p, HEAD_DIM:PAIR, HEAD_DIM:PAIR]


def _pair_masks(c):
    i = lax.broadcasted_iota(jnp.int32, (c, 2 * c), 0)
    j = lax.broadcasted_iota(jnp.int32, (c, 2 * c), 1) % c
    r = lax.broadcasted_iota(jnp.int32, (PAIR, PAIR), 0) // HEAD_DIM
    q = lax.broadcasted_iota(jnp.int32, (PAIR, PAIR), 1) // HEAD_DIM
    return j < i, j <= i, r == q


def _gdn_kernel(a_ref, buf_ref, s0_ref, cw_ref, gp_ref, ng_ref, o_ref, s_ref, xp_ref, sp_ref, *, t_valid):
    ci = pl.program_id(1)
    g_b, c, _ = a_ref.shape
    m = g_b * c
    hist = SUBLANES - (GDN_CONV - 1)

    @pl.when(ci == 0)
    def _():
        xp_ref[:, hist:SUBLANES, :] = buf_ref[...]
        _load_pair_state(sp_ref, s0_ref, H_A)

    xp_ref[:, SUBLANES:SUBLANES + c, :] = a_ref[:, :, 0:3 * W_A]
    y = xp_ref[:, hist:hist + c, :] * cw_ref[0:1, :]
    for j in range(1, GDN_CONV):
        y = y + xp_ref[:, hist + j:hist + j + c, :] * cw_ref[j:j + 1, :]
    xp_ref[:, hist:SUBLANES, :] = xp_ref[:, c + hist:c + SUBLANES, :]
    qkv = (y * _sigmoid(y)).reshape(m, 3 * W_A)

    head_sum = _head_sum_matrix(W_A)
    q = qkv[:, 0:W_A]
    k = qkv[:, W_A:2 * W_A]
    v = qkv[:, 2 * W_A:3 * W_A]
    q = q * lax.rsqrt(_select_dot(q * q, head_sum, 2) + 1e-6) * (HEAD_DIM ** -0.5)
    k = k * lax.rsqrt(_select_dot(k * k, head_sum, 2) + 1e-6)

    gates = a_ref[:, :, 4 * W_A:A_PAD]
    tpos = ci * c + lax.broadcasted_iota(jnp.int32, (1, c, 1), 1)
    valid = tpos < t_valid
    beta_all = jnp.where(valid, _sigmoid(gates), 0.0)
    g_all = jnp.where(valid, -jnp.exp(gp_ref[0:1, :]) * _softplus(gates + gp_ref[1:2, :]), 0.0)
    gam_all = _chunk_cumsum(g_all, c)
    gam_t = jnp.swapaxes(gam_all, 1, 2)
    beta = _select_dot(beta_all.reshape(m, LANES), _head_spread_matrix(0, W_A), 3)
    gam = _select_dot(gam_all.reshape(m, LANES), _head_spread_matrix(H_A, W_A), 3)
    eg = jnp.exp(gam)
    kb = k * beta
    to3 = lambda z: z.reshape(g_b, c, W_A)
    gam3 = to3(gam)
    glast = gam3[:, c - 1:c, :]
    q3, k3, kb3, qe3, vb3, kbe3 = (to3(z) for z in (q, k, kb, q * eg, v * beta, kb * eg))
    kd3 = k3 * jnp.exp(glast - gam3)
    gl = jnp.exp(glast)
    strict, incl, diag_blocks = _pair_masks(c)

    n_pairs = H_A // 2
    stack = lambda z: jnp.concatenate([z[:, :, p * PAIR:(p + 1) * PAIR] for p in range(n_pairs)], axis=0)
    grow = jnp.concatenate(
        [jnp.concatenate([gam_t[:, H_A + 2 * p:H_A + 2 * p + 1, :], gam_t[:, H_A + 2 * p + 1:H_A + 2 * p + 2, :]],
                         axis=-1) for p in range(n_pairs)], axis=0)
    diff = stack(gam3) - grow
    k_rows = _pair_rows(stack(k3))
    a_mat = _dot(stack(kb3), k_rows, _BNT) * jnp.exp(jnp.where(strict, diff, -jnp.inf))
    qk = _dot(stack(q3), k_rows, _BNT) * jnp.exp(jnp.where(incl, diff, -jnp.inf))
    x = jnp.concatenate([stack(vb3), stack(kbe3)], axis=-1)
    x = _pair_solve(-a_mat, x, c)
    s = jnp.concatenate([sp_ref[:, p] for p in range(n_pairs)], axis=0)
    v_new = x[:, :, 0:PAIR] - _dot(x[:, :, PAIR:2 * PAIR], s, _BNN)
    o = _dot(stack(qe3), s, _BNN) + _dot(qk, _pair_rows(v_new), _BNN)
    s_new = jnp.where(diag_blocks, s * stack(gl) + _dot(stack(kd3), v_new, _BTN), 0.0)
    for p in range(n_pairs):
        sp_ref[:, p] = s_new[p * g_b:(p + 1) * g_b]
    outs = [o[p * g_b:(p + 1) * g_b] for p in range(n_pairs)]
    o = jnp.concatenate(outs, axis=-1).reshape(m, W_A)
    o = o * lax.rsqrt(_select_dot(o * o, head_sum, 2) * (1.0 / HEAD_DIM) + NORM_EPS) * ng_ref[...]
    z = a_ref[:, :, 3 * W_A:4 * W_A].reshape(m, W_A)
    o_ref[...] = (o * (z * _sigmoid(z))).reshape(g_b, c, W_A)

    @pl.when(ci == pl.num_programs(1) - 1)
    def _():
        _store_pair_state(s_ref, sp_ref, H_A)


def _gdn(a_cols, conv_buf, s0, conv_w, gate_par, norm_g, t_valid):
    b, t, _ = a_cols.shape
    g_b = min(GROUP_B, b)
    c = CHUNK
    kern = functools.partial(_gdn_kernel, t_valid=t_valid)
    state = pl.BlockSpec((g_b, H_A, HEAD_DIM, HEAD_DIM), lambda bi, ci: (bi, 0, 0, 0))
    return pl.pallas_call(
        kern,
        grid=(b // g_b, t // c),
        in_specs=[pl.BlockSpec((g_b, c, A_PAD), lambda bi, ci: (bi, ci, 0)),
                  pl.BlockSpec((g_b, GDN_CONV - 1, 3 * W_A), lambda bi, ci: (bi, 0, 0)),
                  state, _const_spec(conv_w.shape), _const_spec(gate_par.shape), _const_spec(norm_g.shape)],
        out_specs=[pl.BlockSpec((g_b, c, W_A), lambda bi, ci: (bi, ci, 0)), state],
        out_shape=[jax.ShapeDtypeStruct((b, t, W_A), F32),
                   jax.ShapeDtypeStruct((b, H_A, HEAD_DIM, HEAD_DIM), F32)],
        scratch_shapes=[pltpu.VMEM((g_b, c + SUBLANES, 3 * W_A), F32),
                        pltpu.VMEM((g_b, H_A // 2, PAIR, PAIR), F32)],
        compiler_params=_params("arbitrary", "arbitrary"),
        name="gdn_mixer",
    )(a_cols, conv_buf, s0, conv_w, gate_par, norm_g)


def _rwkv_kernel(c_ref, prev_ref, s0_ref, mu_ref, vec_ref, w2_ref, a2_ref, g2_ref, o_ref, s_ref, xp_ref, sp_ref,
                 *, t_valid):
    ci = pl.program_id(1)
    g_b, c, _ = c_ref.shape
    m = g_b * c

    @pl.when(ci == 0)
    def _():
        xp_ref[:, SUBLANES - 1:SUBLANES, :] = prev_ref[...]
        _load_pair_state(sp_ref, s0_ref, H_C)

    x = c_ref[...]
    xp_ref[:, SUBLANES:SUBLANES + c, :] = x
    shifted = xp_ref[:, SUBLANES - 1:SUBLANES - 1 + c, :]
    xp_ref[:, SUBLANES - 1:SUBLANES, :] = xp_ref[:, c + SUBLANES - 1:c + SUBLANES, :]
    xs = (x + (shifted - x) * mu_ref[...]).reshape(m, C_COLS)

    w0, a0, k_k, k_a, r_k, lnx_w, lnx_b = (vec_ref[i:i + 1, :] for i in range(7))
    r = xs[:, 0:W_C]
    k = xs[:, W_C:2 * W_C]
    v = xs[:, 2 * W_C:3 * W_C]
    lora = xs[:, 3 * W_C:3 * W_C + DECAY_LORA + AAA_LORA]
    gd = xs[:, 3 * W_C + DECAY_LORA + AAA_LORA:C_COLS]
    lane = lax.broadcasted_iota(jnp.int32, (1, DECAY_LORA + AAA_LORA), 1)
    lora = jnp.where(lane < DECAY_LORA, jnp.tanh(lora), lora)
    w = -_softplus(-(w0 + _dot(lora, w2_ref[...], _NN))) - 0.5
    a = _sigmoid(a0 + _dot(lora, a2_ref[...], _NN))
    gate = _dot(_sigmoid(gd), g2_ref[...], _NN)
    head_sum = _head_sum_matrix(W_C)
    kk = k * k_k
    kk = kk * lax.rsqrt(_select_dot(kk * kk, head_sum, 2) + 1e-12)
    k = k * (1.0 + (a - 1.0) * k_a)

    tpos = ci * c + lax.broadcasted_iota(jnp.int32, (1, c, 1), 1)
    valid = tpos < t_valid
    to3 = lambda z: z.reshape(g_b, c, W_C)
    lw = jnp.where(valid, to3(-jnp.exp(w)), 0.0)
    cl = _chunk_cumsum(lw, c)
    p_in = jnp.exp(cl)
    p_inv = jnp.exp(-cl)
    k3 = jnp.where(valid, to3(k), 0.0)
    kk3 = jnp.where(valid, to3(kk), 0.0)
    v3 = to3(v)
    at = -kk3 * jnp.exp(cl - lw)
    bt = kk3 * to3(a) * p_inv
    kt = k3 * p_inv
    rt = to3(r) * p_in

    strict, incl, diag_blocks = _pair_masks(c)
    tri4 = jnp.concatenate([jnp.concatenate([strict, strict], axis=1),
                            jnp.concatenate([incl, incl], axis=1)], axis=0)

    n_pairs = H_C // 2
    stack = lambda z: jnp.concatenate([z[:, :, p * PAIR:(p + 1) * PAIR] for p in range(n_pairs)], axis=0)
    atp, btp, ktp, rtp, vp = stack(at), stack(bt), stack(kt), stack(rt), stack(v3)
    lhs = jnp.concatenate([atp, rtp], axis=1)
    rhs = jnp.concatenate([_pair_rows(btp), _pair_rows(ktp)], axis=1)
    gm = jnp.where(tri4, _dot(lhs, rhs, _BNT), 0.0)
    top = gm[:, 0:c, :]
    bot = gm[:, c:2 * c, :]
    v_rows = _pair_rows(vp)
    s = jnp.concatenate([sp_ref[:, p] for p in range(n_pairs)], axis=0)
    rhs_u = _dot(atp, s, _BNT) + _dot(top[:, :, 2 * c:4 * c], v_rows, _BNN)
    u = _pair_solve(top[:, :, 0:2 * c], rhs_u, c)
    o = _dot(rtp, s, _BNT) + _dot(bot, jnp.concatenate([_pair_rows(u), v_rows], axis=1), _BNN)
    s_new = s + _dot(jnp.concatenate([u, vp], axis=1), jnp.concatenate([btp, ktp], axis=1), _BTN)
    s_new = jnp.where(diag_blocks, s_new, 0.0) * stack(p_in[:, c - 1:c, :])
    for p in range(n_pairs):
        sp_ref[:, p] = s_new[p * g_b:(p + 1) * g_b]
    outs = [o[p * g_b:(p + 1) * g_b] for p in range(n_pairs)]
    o = jnp.concatenate(outs, axis=-1).reshape(m, W_C)
    inv_n = 1.0 / HEAD_DIM
    mean = _select_dot(o, head_sum, 2) * inv_n
    var = _select_dot(jnp.square(o - mean), head_sum, 2) * inv_n
    o = (o - mean) * lax.rsqrt(var + RWKV_GN_EPS) * lnx_w + lnx_b
    k2 = k3.reshape(m, W_C)
    o = o + _select_dot(r * k2 * r_k, head_sum, 2) * v
    o_ref[...] = (o * gate).reshape(g_b, c, W_C)

    @pl.when(ci == pl.num_programs(1) - 1)
    def _():
        _store_pair_state(s_ref, sp_ref, H_C)


def _rwkv(c_cols, prev, s0, mu, vecs, w2p, a2p, g2, t_valid):
    b, t, _ = c_cols.shape
    g_b = min(GROUP_B, b)
    c = CHUNK
    kern = functools.partial(_rwkv_kernel, t_valid=t_valid)
    state = pl.BlockSpec((g_b, H_C, HEAD_DIM, HEAD_DIM), lambda bi, ci: (bi, 0, 0, 0))
    return pl.pallas_call(
        kern,
        grid=(b // g_b, t // c),
        in_specs=[pl.BlockSpec((g_b, c, C_COLS), lambda bi, ci: (bi, ci, 0)),
                  pl.BlockSpec((g_b, 1, C_COLS), lambda bi, ci: (bi, 0, 0)),
                  state, _const_spec(mu.shape), _const_spec(vecs.shape), _const_spec(w2p.shape),
                  _const_spec(a2p.shape), _const_spec(g2.shape)],
        out_specs=[pl.BlockSpec((g_b, c, W_C), lambda bi, ci: (bi, ci, 0)), state],
        out_shape=[jax.ShapeDtypeStruct((b, t, W_C), F32),
                   jax.ShapeDtypeStruct((b, H_C, HEAD_DIM, HEAD_DIM), F32)],
        scratch_shapes=[pltpu.VMEM((g_b, c + SUBLANES, C_COLS), F32),
                        pltpu.VMEM((g_b, H_C // 2, PAIR, PAIR), F32)],
        compiler_params=_params("arbitrary", "arbitrary"),
        name="rwkv_mixer",
    )(c_cols, prev, s0, mu, vecs, w2p, a2p, g2)


def _diff_lambda(lp_ref, lam_init):
    l1 = jnp.exp(jnp.sum(lp_ref[0:1, :] * lp_ref[1:2, :], axis=-1, keepdims=True))
    l2 = jnp.exp(jnp.sum(lp_ref[2:3, :] * lp_ref[3:4, :], axis=-1, keepdims=True))
    return l1 - l2 + lam_init


def _dattn_kernel(qi_ref, ki_ref, lp_ref, qt_ref, k_ref, vt_ref, sub_ref, o_ref, qm_ref, m_ref, l_ref, acc_ref,
                  *, lam_init):
    step = pl.program_id(1)
    qi = qi_ref[step]
    ki = ki_ref[step]
    tq = qt_ref.shape[2]
    tk = k_ref.shape[1]

    @pl.when(ki == 0)
    def _():
        m_ref[...] = jnp.full(m_ref.shape, -jnp.inf, F32)
        l_ref[...] = jnp.zeros(l_ref.shape, F32)
        acc_ref[...] = jnp.zeros(acc_ref.shape, F32)
        fmap = lax.broadcasted_iota(jnp.int32, (W_B, 1), 0) // DB
        qt = qt_ref[0]
        for hm in range(2 * H_B):
            qm_ref[hm] = jnp.where(fmap == hm, qt, jnp.zeros_like(qt))

    def scores(hm, diagonal):
        st = jnp.dot(k_ref[0], qm_ref[hm], preferred_element_type=F32)
        if diagonal:
            kpos = lax.broadcasted_iota(jnp.int32, (tk, tq), 0)
            qpos = lax.broadcasted_iota(jnp.int32, (tk, tq), 1)
            st = jnp.where(kpos <= qpos, st, -jnp.inf)
        m = m_ref[hm]
        return st, m, jnp.maximum(m, jnp.max(st, axis=0, keepdims=True))

    def weights(hm, st, m, m_new):
        alpha = jnp.exp(m - m_new)
        p = jnp.exp(st - m_new)
        l_ref[hm] = alpha * l_ref[hm] + jnp.sum(p, axis=0, keepdims=True)
        m_ref[hm] = m_new
        return alpha, p.astype(BF16)

    def accumulate(hm, alpha, p):
        h = hm // 2
        pv = jnp.dot(vt_ref[0, h * HEAD_DIM:(h + 1) * HEAD_DIM, :], p, preferred_element_type=F32)
        acc_ref[hm] = alpha * acc_ref[hm] + pv

    def tile(diagonal):
        n = 2 * H_B
        sc = scores(0, diagonal)
        for hm in range(n):
            wt = weights(hm, *sc)
            if hm + 1 < n:
                sc = scores(hm + 1, diagonal)
            accumulate(hm, *wt)

    pl.when(ki < qi)(lambda: tile(False))

    @pl.when(ki == qi)
    def _():
        tile(True)
        lam = _diff_lambda(lp_ref, lam_init)
        heads = []
        for h in range(H_B):
            oh = acc_ref[2 * h] / l_ref[2 * h] - lam * (acc_ref[2 * h + 1] / l_ref[2 * h + 1])
            heads.append(oh * lax.rsqrt(jnp.mean(oh * oh, axis=0, keepdims=True) + NORM_EPS))
        o = jnp.concatenate(heads, axis=0).T
        o_ref[0] = o * sub_ref[...] * (1.0 - lam_init)


def _dattn_prompt(qt, k, vt, lam_par, subln, lam_init):
    b, _, t = qt.shape
    tile = _pick_tile(t, ATT_TILE)
    n = t // tile
    pairs = [(qi, ki) for qi in range(n) for ki in range(qi + 1)]
    qi_tab = jnp.asarray(np.array([p[0] for p in pairs], np.int32))
    ki_tab = jnp.asarray(np.array([p[1] for p in pairs], np.int32))
    const = lambda shape: pl.BlockSpec(shape, lambda bi, s, qi, ki: (0,) * len(shape))
    grid_spec = pltpu.PrefetchScalarGridSpec(
        num_scalar_prefetch=2,
        grid=(b, len(pairs)),
        in_specs=[const(lam_par.shape),
                  pl.BlockSpec((1, W_B, tile), lambda bi, s, qi, ki: (bi, 0, qi[s])),
                  pl.BlockSpec((1, tile, W_B), lambda bi, s, qi, ki: (bi, ki[s], 0)),
                  pl.BlockSpec((1, W_B, tile), lambda bi, s, qi, ki: (bi, 0, ki[s])),
                  const(subln.shape)],
        out_specs=pl.BlockSpec((1, tile, W_B), lambda bi, s, qi, ki: (bi, qi[s], 0)),
        scratch_shapes=[pltpu.VMEM((2 * H_B, W_B, tile), BF16), pltpu.VMEM((2 * H_B, 1, tile), F32),
                        pltpu.VMEM((2 * H_B, 1, tile), F32), pltpu.VMEM((2 * H_B, HEAD_DIM, tile), F32)],
    )
    return pl.pallas_call(
        functools.partial(_dattn_kernel, lam_init=lam_init),
        grid_spec=grid_spec,
        out_shape=jax.ShapeDtypeStruct((b, t, W_B), F32),
        compiler_params=_params("arbitrary", "arbitrary"),
        name="diff_attn_prompt",
    )(qi_tab, ki_tab, lam_par, qt, k, vt, subln)


def _softmax_rows(s, m_ref, l_ref):
    m_old = m_ref[...]
    m_new = jnp.maximum(m_old, jnp.max(s, axis=-1, keepdims=True))
    alpha = jnp.exp(m_old - m_new)
    p = jnp.exp(s - m_new)
    l_ref[...] = alpha * l_ref[...] + jnp.sum(p, axis=-1, keepdims=True)
    m_ref[...] = m_new
    return p, alpha


def _paged_kernel(pt_ref, lp_ref, q_ref, kn_ref, vn_ref, sub_ref, *refs, pages, lam_init, t_valid):
    k_pages = refs[0:pages]
    v_pages = refs[pages:2 * pages]
    o_ref, m_ref, l_ref, acc_ref = refs[2 * pages:]
    j = pl.program_id(1)
    tp = q_ref.shape[1]
    maps = lax.broadcasted_iota(jnp.int32, (1, W_B), 1) // DB

    @pl.when(j == 0)
    def _():
        m_ref[...] = jnp.full(m_ref.shape, -jnp.inf, F32)
        l_ref[...] = jnp.zeros(l_ref.shape, F32)
        acc_ref[...] = jnp.zeros(acc_ref.shape, F32)

    q = q_ref[0]
    qm = jnp.concatenate([jnp.where(maps == hm, q, 0.0) for hm in range(2 * H_B)], axis=0).astype(BF16)
    scale = DB ** -0.5

    kt = jnp.concatenate([kp[0, 0].astype(BF16) for kp in k_pages], axis=1)
    vt = jnp.concatenate([vp[0, 0].astype(BF16) for vp in v_pages], axis=1)
    s = jnp.dot(qm, kt, preferred_element_type=F32) * scale
    p, alpha = _softmax_rows(s, m_ref, l_ref)
    acc_ref[...] = alpha * acc_ref[...] + _dot(p, vt, _NT)

    @pl.when(j == pl.num_programs(1) - 1)
    def _():
        rows = 2 * H_B * tp
        tq_pos = lax.broadcasted_iota(jnp.int32, (rows, tp), 0) % tp
        tk_pos = lax.broadcasted_iota(jnp.int32, (rows, tp), 1)
        s_new = _dot(qm, kn_ref[0], _NT) * scale
        s_new = jnp.where((tk_pos <= tq_pos) & (tk_pos < t_valid), s_new, -jnp.inf)
        p_new, alpha_new = _softmax_rows(s_new, m_ref, l_ref)
        res = (alpha_new * acc_ref[...] + _dot(p_new, vn_ref[0], _NN)) / l_ref[...]
        lam = _diff_lambda(lp_ref, lam_init)
        heads = lax.broadcasted_iota(jnp.int32, (1, W_B), 1) // HEAD_DIM
        o = jnp.zeros((tp, W_B), F32)
        for h in range(H_B):
            oh = res[2 * h * tp:(2 * h + 1) * tp] - lam * res[(2 * h + 1) * tp:(2 * h + 2) * tp]
            o = jnp.where(heads == h, oh, o)
        r = lax.broadcasted_iota(jnp.int32, (W_B, W_B), 0) // HEAD_DIM
        c = lax.broadcasted_iota(jnp.int32, (W_B, W_B), 1) // HEAD_DIM
        ms = _dot_f32(o * o, jnp.where(r == c, 1.0 / HEAD_DIM, 0.0).astype(F32), _NN)
        o_ref[0] = o * lax.rsqrt(ms + NORM_EPS) * sub_ref[...] * (1.0 - lam_init)


def _dattn_paged(q, k, v, kt_pool, vt_pool, layer, page_table, lam_par, subln, lam_init, t_valid):
    b, tp, _ = q.shape
    n_pages = page_table.shape[1]
    page = kt_pool.shape[3]
    pages = math.gcd(n_pages, PAGES_PER_STEP)
    rows = 2 * H_B * tp
    kern = functools.partial(_paged_kernel, pages=pages, lam_init=lam_init, t_valid=t_valid)

    def page_spec(i):
        return pl.BlockSpec((1, 1, W_B, page), lambda bi, j, pt: (layer, pt[bi, j * pages + i], 0, 0))

    tok = pl.BlockSpec((1, tp, W_B), lambda bi, j, pt: (bi, 0, 0))
    const = lambda shape: pl.BlockSpec(shape, lambda bi, j, pt: (0,) * len(shape))
    grid_spec = pltpu.PrefetchScalarGridSpec(
        num_scalar_prefetch=1,
        grid=(b, n_pages // pages),
        in_specs=[const(lam_par.shape), tok, tok, tok, const(subln.shape)]
        + [page_spec(i) for i in range(pages)] * 2,
        out_specs=tok,
        scratch_shapes=[pltpu.VMEM((rows, 1), F32), pltpu.VMEM((rows, 1), F32), pltpu.VMEM((rows, W_B), F32)],
    )
    return pl.pallas_call(
        kern,
        grid_spec=grid_spec,
        out_shape=jax.ShapeDtypeStruct((b, tp, W_B), F32),
        compiler_params=_params("arbitrary", "arbitrary"),
        name="diff_attn_paged",
    )(page_table, lam_par, q, k, v, subln, *([kt_pool] * pages), *([vt_pool] * pages))


def _mem_project_kernel(m_ref, g_ref, wk_ref, wv_ref, k_ref, v_ref):
    h = _rms(m_ref[...], g_ref[...]).astype(BF16)
    k_ref[...] = jnp.dot(h, wk_ref[...], preferred_element_type=F32)
    v_ref[...] = jnp.dot(h, wv_ref[...], preferred_element_type=F32)


def _mem_project(mem2d, g, wk, wv):
    n = mem2d.shape[0]
    tm = _pick_tile(n, ROW_TILE)
    return pl.pallas_call(
        _mem_project_kernel,
        grid=(n // tm,),
        in_specs=[pl.BlockSpec((tm, D_MODEL), lambda i: (i, 0)), _const_spec(g.shape), _const_spec(wk.shape),
                  _const_spec(wv.shape)],
        out_specs=[pl.BlockSpec((tm, MEM_INNER), lambda i: (i, 0))] * 2,
        out_shape=[jax.ShapeDtypeStruct((n, MEM_INNER), F32)] * 2,
        compiler_params=_params("arbitrary"),
        name="mem_project",
    )(mem2d, g, wk, wv)


def _mix_mem_kernel(x_ref, oa_ref, ob_ref, oc_ref, wout_ref, g_ref, wq_ref, mk_ref, mv_ref, wo_ref, out_ref):
    bb, tt, _ = x_ref.shape
    m = bb * tt
    x = x_ref[...].reshape(m, D_MODEL)
    mix = (_dot(oa_ref[...].reshape(m, W_A), wout_ref[0:W_A, :], _NN)
           + _dot(ob_ref[...].reshape(m, W_B), wout_ref[W_A:W_A + W_B, :], _NN)
           + _dot(oc_ref[...].reshape(m, W_C), wout_ref[W_A + W_B:D_MODEL, :], _NN))
    x = x + mix
    q = _dot(_rms(x, g_ref[...]), wq_ref[...], _NN)
    heads = []
    for h in range(MEM_HEADS):
        sl = slice(h * MEM_HD, (h + 1) * MEM_HD)
        qh = q[:, sl].reshape(bb, tt, MEM_HD)
        s = _dot(qh, mk_ref[:, :, sl], _BNT) * (MEM_HD ** -0.5)
        p = jnp.exp(s - jnp.max(s, axis=-1, keepdims=True))
        p = p / jnp.sum(p, axis=-1, keepdims=True)
        heads.append(_dot(p, mv_ref[:, :, sl], _BNN).reshape(m, MEM_HD))
    o = jnp.concatenate(heads, axis=-1)
    out_ref[...] = (x + _dot(o, wo_ref[...], _NN)).reshape(bb, tt, D_MODEL)


def _mix_mem(x, o_a, o_b, o_c, w_out, g, wq, mem_k, mem_v, wo, bb, tt):
    b, t, _ = x.shape
    mt = mem_k.shape[1]
    blk = lambda w: pl.BlockSpec((bb, tt, w), lambda bi, ti: (bi, ti, 0))
    mem = pl.BlockSpec((bb, mt, MEM_INNER), lambda bi, ti: (bi, 0, 0))
    return pl.pallas_call(
        _mix_mem_kernel,
        grid=(b // bb, t // tt),
        in_specs=[blk(D_MODEL), blk(W_A), blk(W_B), blk(W_C), _const_spec(w_out.shape), _const_spec(g.shape),
                  _const_spec(wq.shape), mem, mem, _const_spec(wo.shape)],
        out_specs=blk(D_MODEL),
        out_shape=jax.ShapeDtypeStruct((b, t, D_MODEL), F32),
        compiler_params=_params("arbitrary", "arbitrary"),
        name="mix_out_mem_attn",
    )(x, o_a, o_b, o_c, w_out, g, wq, mem_k, mem_v, wo)


FF_CHUNK = 256


def _ffn_kernel(x_ref, buf_ref, g_ref, wg_ref, wv_ref, cw_ref, cb_ref, wo_ref, gf_ref, out_ref, tail_ref,
                carry_ref, gp_ref, acc_ref, *, final_norm):
    ti = pl.program_id(1)
    bb, tt, _ = x_ref.shape
    m = bb * tt
    hist = SUBLANES - (FFN_CONV - 1)

    @pl.when(ti == 0)
    def _():
        carry_ref[:, hist:SUBLANES, :] = buf_ref[...]

    x = x_ref[...].reshape(m, D_MODEL)
    h = _rms(x, g_ref[...]).astype(BF16)
    for ck in range(D_FF // FF_CHUNK):
        sl = slice(ck * FF_CHUNK, (ck + 1) * FF_CHUNK)
        gate = jnp.dot(h, wg_ref[:, sl], preferred_element_type=F32).reshape(bb, tt, FF_CHUNK)
        val = jnp.dot(h, wv_ref[:, sl], preferred_element_type=F32)
        gp_ref[:, hist:SUBLANES, :] = carry_ref[:, hist:SUBLANES, sl]
        gp_ref[:, SUBLANES:SUBLANES + tt, :] = gate
        y = cb_ref[:, sl] + gate * cw_ref[FFN_CONV - 1:FFN_CONV, sl]
        for j in range(FFN_CONV - 1):
            y = y + gp_ref[:, hist + j:hist + j + tt, :] * cw_ref[j:j + 1, sl]
        carry_ref[:, :, sl] = gate[:, tt - SUBLANES:tt, :]
        y = y.reshape(m, FF_CHUNK)
        act = (y * _sigmoid(y) * val).astype(BF16)
        part = jnp.dot(act, wo_ref[sl, :], preferred_element_type=F32)
        if ck == 0:
            acc_ref[...] = x + part
        else:
            acc_ref[...] += part
    tail_ref[...] = carry_ref[...]
    y = acc_ref[...]
    if final_norm:
        y = _rms(y, gf_ref[...])
    out_ref[...] = y.reshape(bb, tt, D_MODEL)


def _ffn(x, buf, g, wg, wv, conv_w, conv_b, wo, g_final, bb, tt, final_norm):
    b, t, _ = x.shape
    kern = functools.partial(_ffn_kernel, final_norm=final_norm)
    blk = pl.BlockSpec((bb, tt, D_MODEL), lambda bi, ti: (bi, ti, 0))
    return pl.pallas_call(
        kern,
        grid=(b // bb, t // tt),
        in_specs=[blk, pl.BlockSpec((bb, FFN_CONV - 1, D_FF), lambda bi, ti: (bi, 0, 0)), _const_spec(g.shape),
                  _const_spec(wg.shape), _const_spec(wv.shape), _const_spec(conv_w.shape),
                  _const_spec(conv_b.shape), _const_spec(wo.shape), _const_spec(g_final.shape)],
        out_specs=[blk, pl.BlockSpec((bb, SUBLANES, D_FF), lambda bi, ti: (bi, 0, 0))],
        out_shape=[jax.ShapeDtypeStruct((b, t, D_MODEL), F32), jax.ShapeDtypeStruct((b, SUBLANES, D_FF), F32)],
        scratch_shapes=[pltpu.VMEM((bb, SUBLANES, D_FF), F32), pltpu.VMEM((bb, tt + SUBLANES, FF_CHUNK), F32),
                        pltpu.VMEM((bb * tt, D_MODEL), F32)],
        compiler_params=_params("arbitrary", "arbitrary"),
        name="conv_glu_ffn",
    )(x, buf, g, wg, wv, conv_w, conv_b, wo, g_final)


def _layer_weights(l, w):
    bf = lambda z: z.astype(BF16)
    row = lambda z: z.reshape(1, -1)
    w_in = w['w_in'][l]
    gates = jnp.pad(w_in[:, 4 * W_A:A_COLS], ((0, 0), (0, GATE_PAD - 2 * H_A)))
    pad_gate = lambda z: jnp.pad(z, (H_A, GATE_PAD - 2 * H_A)).reshape(1, GATE_PAD)
    zeros_lora = jnp.zeros((DECAY_LORA, W_C), F32)
    wb = w_in[:, A_COLS:A_COLS + B_COLS]
    return dict(
        norm_mix=row(w['norm_mix'][l]),
        wa=jnp.concatenate([w_in[:, 0:4 * W_A], gates], axis=1),
        wb=wb, wbt=wb.T,
        wc=w_in[:, A_COLS + B_COLS:],
        gdn_conv_w=w['gdn_conv_w'][l],
        gdn_gate=jnp.concatenate([pad_gate(w['gdn_a_log'][l]), pad_gate(w['gdn_dt_bias'][l])], axis=0),
        gdn_norm=row(jnp.tile(w['gdn_norm'][l], H_A)),
        lam_par=jnp.stack([w['diff_lq1'][l], w['diff_lk1'][l], w['diff_lq2'][l], w['diff_lk2'][l]]),
        subln=row(jnp.tile(w['diff_subln'][l], H_B)),
        rw_mu=row(w['rw_mu'][l]),
        rw_vecs=jnp.stack([w['rw_w0'][l], w['rw_a0'][l], w['rw_k_k'][l], w['rw_k_a'][l],
                           w['rw_r_k'][l].reshape(-1), w['rw_lnx_w'][l], w['rw_lnx_b'][l],
                           jnp.zeros((W_C,), F32)]),
        rw_w2=bf(jnp.concatenate([w['rw_w2'][l], zeros_lora], axis=0)),
        rw_a2=bf(jnp.concatenate([zeros_lora, w['rw_a2'][l]], axis=0)),
        rw_g2=bf(w['rw_g2'][l]),
        w_out=bf(w['w_out'][l]),
        norm_mem=row(w['norm_mem'][l]),
        norm_mem_kv=row(w['norm_mem_kv'][l]),
        mem_wq=bf(w['mem_wq'][l]), mem_wk=bf(w['mem_wk'][l]), mem_wv=bf(w['mem_wv'][l]),
        mem_wo=bf(w['mem_wo'][l]),
        norm_ffn=row(w['norm_ffn'][l]),
        ffn_wg=bf(w['ffn_w_in'][l][:, 0:D_FF]), ffn_wv=bf(w['ffn_w_in'][l][:, D_FF:]),
        ffn_conv_w=w['ffn_conv_w'][l], ffn_conv_b=row(w['ffn_conv_b'][l]),
        ffn_wo=bf(w['ffn_w_out'][l]),
        norm_final=row(w['norm_final']),
    )


def _trunk_layer(x, t_valid, mem_k, mem_v, gdn_buf, gdn_s, rw_prev, rw_s, ffn_buf, attn_fn, p,
                 proj_bb, mem_bb, ffn_bb, tile_t, final_norm, feature_major):
    b, t, _ = x.shape
    proj = _proj_in(x, p['norm_mix'], p['wa'], p['wb'], p['wbt'], p['wc'], proj_bb, tile_t, feature_major)
    if feature_major:
        a_cols, qt, k_tok, k_new, v_new, vt, c_cols = proj
        o_b = attn_fn(qt, k_tok, vt)
    else:
        a_cols, q_b, k_b, v_b, c_cols = proj
        o_b = attn_fn(q_b, k_b, v_b)
        k_new, v_new = k_b[:, 0:t_valid], v_b[:, 0:t_valid]
    pad_t = lambda z: jnp.pad(z, ((0, 0), (0, -t % CHUNK), (0, 0)))
    o_a, gdn_s_new = _gdn(pad_t(a_cols), gdn_buf, gdn_s, p['gdn_conv_w'], p['gdn_gate'], p['gdn_norm'], t_valid)
    o_c, rw_s_new = _rwkv(pad_t(c_cols), rw_prev.reshape(b, 1, C_COLS), rw_s, p['rw_mu'], p['rw_vecs'], p['rw_w2'],
                          p['rw_a2'], p['rw_g2'], t_valid)
    o_a, o_c = o_a[:, 0:t], o_c[:, 0:t]
    x = _mix_mem(x, o_a, o_b, o_c, p['w_out'], p['norm_mem'], p['mem_wq'], mem_k, mem_v, p['mem_wo'],
                 mem_bb, tile_t)
    x, tail = _ffn(x, ffn_buf, p['norm_ffn'], p['ffn_wg'], p['ffn_wv'], p['ffn_conv_w'], p['ffn_conv_b'],
                   p['ffn_wo'], p['norm_final'], ffn_bb, tile_t, final_norm)
    keep = min(t_valid, GDN_CONV - 1)
    gdn_buf_new = jnp.concatenate([gdn_buf[:, keep:], a_cols[:, t_valid - keep:t_valid, 0:3 * W_A]], axis=1)
    last = t_valid - (t - SUBLANES)
    ffn_buf_new = tail[:, last - (FFN_CONV - 1):last]
    return (x, k_new, v_new, gdn_buf_new, gdn_s_new, c_cols[:, t_valid - 1], rw_s_new, ffn_buf_new)


def kernel(x_prompt, x_sample, cache_diff_k, cache_diff_v, state_gdn_conv, state_gdn, state_rwkv_shift, state_rwkv, state_ffn_conv, cache_mem_k, cache_mem_v, page_table, mem_prompt, norm_mix, w_in, gdn_conv_w, gdn_a_log, gdn_dt_bias, gdn_norm, diff_lq1, diff_lk1, diff_lq2, diff_lk2, diff_subln, rw_mu, rw_w0, rw_w2, rw_a0, rw_a2, rw_g2, rw_k_k, rw_k_a, rw_r_k, rw_lnx_w, rw_lnx_b, w_out, norm_mem, norm_mem_kv, mem_wq, mem_wk, mem_wv, mem_wo, norm_ffn, ffn_w_in, ffn_conv_w, ffn_conv_b, ffn_w_out, norm_final):
    w = dict(norm_mix=norm_mix, w_in=w_in, gdn_conv_w=gdn_conv_w, gdn_a_log=gdn_a_log, gdn_dt_bias=gdn_dt_bias,
             gdn_norm=gdn_norm, diff_lq1=diff_lq1, diff_lk1=diff_lk1, diff_lq2=diff_lq2, diff_lk2=diff_lk2,
             diff_subln=diff_subln, rw_mu=rw_mu, rw_w0=rw_w0, rw_w2=rw_w2, rw_a0=rw_a0, rw_a2=rw_a2, rw_g2=rw_g2,
             rw_k_k=rw_k_k, rw_k_a=rw_k_a, rw_r_k=rw_r_k, rw_lnx_w=rw_lnx_w, rw_lnx_b=rw_lnx_b, w_out=w_out,
             norm_mem=norm_mem, norm_mem_kv=norm_mem_kv, mem_wq=mem_wq, mem_wk=mem_wk, mem_wv=mem_wv,
             mem_wo=mem_wo, norm_ffn=norm_ffn, ffn_w_in=ffn_w_in, ffn_conv_w=ffn_conv_w, ffn_conv_b=ffn_conv_b,
             ffn_w_out=ffn_w_out, norm_final=norm_final)
    depth = w_in.shape[0]
    bp, tp, _ = x_prompt.shape
    bs, ts, _ = x_sample.shape
    mt = mem_prompt.shape[1]
    ts_pad = -(-ts // SUBLANES) * SUBLANES
    n_pool, page = cache_diff_k.shape[1], cache_diff_k.shape[2]
    pool_view = lambda z: jnp.transpose(z, (0, 1, 3, 4, 2)).reshape(depth, n_pool, W_B, page)
    kt_pool = pool_view(cache_diff_k)
    vt_pool = pool_view(cache_diff_v)

    xp = x_prompt
    xs = jnp.pad(x_sample, ((0, 0), (0, ts_pad - ts), (0, 0)))
    prompt_tile = _pick_tile(tp, ROW_TILE)
    outs_p = [[] for _ in range(9)]
    outs_s = [[] for _ in range(7)]
    for l in range(depth):
        lam_init = 0.8 - 0.6 * math.exp(-0.3 * l)
        last = l == depth - 1
        p = _layer_weights(l, w)
        mk, mv = _mem_project(mem_prompt.reshape(bp * mt, D_MODEL), p['norm_mem_kv'], p['mem_wk'], p['mem_wv'])
        mk = mk.reshape(bp, mt, MEM_INNER)
        mv = mv.reshape(bp, mt, MEM_INNER)
        attn_p = functools.partial(_dattn_prompt, lam_par=p['lam_par'], subln=p['subln'], lam_init=lam_init)
        res = _trunk_layer(
            xp, tp, mk, mv,
            jnp.zeros((bp, GDN_CONV - 1, 3 * W_A), F32), jnp.zeros((bp, H_A, HEAD_DIM, HEAD_DIM), F32),
            jnp.zeros((bp, C_COLS), F32), jnp.zeros((bp, H_C, HEAD_DIM, HEAD_DIM), F32),
            jnp.zeros((bp, FFN_CONV - 1, D_FF), F32), attn_p, p, 1, 1, 1, prompt_tile, last, True)
        xp = res[0]
        for acc, val in zip(outs_p, res[1:] + (mk, mv)):
            acc.append(val)
        attn_s = functools.partial(_dattn_paged, kt_pool=kt_pool, vt_pool=vt_pool, layer=l, page_table=page_table,
                                   lam_par=p['lam_par'], subln=p['subln'], lam_init=lam_init, t_valid=ts)
        res = _trunk_layer(
            xs, ts, cache_mem_k[l].reshape(bs, mt, MEM_INNER), cache_mem_v[l].reshape(bs, mt, MEM_INNER),
            state_gdn_conv[l], state_gdn[l], state_rwkv_shift[l], state_rwkv[l], state_ffn_conv[l], attn_s, p,
            bs, min(GROUP_B, bs), bs, ts_pad, last, False)
        xs = res[0]
        for acc, val in zip(outs_s, res[1:]):
            acc.append(val)

    st = lambda vals, shape: jnp.stack(vals).reshape((depth,) + shape)
    p_k, p_v, p_gc, p_gs, p_rp, p_rs, p_fc, p_mk, p_mv = outs_p
    s_k, s_v, s_gc, s_gs, s_rp, s_rs, s_fc = outs_s
    kv_p = lambda vals: jnp.transpose(jnp.stack(vals).reshape(depth, bp, H_B, 2 * DB, tp), (0, 1, 4, 2, 3))
    kv_s = (bs, ts, H_B, 2 * DB)
    mem_shape = (bp, mt, MEM_HEADS, MEM_HD)
    return (xp, xs[:, 0:ts],
            kv_p(p_k), kv_p(p_v), jnp.stack(p_gc), jnp.stack(p_gs), jnp.stack(p_rp), jnp.stack(p_rs),
            jnp.stack(p_fc), st(p_mk, mem_shape), st(p_mv, mem_shape),
            st(s_k, kv_s), st(s_v, kv_s), jnp.stack(s_gc), jnp.stack(s_gs), jnp.stack(s_rp), jnp.stack(s_rs),
            jnp.stack(s_fc))
```

```python
import functools
import math

import numpy as np
import jax
import jax.numpy as jnp
from jax import lax
from jax.experimental import pallas as pl
from jax.experimental.pallas import tpu as pltpu

F32 = jnp.float32
BF16 = jnp.bfloat16

D_MODEL = 1024
HEAD_DIM = 64
W_A = 384
W_B = 256
W_C = 384
H_A = W_A // HEAD_DIM
H_B = W_B // HEAD_DIM
H_C = W_C // HEAD_DIM
DB = HEAD_DIM // 2
GDN_CONV = 4
DECAY_LORA = 64
AAA_LORA = 64
GATE_LORA = 128
RWKV_GN_EPS = 64e-5
MEM_HEADS = 4
MEM_HD = 128
MEM_INNER = MEM_HEADS * MEM_HD
D_FF = 2816
FFN_CONV = 3
NORM_EPS = 1e-6
A_COLS = 4 * W_A + 2 * H_A
B_COLS = 3 * W_B
C_COLS = 3 * W_C + DECAY_LORA + AAA_LORA + GATE_LORA
GATE_PAD = 128
A_PAD = 4 * W_A + GATE_PAD

SUBLANES = 8
LANES = 128
VMEM_LIMIT = 56 * 1024 * 1024
GROUP_B = 8
ROW_TILE = 512
ATT_TILE = 1024
PAGES_PER_STEP = 32

_NN = (((1,), (0,)), ((), ()))
_NT = (((1,), (1,)), ((), ()))
_BNN = (((2,), (1,)), ((0,), (0,)))
_BNT = (((2,), (2,)), ((0,), (0,)))
_BTN = (((1,), (1,)), ((0,), (0,)))


def _dot(a, b, dn):
    return lax.dot_general(a.astype(BF16), b.astype(BF16), dn, preferred_element_type=F32)


def _dot_f32(a, b, dn):
    return lax.dot_general(a, b, dn, precision=lax.Precision.HIGHEST, preferred_element_type=F32)


def _sigmoid(x):
    return 1.0 / (1.0 + jnp.exp(-x))


def _softplus(x):
    return jnp.maximum(x, 0.0) + jnp.log(1.0 + jnp.exp(-jnp.abs(x)))


def _rms(x, g, eps=NORM_EPS):
    return x * lax.rsqrt(jnp.mean(x * x, axis=-1, keepdims=True) + eps) * g


def _pick_tile(n, pref):
    if n <= pref:
        return n
    t = pref - pref % SUBLANES
    while n % t:
        t -= SUBLANES
    return t


def _params(*sem):
    return pltpu.CompilerParams(dimension_semantics=sem, vmem_limit_bytes=VMEM_LIMIT)


def _const_spec(shape):
    nd = len(shape)
    return pl.BlockSpec(shape, lambda *_: (0,) * nd, pipeline_mode=pl.Buffered(1))


def _proj_in_kernel(x_ref, g_ref, wa32_ref, wb32_ref, wbt32_ref, wc32_ref, *refs, feature_major):
    outs, (wa_ref, wb_ref, wbt_ref, wc_ref) = refs[:-4], refs[-4:]

    @pl.when((pl.program_id(0) == 0) & (pl.program_id(1) == 0))
    def _():
        wa_ref[...] = wa32_ref[...].astype(BF16)
        wb_ref[...] = wb32_ref[...].astype(BF16)
        wbt_ref[...] = wbt32_ref[...].astype(BF16)
        wc_ref[...] = wc32_ref[...].astype(BF16)

    bb, tt, _ = x_ref.shape
    m = bb * tt
    h = _rms(x_ref[...].reshape(m, D_MODEL), g_ref[...]).astype(BF16)
    cols = lambda w_ref, lo, n: jnp.dot(h, w_ref[:, lo:lo + n], preferred_element_type=F32)
    if feature_major:
        oa_ref, oqt_ref, ok_ref, okt_ref, ovt_ref, ovtb_ref, oc_ref = outs
        rows = lambda i: lax.dot_general(wbt_ref[i * W_B:(i + 1) * W_B, :], h, _NT, preferred_element_type=F32)
        oqt_ref[0] = (rows(0) * (DB ** -0.5)).astype(BF16)
        ok_ref[0] = cols(wb_ref, W_B, W_B).astype(BF16)
        okt_ref[0] = rows(1)
        vt = rows(2)
        ovt_ref[0] = vt
        ovtb_ref[0] = vt.astype(BF16)
    else:
        oa_ref, oq_ref, ok_ref, ov_ref, oc_ref = outs
        oq_ref[...] = cols(wb_ref, 0, W_B).reshape(bb, tt, W_B)
        ok_ref[...] = cols(wb_ref, W_B, W_B).reshape(bb, tt, W_B)
        ov_ref[...] = cols(wb_ref, 2 * W_B, W_B).reshape(bb, tt, W_B)
    oa_ref[...] = cols(wa_ref, 0, A_PAD).reshape(bb, tt, A_PAD)
    oc_ref[...] = cols(wc_ref, 0, C_COLS).reshape(bb, tt, C_COLS)


def _proj_in(x, g, wa, wb, wbt, wc, bb, tt, feature_major):
    b, t, _ = x.shape
    tok = lambda w: pl.BlockSpec((bb, tt, w), lambda bi, ti: (bi, ti, 0))
    feat = pl.BlockSpec((1, W_B, tt), lambda bi, ti: (bi, 0, ti))
    sds = jax.ShapeDtypeStruct
    if feature_major:
        assert bb == 1
        out_specs = [tok(A_PAD), feat, tok(W_B), feat, feat, feat, tok(C_COLS)]
        out_shape = [sds((b, t, A_PAD), F32), sds((b, W_B, t), BF16), sds((b, t, W_B), BF16),
                     sds((b, W_B, t), F32), sds((b, W_B, t), F32), sds((b, W_B, t), BF16), sds((b, t, C_COLS), F32)]
    else:
        out_specs = [tok(A_PAD), tok(W_B), tok(W_B), tok(W_B), tok(C_COLS)]
        out_shape = [sds((b, t, w), F32) for w in (A_PAD, W_B, W_B, W_B, C_COLS)]
    return pl.pallas_call(
        functools.partial(_proj_in_kernel, feature_major=feature_major),
        grid=(b // bb, t // tt),
        in_specs=[tok(D_MODEL), _const_spec(g.shape), _const_spec(wa.shape), _const_spec(wb.shape),
                  _const_spec(wbt.shape), _const_spec(wc.shape)],
        out_specs=out_specs,
        out_shape=out_shape,
        scratch_shapes=[pltpu.VMEM(z.shape, BF16) for z in (wa, wb, wbt, wc)],
        compiler_params=_params("arbitrary", "arbitrary"),
        name="proj_in",
    )(x, g, wa, wb, wbt, wc)


CHUNK = 64
PAIR = 2 * HEAD_DIM


def _tri(c, inclusive):
    r = lax.broadcasted_iota(jnp.int32, (c, c), 0)
    q = lax.broadcasted_iota(jnp.int32, (c, c), 1)
    return (q <= r) if inclusive else (q < r)


def _chunk_cumsum(x, c):
    tri = jnp.broadcast_to(_tri(c, True).astype(F32), (x.shape[0], c, c))
    return _dot_f32(tri, x, _BNN)


def _select_dot(x, sel, terms):
    acc = None
    rest = x
    for _ in range(terms):
        piece = rest.astype(BF16)
        part = jnp.dot(piece, sel, preferred_element_type=F32)
        acc = part if acc is None else acc + part
        rest = rest - piece.astype(F32)
    return acc


def _head_sum_matrix(width):
    r = lax.broadcasted_iota(jnp.int32, (width, width), 0) // HEAD_DIM
    q = lax.broadcasted_iota(jnp.int32, (width, width), 1) // HEAD_DIM
    return jnp.where(r == q, 1.0, 0.0).astype(BF16)


def _head_spread_matrix(first_lane, width):
    r = lax.broadcasted_iota(jnp.int32, (LANES, width), 0)
    q = lax.broadcasted_iota(jnp.int32, (LANES, width), 1) // HEAD_DIM
    return jnp.where(r == q + first_lane, 1.0, 0.0).astype(BF16)


def _pair_rows(z):
    first = (lax.broadcasted_iota(jnp.int32, (1, 1, z.shape[-1]), 2) % PAIR) < HEAD_DIM
    return jnp.concatenate([jnp.where(first, z, 0.0), jnp.where(first, 0.0, z)], axis=1)


def _pair_solve(l, x, c):
    width = x.shape[-1]
    steps = int(math.log2(c))
    p = l
    for step in range(steps):
        if step == steps - 1:
            return x + _dot(p, _pair_rows(x), _BNN)
        both = _dot(p, _pair_rows(jnp.concatenate([x, p], axis=-1)), _BNN)
        x = x + both[:, :, 0:width]
        p = both[:, :, width:width + 2 * c]


def _load_pair_state(sp_ref, s0_ref, heads):
    sp_ref[...] = jnp.zeros(sp_ref.shape, F32)
    for p in range(heads // 2):
        sp_ref[:, p, 0:HEAD_DIM, 0:HEAD_DIM] = s0_ref[:, 2 * p]
        sp_ref[:, p, HEAD_DIM:PAIR, HEAD_DIM:PAIR] = s0_ref[:, 2 * p + 1]


def _store_pair_state(s_ref, sp_ref, heads):
    for p in range(heads // 2):
        s_ref[:, 2 * p] = sp_ref[:, p, 0:HEAD_DIM, 0:HEAD_DIM]
        s_ref[:, 2 * p + 1] = sp_ref[:, p, HEAD_DIM:PAIR, HEAD_DIM:PAIR]


def _pair_masks(c):
    i = lax.broadcasted_iota(jnp.int32, (c, 2 * c), 0)
    j = lax.broadcasted_iota(jnp.int32, (c, 2 * c), 1) % c
    r = lax.broadcasted_iota(jnp.int32, (PAIR, PAIR), 0) // HEAD_DIM
    q = lax.broadcasted_iota(jnp.int32, (PAIR, PAIR), 1) // HEAD_DIM
    return j < i, j <= i, r == q


def _gdn_kernel(a_ref, buf_ref, s0_ref, cw_ref, gp_ref, ng_ref, o_ref, s_ref, xp_ref, sp_ref, *, t_valid):
    ci = pl.program_id(1)
    g_b, c, _ = a_ref.shape
    m = g_b * c
    hist = SUBLANES - (GDN_CONV - 1)

    @pl.when(ci == 0)
    def _():
        xp_ref[:, hist:SUBLANES, :] = buf_ref[...]
        _load_pair_state(sp_ref, s0_ref, H_A)

    xp_ref[:, SUBLANES:SUBLANES + c, :] = a_ref[:, :, 0:3 * W_A]
    y = xp_ref[:, hist:hist + c, :] * cw_ref[0:1, :]
    for j in range(1, GDN_CONV):
        y = y + xp_ref[:, hist + j:hist + j + c, :] * cw_ref[j:j + 1, :]
    xp_ref[:, hist:SUBLANES, :] = xp_ref[:, c + hist:c + SUBLANES, :]
    qkv = (y * _sigmoid(y)).reshape(m, 3 * W_A)

    head_sum = _head_sum_matrix(W_A)
    q = qkv[:, 0:W_A]
    k = qkv[:, W_A:2 * W_A]
    v = qkv[:, 2 * W_A:3 * W_A]
    q = q * lax.rsqrt(_select_dot(q * q, head_sum, 2) + 1e-6) * (HEAD_DIM ** -0.5)
    k = k * lax.rsqrt(_select_dot(k * k, head_sum, 2) + 1e-6)

    gates = a_ref[:, :, 4 * W_A:A_PAD]
    tpos = ci * c + lax.broadcasted_iota(jnp.int32, (1, c, 1), 1)
    valid = tpos < t_valid
    beta_all = jnp.where(valid, _sigmoid(gates), 0.0)
    g_all = jnp.where(valid, -jnp.exp(gp_ref[0:1, :]) * _softplus(gates + gp_ref[1:2, :]), 0.0)
    gam_all = _chunk_cumsum(g_all, c)
    gam_t = jnp.swapaxes(gam_all, 1, 2)
    beta = _select_dot(beta_all.reshape(m, LANES), _head_spread_matrix(0, W_A), 3)
    gam = _select_dot(gam_all.reshape(m, LANES), _head_spread_matrix(H_A, W_A), 3)
    eg = jnp.exp(gam)
    kb = k * beta
    to3 = lambda z: z.reshape(g_b, c, W_A)
    gam3 = to3(gam)
    glast = gam3[:, c - 1:c, :]
    q3, k3, kb3, qe3, vb3, kbe3 = (to3(z) for z in (q, k, kb, q * eg, v * beta, kb * eg))
    kd3 = k3 * jnp.exp(glast - gam3)
    gl = jnp.exp(glast)
    strict, incl, diag_blocks = _pair_masks(c)

    n_pairs = H_A // 2
    stack = lambda z: jnp.concatenate([z[:, :, p * PAIR:(p + 1) * PAIR] for p in range(n_pairs)], axis=0)
    grow = jnp.concatenate(
        [jnp.concatenate([gam_t[:, H_A + 2 * p:H_A + 2 * p + 1, :], gam_t[:, H_A + 2 * p + 1:H_A + 2 * p + 2, :]],
                         axis=-1) for p in range(n_pairs)], axis=0)
    diff = stack(gam3) - grow
    k_rows = _pair_rows(stack(k3))
    kk = _dot(jnp.concatenate([stack(kb3), stack(q3)], axis=1), k_rows, _BNT)
    a_mat = kk[:, 0:c, :] * jnp.exp(jnp.where(strict, diff, -jnp.inf))
    qk = kk[:, c:2 * c, :] * jnp.exp(jnp.where(incl, diff, -jnp.inf))
    x = jnp.concatenate([stack(vb3), stack(kbe3)], axis=-1)
    x = _pair_solve(-a_mat, x, c)
    s = jnp.concatenate([sp_ref[:, p] for p in range(n_pairs)], axis=0)
    ws = _dot(jnp.concatenate([x[:, :, PAIR:2 * PAIR], stack(qe3)], axis=1), s, _BNN)
    v_new = x[:, :, 0:PAIR] - ws[:, 0:c, :]
    o = ws[:, c:2 * c, :] + _dot(qk, _pair_rows(v_new), _BNN)
    s_new = jnp.where(diag_blocks, s * stack(gl) + _dot(stack(kd3), v_new, _BTN), 0.0)
    for p in range(n_pairs):
        sp_ref[:, p] = s_new[p * g_b:(p + 1) * g_b]
    outs = [o[p * g_b:(p + 1) * g_b] for p in range(n_pairs)]
    o = jnp.concatenate(outs, axis=-1).reshape(m, W_A)
    o = o * lax.rsqrt(_select_dot(o * o, head_sum, 2) * (1.0 / HEAD_DIM) + NORM_EPS) * ng_ref[...]
    z = a_ref[:, :, 3 * W_A:4 * W_A].reshape(m, W_A)
    o_ref[...] = (o * (z * _sigmoid(z))).reshape(g_b, c, W_A)

    @pl.when(ci == pl.num_programs(1) - 1)
    def _():
        _store_pair_state(s_ref, sp_ref, H_A)


def _gdn(a_cols, conv_buf, s0, conv_w, gate_par, norm_g, t_valid):
    b, t, _ = a_cols.shape
    g_b = min(GROUP_B, b)
    c = CHUNK
    kern = functools.partial(_gdn_kernel, t_valid=t_valid)
    state = pl.BlockSpec((g_b, H_A, HEAD_DIM, HEAD_DIM), lambda bi, ci: (bi, 0, 0, 0))
    return pl.pallas_call(
        kern,
        grid=(b // g_b, t // c),
        in_specs=[pl.BlockSpec((g_b, c, A_PAD), lambda bi, ci: (bi, ci, 0)),
                  pl.BlockSpec((g_b, GDN_CONV - 1, 3 * W_A), lambda bi, ci: (bi, 0, 0)),
                  state, _const_spec(conv_w.shape), _const_spec(gate_par.shape), _const_spec(norm_g.shape)],
        out_specs=[pl.BlockSpec((g_b, c, W_A), lambda bi, ci: (bi, ci, 0)), state],
        out_shape=[jax.ShapeDtypeStruct((b, t, W_A), F32),
                   jax.ShapeDtypeStruct((b, H_A, HEAD_DIM, HEAD_DIM), F32)],
        scratch_shapes=[pltpu.VMEM((g_b, c + SUBLANES, 3 * W_A), F32),
                        pltpu.VMEM((g_b, H_A // 2, PAIR, PAIR), F32)],
        compiler_params=_params("arbitrary", "arbitrary"),
        name="gdn_mixer",
    )(a_cols, conv_buf, s0, conv_w, gate_par, norm_g)


def _rwkv_kernel(c_ref, prev_ref, s0_ref, mu_ref, vec_ref, w2_ref, a2_ref, g2_ref, o_ref, s_ref, xp_ref, sp_ref,
                 *, t_valid):
    ci = pl.program_id(1)
    g_b, c, _ = c_ref.shape
    m = g_b * c

    @pl.when(ci == 0)
    def _():
        xp_ref[:, SUBLANES - 1:SUBLANES, :] = prev_ref[...]
        _load_pair_state(sp_ref, s0_ref, H_C)

    x = c_ref[...]
    xp_ref[:, SUBLANES:SUBLANES + c, :] = x
    shifted = xp_ref[:, SUBLANES - 1:SUBLANES - 1 + c, :]
    xp_ref[:, SUBLANES - 1:SUBLANES, :] = xp_ref[:, c + SUBLANES - 1:c + SUBLANES, :]
    xs = (x + (shifted - x) * mu_ref[...]).reshape(m, C_COLS)

    w0, a0, k_k, k_a, r_k, lnx_w, lnx_b = (vec_ref[i:i + 1, :] for i in range(7))
    r = xs[:, 0:W_C]
    k = xs[:, W_C:2 * W_C]
    v = xs[:, 2 * W_C:3 * W_C]
    lora = xs[:, 3 * W_C:3 * W_C + DECAY_LORA + AAA_LORA]
    gd = xs[:, 3 * W_C + DECAY_LORA + AAA_LORA:C_COLS]
    lane = lax.broadcasted_iota(jnp.int32, (1, DECAY_LORA + AAA_LORA), 1)
    lora = jnp.where(lane < DECAY_LORA, jnp.tanh(lora), lora)
    w = -_softplus(-(w0 + _dot(lora, w2_ref[...], _NN))) - 0.5
    a = _sigmoid(a0 + _dot(lora, a2_ref[...], _NN))
    gate = _dot(_sigmoid(gd), g2_ref[...], _NN)
    head_sum = _head_sum_matrix(W_C)
    kk = k * k_k
    kk = kk * lax.rsqrt(_select_dot(kk * kk, head_sum, 2) + 1e-12)
    k = k * (1.0 + (a - 1.0) * k_a)

    tpos = ci * c + lax.broadcasted_iota(jnp.int32, (1, c, 1), 1)
    valid = tpos < t_valid
    to3 = lambda z: z.reshape(g_b, c, W_C)
    lw = jnp.where(valid, to3(-jnp.exp(w)), 0.0)
    cl = _chunk_cumsum(lw, c)
    p_in = jnp.exp(cl)
    p_inv = jnp.exp(-cl)
    k3 = jnp.where(valid, to3(k), 0.0)
    kk3 = jnp.where(valid, to3(kk), 0.0)
    v3 = to3(v)
    at = -kk3 * jnp.exp(cl - lw)
    bt = kk3 * to3(a) * p_inv
    kt = k3 * p_inv
    rt = to3(r) * p_in

    strict, incl, diag_blocks = _pair_masks(c)
    tri4 = jnp.concatenate([jnp.concatenate([strict, strict], axis=1),
                            jnp.concatenate([incl, incl], axis=1)], axis=0)

    n_pairs = H_C // 2
    stack = lambda z: jnp.concatenate([z[:, :, p * PAIR:(p + 1) * PAIR] for p in range(n_pairs)], axis=0)
    atp, btp, ktp, rtp, vp = stack(at), stack(bt), stack(kt), stack(rt), stack(v3)
    lhs = jnp.concatenate([atp, rtp], axis=1)
    rhs = jnp.concatenate([_pair_rows(btp), _pair_rows(ktp)], axis=1)
    gm = jnp.where(tri4, _dot(lhs, rhs, _BNT), 0.0)
    top = gm[:, 0:c, :]
    bot = gm[:, c:2 * c, :]
    v_rows = _pair_rows(vp)
    s = jnp.concatenate([sp_ref[:, p] for p in range(n_pairs)], axis=0)
    from_state = _dot(lhs, s, _BNT)
    rhs_u = from_state[:, 0:c, :] + _dot(top[:, :, 2 * c:4 * c], v_rows, _BNN)
    u = _pair_solve(top[:, :, 0:2 * c], rhs_u, c)
    o = from_state[:, c:2 * c, :] + _dot(bot, jnp.concatenate([_pair_rows(u), v_rows], axis=1), _BNN)
    s_new = s + _dot(jnp.concatenate([u, vp], axis=1), jnp.concatenate([btp, ktp], axis=1), _BTN)
    s_new = jnp.where(diag_blocks, s_new, 0.0) * stack(p_in[:, c - 1:c, :])
    for p in range(n_pairs):
        sp_ref[:, p] = s_new[p * g_b:(p + 1) * g_b]
    outs = [o[p * g_b:(p + 1) * g_b] for p in range(n_pairs)]
    o = jnp.concatenate(outs, axis=-1).reshape(m, W_C)
    inv_n = 1.0 / HEAD_DIM
    mean = _select_dot(o, head_sum, 2) * inv_n
    var = _select_dot(jnp.square(o - mean), head_sum, 2) * inv_n
    o = (o - mean) * lax.rsqrt(var + RWKV_GN_EPS) * lnx_w + lnx_b
    k2 = k3.reshape(m, W_C)
    o = o + _select_dot(r * k2 * r_k, head_sum, 2) * v
    o_ref[...] = (o * gate).reshape(g_b, c, W_C)

    @pl.when(ci == pl.num_programs(1) - 1)
    def _():
        _store_pair_state(s_ref, sp_ref, H_C)


def _rwkv(c_cols, prev, s0, mu, vecs, w2p, a2p, g2, t_valid):
    b, t, _ = c_cols.shape
    g_b = min(GROUP_B, b)
    c = CHUNK
    kern = functools.partial(_rwkv_kernel, t_valid=t_valid)
    state = pl.BlockSpec((g_b, H_C, HEAD_DIM, HEAD_DIM), lambda bi, ci: (bi, 0, 0, 0))
    return pl.pallas_call(
        kern,
        grid=(b // g_b, t // c),
        in_specs=[pl.BlockSpec((g_b, c, C_COLS), lambda bi, ci: (bi, ci, 0)),
                  pl.BlockSpec((g_b, 1, C_COLS), lambda bi, ci: (bi, 0, 0)),
                  state, _const_spec(mu.shape), _const_spec(vecs.shape), _const_spec(w2p.shape),
                  _const_spec(a2p.shape), _const_spec(g2.shape)],
        out_specs=[pl.BlockSpec((g_b, c, W_C), lambda bi, ci: (bi, ci, 0)), state],
        out_shape=[jax.ShapeDtypeStruct((b, t, W_C), F32),
                   jax.ShapeDtypeStruct((b, H_C, HEAD_DIM, HEAD_DIM), F32)],
        scratch_shapes=[pltpu.VMEM((g_b, c + SUBLANES, C_COLS), F32),
                        pltpu.VMEM((g_b, H_C // 2, PAIR, PAIR), F32)],
        compiler_params=_params("arbitrary", "arbitrary"),
        name="rwkv_mixer",
    )(c_cols, prev, s0, mu, vecs, w2p, a2p, g2)


def _diff_lambda(lp_ref, lam_init):
    l1 = jnp.exp(jnp.sum(lp_ref[0:1, :] * lp_ref[1:2, :], axis=-1, keepdims=True))
    l2 = jnp.exp(jnp.sum(lp_ref[2:3, :] * lp_ref[3:4, :], axis=-1, keepdims=True))
    return l1 - l2 + lam_init


def _dattn_kernel(qi_ref, ki_ref, lp_ref, qt_ref, k_ref, vt_ref, sub_ref, o_ref, qm_ref, m_ref, l_ref, acc_ref,
                  *, lam_init):
    step = pl.program_id(1)
    qi = qi_ref[step]
    ki = ki_ref[step]
    tq = qt_ref.shape[2]
    tk = k_ref.shape[1]

    @pl.when(ki == 0)
    def _():
        m_ref[...] = jnp.full(m_ref.shape, -jnp.inf, F32)
        l_ref[...] = jnp.zeros(l_ref.shape, F32)
        acc_ref[...] = jnp.zeros(acc_ref.shape, F32)
        fmap = lax.broadcasted_iota(jnp.int32, (W_B, 1), 0) // DB
        qt = qt_ref[0]
        for hm in range(2 * H_B):
            qm_ref[hm] = jnp.where(fmap == hm, qt, jnp.zeros_like(qt))

    def scores(hm, diagonal):
        st = jnp.dot(k_ref[0], qm_ref[hm], preferred_element_type=F32)
        if diagonal:
            kpos = lax.broadcasted_iota(jnp.int32, (tk, tq), 0)
            qpos = lax.broadcasted_iota(jnp.int32, (tk, tq), 1)
            st = jnp.where(kpos <= qpos, st, -jnp.inf)
        m = m_ref[hm]
        return st, m, jnp.maximum(m, jnp.max(st, axis=0, keepdims=True))

    def weights(hm, st, m, m_new):
        alpha = jnp.exp(m - m_new)
        p = jnp.exp(st - m_new)
        l_ref[hm] = alpha * l_ref[hm] + jnp.sum(p, axis=0, keepdims=True)
        m_ref[hm] = m_new
        return alpha, p.astype(BF16)

    def accumulate(hm, alpha, p):
        h = hm // 2
        pv = jnp.dot(vt_ref[0, h * HEAD_DIM:(h + 1) * HEAD_DIM, :], p, preferred_element_type=F32)
        acc_ref[hm] = alpha * acc_ref[hm] + pv

    def tile(diagonal):
        n = 2 * H_B
        sc = scores(0, diagonal)
        for hm in range(n):
            wt = weights(hm, *sc)
            if hm + 1 < n:
                sc = scores(hm + 1, diagonal)
            accumulate(hm, *wt)

    pl.when(ki < qi)(lambda: tile(False))

    @pl.when(ki == qi)
    def _():
        tile(True)
        lam = _diff_lambda(lp_ref, lam_init)
        heads = []
        for h in range(H_B):
            oh = acc_ref[2 * h] / l_ref[2 * h] - lam * (acc_ref[2 * h + 1] / l_ref[2 * h + 1])
            heads.append(oh * lax.rsqrt(jnp.mean(oh * oh, axis=0, keepdims=True) + NORM_EPS))
        o = jnp.concatenate(heads, axis=0).T
        o_ref[0] = o * sub_ref[...] * (1.0 - lam_init)


def _dattn_prompt(qt, k, vt, lam_par, subln, lam_init):
    b, _, t = qt.shape
    tile = _pick_tile(t, ATT_TILE)
    n = t // tile
    pairs = [(qi, ki) for qi in range(n) for ki in range(qi + 1)]
    qi_tab = jnp.asarray(np.array([p[0] for p in pairs], np.int32))
    ki_tab = jnp.asarray(np.array([p[1] for p in pairs], np.int32))
    const = lambda shape: pl.BlockSpec(shape, lambda bi, s, qi, ki: (0,) * len(shape))
    grid_spec = pltpu.PrefetchScalarGridSpec(
        num_scalar_prefetch=2,
        grid=(b, len(pairs)),
        in_specs=[const(lam_par.shape),
                  pl.BlockSpec((1, W_B, tile), lambda bi, s, qi, ki: (bi, 0, qi[s])),
                  pl.BlockSpec((1, tile, W_B), lambda bi, s, qi, ki: (bi, ki[s], 0)),
                  pl.BlockSpec((1, W_B, tile), lambda bi, s, qi, ki: (bi, 0, ki[s])),
                  const(subln.shape)],
        out_specs=pl.BlockSpec((1, tile, W_B), lambda bi, s, qi, ki: (bi, qi[s], 0)),
        scratch_shapes=[pltpu.VMEM((2 * H_B, W_B, tile), BF16), pltpu.VMEM((2 * H_B, 1, tile), F32),
                        pltpu.VMEM((2 * H_B, 1, tile), F32), pltpu.VMEM((2 * H_B, HEAD_DIM, tile), F32)],
    )
    return pl.pallas_call(
        functools.partial(_dattn_kernel, lam_init=lam_init),
        grid_spec=grid_spec,
        out_shape=jax.ShapeDtypeStruct((b, t, W_B), F32),
        compiler_params=_params("arbitrary", "arbitrary"),
        name="diff_attn_prompt",
    )(qi_tab, ki_tab, lam_par, qt, k, vt, subln)


def _softmax_rows(s, m_ref, l_ref):
    m_old = m_ref[...]
    m_new = jnp.maximum(m_old, jnp.max(s, axis=-1, keepdims=True))
    alpha = jnp.exp(m_old - m_new)
    p = jnp.exp(s - m_new)
    l_ref[...] = alpha * l_ref[...] + jnp.sum(p, axis=-1, keepdims=True)
    m_ref[...] = m_new
    return p, alpha


def _paged_kernel(pt_ref, lp_ref, q_ref, kn_ref, vn_ref, sub_ref, *refs, pages, lam_init, t_valid):
    k_pages = refs[0:pages]
    v_pages = refs[pages:2 * pages]
    o_ref, m_ref, l_ref, acc_ref = refs[2 * pages:]
    j = pl.program_id(1)
    tp = q_ref.shape[1]
    maps = lax.broadcasted_iota(jnp.int32, (1, W_B), 1) // DB

    @pl.when(j == 0)
    def _():
        m_ref[...] = jnp.full(m_ref.shape, -jnp.inf, F32)
        l_ref[...] = jnp.zeros(l_ref.shape, F32)
        acc_ref[...] = jnp.zeros(acc_ref.shape, F32)

    q = q_ref[0]
    qm = jnp.concatenate([jnp.where(maps == hm, q, 0.0) for hm in range(2 * H_B)], axis=0).astype(BF16)
    scale = DB ** -0.5

    kt = jnp.concatenate([kp[0, 0].astype(BF16) for kp in k_pages], axis=1)
    vt = jnp.concatenate([vp[0, 0].astype(BF16) for vp in v_pages], axis=1)
    s = jnp.dot(qm, kt, preferred_element_type=F32) * scale
    p, alpha = _softmax_rows(s, m_ref, l_ref)
    acc_ref[...] = alpha * acc_ref[...] + _dot(p, vt, _NT)

    @pl.when(j == pl.num_programs(1) - 1)
    def _():
        rows = 2 * H_B * tp
        tq_pos = lax.broadcasted_iota(jnp.int32, (rows, tp), 0) % tp
        tk_pos = lax.broadcasted_iota(jnp.int32, (rows, tp), 1)
        s_new = _dot(qm, kn_ref[0], _NT) * scale
        s_new = jnp.where((tk_pos <= tq_pos) & (tk_pos < t_valid), s_new, -jnp.inf)
        p_new, alpha_new = _softmax_rows(s_new, m_ref, l_ref)
        res = (alpha_new * acc_ref[...] + _dot(p_new, vn_ref[0], _NN)) / l_ref[...]
        lam = _diff_lambda(lp_ref, lam_init)
        heads = lax.broadcasted_iota(jnp.int32, (1, W_B), 1) // HEAD_DIM
        o = jnp.zeros((tp, W_B), F32)
        for h in range(H_B):
            oh = res[2 * h * tp:(2 * h + 1) * tp] - lam * res[(2 * h + 1) * tp:(2 * h + 2) * tp]
            o = jnp.where(heads == h, oh, o)
        r = lax.broadcasted_iota(jnp.int32, (W_B, W_B), 0) // HEAD_DIM
        c = lax.broadcasted_iota(jnp.int32, (W_B, W_B), 1) // HEAD_DIM
        ms = _dot_f32(o * o, jnp.where(r == c, 1.0 / HEAD_DIM, 0.0).astype(F32), _NN)
        o_ref[0] = o * lax.rsqrt(ms + NORM_EPS) * sub_ref[...] * (1.0 - lam_init)


def _dattn_paged(q, k, v, kt_pool, vt_pool, layer, page_table, lam_par, subln, lam_init, t_valid):
    b, tp, _ = q.shape
    n_pages = page_table.shape[1]
    page = kt_pool.shape[3]
    pages = math.gcd(n_pages, PAGES_PER_STEP)
    rows = 2 * H_B * tp
    kern = functools.partial(_paged_kernel, pages=pages, lam_init=lam_init, t_valid=t_valid)

    def page_spec(i):
        return pl.BlockSpec((1, 1, W_B, page), lambda bi, j, pt: (layer, pt[bi, j * pages + i], 0, 0))

    tok = pl.BlockSpec((1, tp, W_B), lambda bi, j, pt: (bi, 0, 0))
    const = lambda shape: pl.BlockSpec(shape, lambda bi, j, pt: (0,) * len(shape))
    grid_spec = pltpu.PrefetchScalarGridSpec(
        num_scalar_prefetch=1,
        grid=(b, n_pages // pages),
        in_specs=[const(lam_par.shape), tok, tok, tok, const(subln.shape)]
        + [page_spec(i) for i in range(pages)] * 2,
        out_specs=tok,
        scratch_shapes=[pltpu.VMEM((rows, 1), F32), pltpu.VMEM((rows, 1), F32), pltpu.VMEM((rows, W_B), F32)],
    )
    return pl.pallas_call(
        kern,
        grid_spec=grid_spec,
        out_shape=jax.ShapeDtypeStruct((b, tp, W_B), F32),
        compiler_params=_params("arbitrary", "arbitrary"),
        name="diff_attn_paged",
    )(page_table, lam_par, q, k, v, subln, *([kt_pool] * pages), *([vt_pool] * pages))


def _mem_project_kernel(m_ref, g_ref, wk_ref, wv_ref, k_ref, v_ref):
    h = _rms(m_ref[...], g_ref[...]).astype(BF16)
    k_ref[...] = jnp.dot(h, wk_ref[...], preferred_element_type=F32)
    v_ref[...] = jnp.dot(h, wv_ref[...], preferred_element_type=F32)


def _mem_project(mem2d, g, wk, wv):
    n = mem2d.shape[0]
    tm = _pick_tile(n, ROW_TILE)
    return pl.pallas_call(
        _mem_project_kernel,
        grid=(n // tm,),
        in_specs=[pl.BlockSpec((tm, D_MODEL), lambda i: (i, 0)), _const_spec(g.shape), _const_spec(wk.shape),
                  _const_spec(wv.shape)],
        out_specs=[pl.BlockSpec((tm, MEM_INNER), lambda i: (i, 0))] * 2,
        out_shape=[jax.ShapeDtypeStruct((n, MEM_INNER), F32)] * 2,
        compiler_params=_params("arbitrary"),
        name="mem_project",
    )(mem2d, g, wk, wv)


def _mix_mem_kernel(x_ref, oa_ref, ob_ref, oc_ref, wout_ref, g_ref, wq_ref, mk_ref, mv_ref, wo_ref, out_ref):
    bb, tt, _ = x_ref.shape
    m = bb * tt
    x = x_ref[...].reshape(m, D_MODEL)
    mix = (_dot(oa_ref[...].reshape(m, W_A), wout_ref[0:W_A, :], _NN)
           + _dot(ob_ref[...].reshape(m, W_B), wout_ref[W_A:W_A + W_B, :], _NN)
           + _dot(oc_ref[...].reshape(m, W_C), wout_ref[W_A + W_B:D_MODEL, :], _NN))
    x = x + mix
    q = _dot(_rms(x, g_ref[...]), wq_ref[...], _NN)
    heads = []
    for h in range(MEM_HEADS):
        sl = slice(h * MEM_HD, (h + 1) * MEM_HD)
        qh = q[:, sl].reshape(bb, tt, MEM_HD)
        s = _dot(qh, mk_ref[:, :, sl], _BNT) * (MEM_HD ** -0.5)
        p = jnp.exp(s - jnp.max(s, axis=-1, keepdims=True))
        p = p / jnp.sum(p, axis=-1, keepdims=True)
        heads.append(_dot(p, mv_ref[:, :, sl], _BNN).reshape(m, MEM_HD))
    o = jnp.concatenate(heads, axis=-1)
    out_ref[...] = (x + _dot(o, wo_ref[...], _NN)).reshape(bb, tt, D_MODEL)


def _mix_mem(x, o_a, o_b, o_c, w_out, g, wq, mem_k, mem_v, wo, bb, tt):
    b, t, _ = x.shape
    mt = mem_k.shape[1]
    blk = lambda w: pl.BlockSpec((bb, tt, w), lambda bi, ti: (bi, ti, 0))
    mem = pl.BlockSpec((bb, mt, MEM_INNER), lambda bi, ti: (bi, 0, 0))
    return pl.pallas_call(
        _mix_mem_kernel,
        grid=(b // bb, t // tt),
        in_specs=[blk(D_MODEL), blk(W_A), blk(W_B), blk(W_C), _const_spec(w_out.shape), _const_spec(g.shape),
                  _const_spec(wq.shape), mem, mem, _const_spec(wo.shape)],
        out_specs=blk(D_MODEL),
        out_shape=jax.ShapeDtypeStruct((b, t, D_MODEL), F32),
        compiler_params=_params("arbitrary", "arbitrary"),
        name="mix_out_mem_attn",
    )(x, o_a, o_b, o_c, w_out, g, wq, mem_k, mem_v, wo)


def _ffn_kernel(x_ref, buf_ref, g_ref, wg_ref, wv_ref, cw_ref, cb_ref, wo_ref, gf_ref, out_ref, tail_ref,
                carry_ref, gp_ref, *, final_norm):
    ti = pl.program_id(1)
    bb, tt, _ = x_ref.shape
    m = bb * tt
    hist = SUBLANES - (FFN_CONV - 1)

    @pl.when(ti == 0)
    def _():
        carry_ref[:, hist:SUBLANES, :] = buf_ref[...]

    x = x_ref[...].reshape(m, D_MODEL)
    h = _rms(x, g_ref[...]).astype(BF16)
    gate = jnp.dot(h, wg_ref[...], preferred_element_type=F32).reshape(bb, tt, D_FF)
    val = jnp.dot(h, wv_ref[...], preferred_element_type=F32)
    gp_ref[:, hist:SUBLANES, :] = carry_ref[:, hist:SUBLANES, :]
    gp_ref[:, SUBLANES:SUBLANES + tt, :] = gate
    y = cb_ref[...] + gate * cw_ref[FFN_CONV - 1:FFN_CONV, :]
    for j in range(FFN_CONV - 1):
        y = y + gp_ref[:, hist + j:hist + j + tt, :] * cw_ref[j:j + 1, :]
    carry_ref[...] = gate[:, tt - SUBLANES:tt, :]
    tail_ref[...] = gate[:, tt - SUBLANES:tt, :]
    y = y.reshape(m, D_FF)
    act = (y * _sigmoid(y) * val).astype(BF16)
    y = x + jnp.dot(act, wo_ref[...], preferred_element_type=F32)
    if final_norm:
        y = _rms(y, gf_ref[...])
    out_ref[...] = y.reshape(bb, tt, D_MODEL)


def _ffn(x, buf, g, wg, wv, conv_w, conv_b, wo, g_final, bb, tt, final_norm):
    b, t, _ = x.shape
    kern = functools.partial(_ffn_kernel, final_norm=final_norm)
    blk = pl.BlockSpec((bb, tt, D_MODEL), lambda bi, ti: (bi, ti, 0))
    return pl.pallas_call(
        kern,
        grid=(b // bb, t // tt),
        in_specs=[blk, pl.BlockSpec((bb, FFN_CONV - 1, D_FF), lambda bi, ti: (bi, 0, 0)), _const_spec(g.shape),
                  _const_spec(wg.shape), _const_spec(wv.shape), _const_spec(conv_w.shape),
                  _const_spec(conv_b.shape), _const_spec(wo.shape), _const_spec(g_final.shape)],
        out_specs=[blk, pl.BlockSpec((bb, SUBLANES, D_FF), lambda bi, ti: (bi, 0, 0))],
        out_shape=[jax.ShapeDtypeStruct((b, t, D_MODEL), F32), jax.ShapeDtypeStruct((b, SUBLANES, D_FF), F32)],
        scratch_shapes=[pltpu.VMEM((bb, SUBLANES, D_FF), F32), pltpu.VMEM((bb, tt + SUBLANES, D_FF), F32)],
        compiler_params=_params("arbitrary", "arbitrary"),
        name="conv_glu_ffn",
    )(x, buf, g, wg, wv, conv_w, conv_b, wo, g_final)


def _layer_weights(l, w):
    bf = lambda z: z.astype(BF16)
    row = lambda z: z.reshape(1, -1)
    w_in = w['w_in'][l]
    gates = jnp.pad(w_in[:, 4 * W_A:A_COLS], ((0, 0), (0, GATE_PAD - 2 * H_A)))
    pad_gate = lambda z: jnp.pad(z, (H_A, GATE_PAD - 2 * H_A)).reshape(1, GATE_PAD)
    zeros_lora = jnp.zeros((DECAY_LORA, W_C), F32)
    wb = w_in[:, A_COLS:A_COLS + B_COLS]
    return dict(
        norm_mix=row(w['norm_mix'][l]),
        wa=jnp.concatenate([w_in[:, 0:4 * W_A], gates], axis=1),
        wb=wb, wbt=wb.T,
        wc=w_in[:, A_COLS + B_COLS:],
        gdn_conv_w=w['gdn_conv_w'][l],
        gdn_gate=jnp.concatenate([pad_gate(w['gdn_a_log'][l]), pad_gate(w['gdn_dt_bias'][l])], axis=0),
        gdn_norm=row(jnp.tile(w['gdn_norm'][l], H_A)),
        lam_par=jnp.stack([w['diff_lq1'][l], w['diff_lk1'][l], w['diff_lq2'][l], w['diff_lk2'][l]]),
        subln=row(jnp.tile(w['diff_subln'][l], H_B)),
        rw_mu=row(w['rw_mu'][l]),
        rw_vecs=jnp.stack([w['rw_w0'][l], w['rw_a0'][l], w['rw_k_k'][l], w['rw_k_a'][l],
                           w['rw_r_k'][l].reshape(-1), w['rw_lnx_w'][l], w['rw_lnx_b'][l],
                           jnp.zeros((W_C,), F32)]),
        rw_w2=bf(jnp.concatenate([w['rw_w2'][l], zeros_lora], axis=0)),
        rw_a2=bf(jnp.concatenate([zeros_lora, w['rw_a2'][l]], axis=0)),
        rw_g2=bf(w['rw_g2'][l]),
        w_out=bf(w['w_out'][l]),
        norm_mem=row(w['norm_mem'][l]),
        norm_mem_kv=row(w['norm_mem_kv'][l]),
        mem_wq=bf(w['mem_wq'][l]), mem_wk=bf(w['mem_wk'][l]), mem_wv=bf(w['mem_wv'][l]),
        mem_wo=bf(w['mem_wo'][l]),
        norm_ffn=row(w['norm_ffn'][l]),
        ffn_wg=bf(w['ffn_w_in'][l][:, 0:D_FF]), ffn_wv=bf(w['ffn_w_in'][l][:, D_FF:]),
        ffn_conv_w=w['ffn_conv_w'][l], ffn_conv_b=row(w['ffn_conv_b'][l]),
        ffn_wo=bf(w['ffn_w_out'][l]),
        norm_final=row(w['norm_final']),
    )


def _trunk_layer(x, t_valid, mem_k, mem_v, gdn_buf, gdn_s, rw_prev, rw_s, ffn_buf, attn_fn, p,
                 proj_bb, mem_bb, ffn_bb, tile_t, final_norm, feature_major):
    b, t, _ = x.shape
    proj = _proj_in(x, p['norm_mix'], p['wa'], p['wb'], p['wbt'], p['wc'], proj_bb, tile_t, feature_major)
    if feature_major:
        a_cols, qt, k_tok, k_new, v_new, vt, c_cols = proj
        o_b = attn_fn(qt, k_tok, vt)
    else:
        a_cols, q_b, k_b, v_b, c_cols = proj
        o_b = attn_fn(q_b, k_b, v_b)
        k_new, v_new = k_b[:, 0:t_valid], v_b[:, 0:t_valid]
    pad_t = lambda z: jnp.pad(z, ((0, 0), (0, -t % CHUNK), (0, 0)))
    o_a, gdn_s_new = _gdn(pad_t(a_cols), gdn_buf, gdn_s, p['gdn_conv_w'], p['gdn_gate'], p['gdn_norm'], t_valid)
    o_c, rw_s_new = _rwkv(pad_t(c_cols), rw_prev.reshape(b, 1, C_COLS), rw_s, p['rw_mu'], p['rw_vecs'], p['rw_w2'],
                          p['rw_a2'], p['rw_g2'], t_valid)
    o_a, o_c = o_a[:, 0:t], o_c[:, 0:t]
    x = _mix_mem(x, o_a, o_b, o_c, p['w_out'], p['norm_mem'], p['mem_wq'], mem_k, mem_v, p['mem_wo'],
                 mem_bb, tile_t)
    x, tail = _ffn(x, ffn_buf, p['norm_ffn'], p['ffn_wg'], p['ffn_wv'], p['ffn_conv_w'], p['ffn_conv_b'],
                   p['ffn_wo'], p['norm_final'], ffn_bb, tile_t, final_norm)
    keep = min(t_valid, GDN_CONV - 1)
    gdn_buf_new = jnp.concatenate([gdn_buf[:, keep:], a_cols[:, t_valid - keep:t_valid, 0:3 * W_A]], axis=1)
    last = t_valid - (t - SUBLANES)
    ffn_buf_new = tail[:, last - (FFN_CONV - 1):last]
    return (x, k_new, v_new, gdn_buf_new, gdn_s_new, c_cols[:, t_valid - 1], rw_s_new, ffn_buf_new)


def kernel(x_prompt, x_sample, cache_diff_k, cache_diff_v, state_gdn_conv, state_gdn, state_rwkv_shift, state_rwkv, state_ffn_conv, cache_mem_k, cache_mem_v, page_table, mem_prompt, norm_mix, w_in, gdn_conv_w, gdn_a_log, gdn_dt_bias, gdn_norm, diff_lq1, diff_lk1, diff_lq2, diff_lk2, diff_subln, rw_mu, rw_w0, rw_w2, rw_a0, rw_a2, rw_g2, rw_k_k, rw_k_a, rw_r_k, rw_lnx_w, rw_lnx_b, w_out, norm_mem, norm_mem_kv, mem_wq, mem_wk, mem_wv, mem_wo, norm_ffn, ffn_w_in, ffn_conv_w, ffn_conv_b, ffn_w_out, norm_final):
    w = dict(norm_mix=norm_mix, w_in=w_in, gdn_conv_w=gdn_conv_w, gdn_a_log=gdn_a_log, gdn_dt_bias=gdn_dt_bias,
             gdn_norm=gdn_norm, diff_lq1=diff_lq1, diff_lk1=diff_lk1, diff_lq2=diff_lq2, diff_lk2=diff_lk2,
             diff_subln=diff_subln, rw_mu=rw_mu, rw_w0=rw_w0, rw_w2=rw_w2, rw_a0=rw_a0, rw_a2=rw_a2, rw_g2=rw_g2,
             rw_k_k=rw_k_k, rw_k_a=rw_k_a, rw_r_k=rw_r_k, rw_lnx_w=rw_lnx_w, rw_lnx_b=rw_lnx_b, w_out=w_out,
             norm_mem=norm_mem, norm_mem_kv=norm_mem_kv, mem_wq=mem_wq, mem_wk=mem_wk, mem_wv=mem_wv,
             mem_wo=mem_wo, norm_ffn=norm_ffn, ffn_w_in=ffn_w_in, ffn_conv_w=ffn_conv_w, ffn_conv_b=ffn_conv_b,
             ffn_w_out=ffn_w_out, norm_final=norm_final)
    depth = w_in.shape[0]
    bp, tp, _ = x_prompt.shape
    bs, ts, _ = x_sample.shape
    mt = mem_prompt.shape[1]
    ts_pad = -(-ts // SUBLANES) * SUBLANES
    n_pool, page = cache_diff_k.shape[1], cache_diff_k.shape[2]
    pool_view = lambda z: jnp.transpose(z, (0, 1, 3, 4, 2)).reshape(depth, n_pool, W_B, page)
    kt_pool = pool_view(cache_diff_k)
    vt_pool = pool_view(cache_diff_v)

    xp = x_prompt
    xs = jnp.pad(x_sample, ((0, 0), (0, ts_pad - ts), (0, 0)))
    prompt_tile = _pick_tile(tp, ROW_TILE)
    outs_p = [[] for _ in range(9)]
    outs_s = [[] for _ in range(7)]
    for l in range(depth):
        lam_init = 0.8 - 0.6 * math.exp(-0.3 * l)
        last = l == depth - 1
        p = _layer_weights(l, w)
        mk, mv = _mem_project(mem_prompt.reshape(bp * mt, D_MODEL), p['norm_mem_kv'], p['mem_wk'], p['mem_wv'])
        mk = mk.reshape(bp, mt, MEM_INNER)
        mv = mv.reshape(bp, mt, MEM_INNER)
        attn_p = functools.partial(_dattn_prompt, lam_par=p['lam_par'], subln=p['subln'], lam_init=lam_init)
        res = _trunk_layer(
            xp, tp, mk, mv,
            jnp.zeros((bp, GDN_CONV - 1, 3 * W_A), F32), jnp.zeros((bp, H_A, HEAD_DIM, HEAD_DIM), F32),
            jnp.zeros((bp, C_COLS), F32), jnp.zeros((bp, H_C, HEAD_DIM, HEAD_DIM), F32),
            jnp.zeros((bp, FFN_CONV - 1, D_FF), F32), attn_p, p, 1, 1, 1, prompt_tile, last, True)
        xp = res[0]
        for acc, val in zip(outs_p, res[1:] + (mk, mv)):
            acc.append(val)
        attn_s = functools.partial(_dattn_paged, kt_pool=kt_pool, vt_pool=vt_pool, layer=l, page_table=page_table,
                                   lam_par=p['lam_par'], subln=p['subln'], lam_init=lam_init, t_valid=ts)
        res = _trunk_layer(
            xs, ts, cache_mem_k[l].reshape(bs, mt, MEM_INNER), cache_mem_v[l].reshape(bs, mt, MEM_INNER),
            state_gdn_conv[l], state_gdn[l], state_rwkv_shift[l], state_rwkv[l], state_ffn_conv[l], attn_s, p,
            bs, min(GROUP_B, bs), bs, ts_pad, last, False)
        xs = res[0]
        for acc, val in zip(outs_s, res[1:]):
            acc.append(val)

    st = lambda vals, shape: jnp.stack(vals).reshape((depth,) + shape)
    p_k, p_v, p_gc, p_gs, p_rp, p_rs, p_fc, p_mk, p_mv = outs_p
    s_k, s_v, s_gc, s_gs, s_rp, s_rs, s_fc = outs_s
    kv_p = lambda vals: jnp.transpose(jnp.stack(vals).reshape(depth, bp, H_B, 2 * DB, tp), (0, 1, 4, 2, 3))
    kv_s = (bs, ts, H_B, 2 * DB)
    mem_shape = (bp, mt, MEM_HEADS, MEM_HD)
    return (xp, xs[:, 0:ts],
            kv_p(p_k), kv_p(p_v), jnp.stack(p_gc), jnp.stack(p_gs), jnp.stack(p_rp), jnp.stack(p_rs),
            jnp.stack(p_fc), st(p_mk, mem_shape), st(p_mv, mem_shape),
            st(s_k, kv_s), st(s_v, kv_s), jnp.stack(s_gc), jnp.stack(s_gs), jnp.stack(s_rp), jnp.stack(s_rs),
            jnp.stack(s_fc))
```

```python
import functools
import math

import numpy as np
import jax
import jax.numpy as jnp
from jax import lax
from jax.experimental import pallas as pl
from jax.experimental.pallas import tpu as pltpu

F32 = jnp.float32
BF16 = jnp.bfloat16

D_MODEL = 1024
HEAD_DIM = 64
W_A = 384
W_B = 256
W_C = 384
H_A = W_A // HEAD_DIM
H_B = W_B // HEAD_DIM
H_C = W_C // HEAD_DIM
DB = HEAD_DIM // 2
GDN_CONV = 4
DECAY_LORA = 64
AAA_LORA = 64
GATE_LORA = 128
RWKV_GN_EPS = 64e-5
MEM_HEADS = 4
MEM_HD = 128
MEM_INNER = MEM_HEADS * MEM_HD
D_FF = 2816
FFN_CONV = 3
NORM_EPS = 1e-6
A_COLS = 4 * W_A + 2 * H_A
B_COLS = 3 * W_B
C_COLS = 3 * W_C + DECAY_LORA + AAA_LORA + GATE_LORA
GATE_PAD = 128
A_PAD = 4 * W_A + GATE_PAD

SUBLANES = 8
LANES = 128
VMEM_LIMIT = 56 * 1024 * 1024
GROUP_B = 8
ROW_TILE = 512
ATT_TILE = 1024
PAGES_PER_STEP = 32

_NN = (((1,), (0,)), ((), ()))
_NT = (((1,), (1,)), ((), ()))
_BNN = (((2,), (1,)), ((0,), (0,)))
_BNT = (((2,), (2,)), ((0,), (0,)))
_BTN = (((1,), (1,)), ((0,), (0,)))


def _dot(a, b, dn):
    return lax.dot_general(a.astype(BF16), b.astype(BF16), dn, preferred_element_type=F32)


def _dot_f32(a, b, dn):
    return lax.dot_general(a, b, dn, precision=lax.Precision.HIGHEST, preferred_element_type=F32)


def _sigmoid(x):
    return 0.5 * jnp.tanh(0.5 * x) + 0.5


def _softplus(x):
    return jnp.maximum(x, 0.0) + jnp.log(1.0 + jnp.exp(-jnp.abs(x)))


def _rms(x, g, eps=NORM_EPS):
    return x * lax.rsqrt(jnp.mean(x * x, axis=-1, keepdims=True) + eps) * g


def _pick_tile(n, pref):
    if n <= pref:
        return n
    t = pref - pref % SUBLANES
    while n % t:
        t -= SUBLANES
    return t


def _params(*sem):
    return pltpu.CompilerParams(dimension_semantics=sem, vmem_limit_bytes=VMEM_LIMIT)


def _const_spec(shape):
    nd = len(shape)
    return pl.BlockSpec(shape, lambda *_: (0,) * nd, pipeline_mode=pl.Buffered(1))


def _proj_in_kernel(x_ref, g_ref, w32_ref, *refs, feature_major):
    outs, (wa_ref, wb_ref, wc_ref) = refs[:-3], refs[-3:]

    @pl.when((pl.program_id(0) == 0) & (pl.program_id(1) == 0))
    def _():
        wa_ref[:, 0:4 * W_A] = w32_ref[0, :, 0:4 * W_A].astype(BF16)
        gate_lane = lax.broadcasted_iota(jnp.int32, (1, GATE_PAD), 1) < 2 * H_A
        wa_ref[:, 4 * W_A:A_PAD] = jnp.where(gate_lane, w32_ref[0, :, 4 * W_A:A_PAD], 0.0).astype(BF16)
        wb_ref[...] = w32_ref[0, :, A_COLS:A_COLS + B_COLS].astype(BF16)
        wc_ref[...] = w32_ref[0, :, A_COLS + B_COLS:A_COLS + B_COLS + C_COLS].astype(BF16)

    bb, tt, _ = x_ref.shape
    m = bb * tt
    h = _rms(x_ref[...].reshape(m, D_MODEL), g_ref[...]).astype(BF16)
    cols = lambda w_ref, lo, n: jnp.dot(h, w_ref[:, lo:lo + n], preferred_element_type=F32)
    if feature_major:
        oa_ref, oqt_ref, ok_ref, okt_ref, ovt_ref, ovtb_ref, oc_ref = outs
        qkv = cols(wb_ref, 0, B_COLS)
        k = qkv[:, W_B:2 * W_B]
        vt = qkv[:, 2 * W_B:3 * W_B].T
        oqt_ref[0] = (qkv[:, 0:W_B] * (DB ** -0.5)).T.astype(BF16)
        ok_ref[0] = k.astype(BF16)
        okt_ref[0] = k.T
        ovt_ref[0] = vt
        ovtb_ref[0] = vt.astype(BF16)
    else:
        oa_ref, oq_ref, ok_ref, ov_ref, oc_ref = outs
        oq_ref[...] = cols(wb_ref, 0, W_B).reshape(bb, tt, W_B)
        ok_ref[...] = cols(wb_ref, W_B, W_B).reshape(bb, tt, W_B)
        ov_ref[...] = cols(wb_ref, 2 * W_B, W_B).reshape(bb, tt, W_B)
    oa_ref[...] = cols(wa_ref, 0, A_PAD).reshape(bb, tt, A_PAD)
    oc_ref[...] = cols(wc_ref, 0, C_COLS).reshape(bb, tt, C_COLS)


def _proj_in(x, g, w_in, layer, bb, tt, feature_major):
    b, t, _ = x.shape
    tok = lambda w: pl.BlockSpec((bb, tt, w), lambda bi, ti: (bi, ti, 0))
    feat = pl.BlockSpec((1, W_B, tt), lambda bi, ti: (bi, 0, ti))
    sds = jax.ShapeDtypeStruct
    if feature_major:
        assert bb == 1
        out_specs = [tok(A_PAD), feat, tok(W_B), feat, feat, feat, tok(C_COLS)]
        out_shape = [sds((b, t, A_PAD), F32), sds((b, W_B, t), BF16), sds((b, t, W_B), BF16),
                     sds((b, W_B, t), F32), sds((b, W_B, t), F32), sds((b, W_B, t), BF16), sds((b, t, C_COLS), F32)]
    else:
        out_specs = [tok(A_PAD), tok(W_B), tok(W_B), tok(W_B), tok(C_COLS)]
        out_shape = [sds((b, t, w), F32) for w in (A_PAD, W_B, W_B, W_B, C_COLS)]
    w_spec = pl.BlockSpec((1,) + w_in.shape[1:], lambda bi, ti: (layer, 0, 0), pipeline_mode=pl.Buffered(1))
    return pl.pallas_call(
        functools.partial(_proj_in_kernel, feature_major=feature_major),
        grid=(b // bb, t // tt),
        in_specs=[tok(D_MODEL), _const_spec(g.shape), w_spec],
        out_specs=out_specs,
        out_shape=out_shape,
        scratch_shapes=[pltpu.VMEM((D_MODEL, w), BF16) for w in (A_PAD, B_COLS, C_COLS)],
        compiler_params=_params("arbitrary", "arbitrary"),
        name="proj_in",
    )(x, g, w_in)


CHUNK = 64
PAIR = 2 * HEAD_DIM


def _tri(c, inclusive):
    r = lax.broadcasted_iota(jnp.int32, (c, c), 0)
    q = lax.broadcasted_iota(jnp.int32, (c, c), 1)
    return (q <= r) if inclusive else (q < r)


def _chunk_cumsum(x, c):
    tri = jnp.broadcast_to(_tri(c, True).astype(F32), (x.shape[0], c, c))
    return _dot_f32(tri, x, _BNN)


def _select_dot(x, sel, terms):
    acc = None
    rest = x
    for _ in range(terms):
        piece = rest.astype(BF16)
        part = jnp.dot(piece, sel, preferred_element_type=F32)
        acc = part if acc is None else acc + part
        rest = rest - piece.astype(F32)
    return acc


def _head_sum_matrix(width):
    r = lax.broadcasted_iota(jnp.int32, (width, width), 0) // HEAD_DIM
    q = lax.broadcasted_iota(jnp.int32, (width, width), 1) // HEAD_DIM
    return jnp.where(r == q, 1.0, 0.0).astype(BF16)


def _head_spread_matrix(first_lane, width):
    r = lax.broadcasted_iota(jnp.int32, (LANES, width), 0)
    q = lax.broadcasted_iota(jnp.int32, (LANES, width), 1) // HEAD_DIM
    return jnp.where(r == q + first_lane, 1.0, 0.0).astype(BF16)


def _pair_rows(z):
    first = (lax.broadcasted_iota(jnp.int32, (1, 1, z.shape[-1]), 2) % PAIR) < HEAD_DIM
    return jnp.concatenate([jnp.where(first, z, 0.0), jnp.where(first, 0.0, z)], axis=1)


def _pair_solve(l, x, c):
    width = x.shape[-1]
    steps = int(math.log2(c))
    p = l
    for step in range(steps):
        if step == steps - 1:
            return x + _dot(p, _pair_rows(x), _BNN)
        both = _dot(p, _pair_rows(jnp.concatenate([x, p], axis=-1)), _BNN)
        x = x + both[:, :, 0:width]
        p = both[:, :, width:width + 2 * c]


def _load_pair_state(sp_ref, s0_ref, heads):
    sp_ref[...] = jnp.zeros(sp_ref.shape, F32)
    for p in range(heads // 2):
        sp_ref[:, p, 0:HEAD_DIM, 0:HEAD_DIM] = s0_ref[:, 2 * p]
        sp_ref[:, p, HEAD_DIM:PAIR, HEAD_DIM:PAIR] = s0_ref[:, 2 * p + 1]


def _store_pair_state(s_ref, sp_ref, heads):
    for p in range(heads // 2):
        s_ref[:, 2 * p] = sp_ref[:, p, 0:HEAD_DIM, 0:HEAD_DIM]
        s_ref[:, 2 * p + 1] = sp_ref[:, p, HEAD_DIM:PAIR, HEAD_DIM:PAIR]


def _pair_masks(c):
    i = lax.broadcasted_iota(jnp.int32, (c, 2 * c), 0)
    j = lax.broadcasted_iota(jnp.int32, (c, 2 * c), 1) % c
    r = lax.broadcasted_iota(jnp.int32, (PAIR, PAIR), 0) // HEAD_DIM
    q = lax.broadcasted_iota(jnp.int32, (PAIR, PAIR), 1) // HEAD_DIM
    return j < i, j <= i, r == q


def _gdn_kernel(a_ref, buf_ref, s0_ref, cw_ref, gp_ref, ng_ref, o_ref, s_ref, xp_ref, sp_ref, *, t_valid):
    ci = pl.program_id(1)
    g_b, c, _ = a_ref.shape
    m = g_b * c
    hist = SUBLANES - (GDN_CONV - 1)

    @pl.when(ci == 0)
    def _():
        xp_ref[:, hist:SUBLANES, :] = buf_ref[...]
        _load_pair_state(sp_ref, s0_ref, H_A)

    xp_ref[:, SUBLANES:SUBLANES + c, :] = a_ref[:, :, 0:3 * W_A]
    y = xp_ref[:, hist:hist + c, :] * cw_ref[0:1, :]
    for j in range(1, GDN_CONV):
        y = y + xp_ref[:, hist + j:hist + j + c, :] * cw_ref[j:j + 1, :]
    xp_ref[:, hist:SUBLANES, :] = xp_ref[:, c + hist:c + SUBLANES, :]
    qkv = (y * _sigmoid(y)).reshape(m, 3 * W_A)

    head_sum = _head_sum_matrix(W_A)
    q = qkv[:, 0:W_A]
    k = qkv[:, W_A:2 * W_A]
    v = qkv[:, 2 * W_A:3 * W_A]
    q = q * lax.rsqrt(_select_dot(q * q, head_sum, 2) + 1e-6) * (HEAD_DIM ** -0.5)
    k = k * lax.rsqrt(_select_dot(k * k, head_sum, 2) + 1e-6)

    gates = a_ref[:, :, 4 * W_A:A_PAD]
    tpos = ci * c + lax.broadcasted_iota(jnp.int32, (1, c, 1), 1)
    valid = tpos < t_valid
    beta_all = jnp.where(valid, _sigmoid(gates), 0.0)
    g_all = jnp.where(valid, -jnp.exp(gp_ref[0:1, :]) * _softplus(gates + gp_ref[1:2, :]), 0.0)
    gam_all = _chunk_cumsum(g_all, c)
    gam_t = jnp.swapaxes(gam_all, 1, 2)
    beta = _select_dot(beta_all.reshape(m, LANES), _head_spread_matrix(0, W_A), 3)
    gam = _select_dot(gam_all.reshape(m, LANES), _head_spread_matrix(H_A, W_A), 3)
    eg = jnp.exp(gam)
    kb = k * beta
    to3 = lambda z: z.reshape(g_b, c, W_A)
    gam3 = to3(gam)
    glast = gam3[:, c - 1:c, :]
    q3, k3, kb3, qe3, vb3, kbe3 = (to3(z) for z in (q, k, kb, q * eg, v * beta, kb * eg))
    kd3 = k3 * jnp.exp(glast - gam3)
    gl = jnp.exp(glast)
    strict, incl, diag_blocks = _pair_masks(c)

    n_pairs = H_A // 2
    stack = lambda z: jnp.concatenate([z[:, :, p * PAIR:(p + 1) * PAIR] for p in range(n_pairs)], axis=0)
    grow = jnp.concatenate(
        [jnp.concatenate([gam_t[:, H_A + 2 * p:H_A + 2 * p + 1, :], gam_t[:, H_A + 2 * p + 1:H_A + 2 * p + 2, :]],
                         axis=-1) for p in range(n_pairs)], axis=0)
    diff = stack(gam3) - grow
    k_rows = _pair_rows(stack(k3))
    kk = _dot(jnp.concatenate([stack(kb3), stack(q3)], axis=1), k_rows, _BNT)
    a_mat = kk[:, 0:c, :] * jnp.exp(jnp.where(strict, diff, -jnp.inf))
    qk = kk[:, c:2 * c, :] * jnp.exp(jnp.where(incl, diff, -jnp.inf))
    x = jnp.concatenate([stack(vb3), stack(kbe3)], axis=-1)
    x = _pair_solve(-a_mat, x, c)
    s = jnp.concatenate([sp_ref[:, p] for p in range(n_pairs)], axis=0)
    ws = _dot(jnp.concatenate([x[:, :, PAIR:2 * PAIR], stack(qe3)], axis=1), s, _BNN)
    v_new = x[:, :, 0:PAIR] - ws[:, 0:c, :]
    o = ws[:, c:2 * c, :] + _dot(qk, _pair_rows(v_new), _BNN)
    s_new = jnp.where(diag_blocks, s * stack(gl) + _dot(stack(kd3), v_new, _BTN), 0.0)
    for p in range(n_pairs):
        sp_ref[:, p] = s_new[p * g_b:(p + 1) * g_b]
    outs = [o[p * g_b:(p + 1) * g_b] for p in range(n_pairs)]
    o = jnp.concatenate(outs, axis=-1).reshape(m, W_A)
    o = o * lax.rsqrt(_select_dot(o * o, head_sum, 2) * (1.0 / HEAD_DIM) + NORM_EPS) * ng_ref[...]
    z = a_ref[:, :, 3 * W_A:4 * W_A].reshape(m, W_A)
    o_ref[...] = (o * (z * _sigmoid(z))).reshape(g_b, c, W_A)

    @pl.when(ci == pl.num_programs(1) - 1)
    def _():
        _store_pair_state(s_ref, sp_ref, H_A)


def _gdn(a_cols, conv_buf, s0, conv_w, gate_par, norm_g, t_valid):
    b, t, _ = a_cols.shape
    g_b = min(GROUP_B, b)
    c = CHUNK
    kern = functools.partial(_gdn_kernel, t_valid=t_valid)
    state = pl.BlockSpec((g_b, H_A, HEAD_DIM, HEAD_DIM), lambda bi, ci: (bi, 0, 0, 0))
    return pl.pallas_call(
        kern,
        grid=(b // g_b, t // c),
        in_specs=[pl.BlockSpec((g_b, c, A_PAD), lambda bi, ci: (bi, ci, 0)),
                  pl.BlockSpec((g_b, GDN_CONV - 1, 3 * W_A), lambda bi, ci: (bi, 0, 0)),
                  state, _const_spec(conv_w.shape), _const_spec(gate_par.shape), _const_spec(norm_g.shape)],
        out_specs=[pl.BlockSpec((g_b, c, W_A), lambda bi, ci: (bi, ci, 0)), state],
        out_shape=[jax.ShapeDtypeStruct((b, t, W_A), F32),
                   jax.ShapeDtypeStruct((b, H_A, HEAD_DIM, HEAD_DIM), F32)],
        scratch_shapes=[pltpu.VMEM((g_b, c + SUBLANES, 3 * W_A), F32),
                        pltpu.VMEM((g_b, H_A // 2, PAIR, PAIR), F32)],
        compiler_params=_params("arbitrary", "arbitrary"),
        name="gdn_mixer",
    )(a_cols, conv_buf, s0, conv_w, gate_par, norm_g)


def _rwkv_kernel(c_ref, prev_ref, s0_ref, mu_ref, vec_ref, w2_ref, a2_ref, g2_ref, o_ref, s_ref, xp_ref, sp_ref,
                 *, t_valid):
    ci = pl.program_id(1)
    g_b, c, _ = c_ref.shape
    m = g_b * c

    @pl.when(ci == 0)
    def _():
        xp_ref[:, SUBLANES - 1:SUBLANES, :] = prev_ref[...]
        _load_pair_state(sp_ref, s0_ref, H_C)

    x = c_ref[...]
    xp_ref[:, SUBLANES:SUBLANES + c, :] = x
    shifted = xp_ref[:, SUBLANES - 1:SUBLANES - 1 + c, :]
    xp_ref[:, SUBLANES - 1:SUBLANES, :] = xp_ref[:, c + SUBLANES - 1:c + SUBLANES, :]
    xs = (x + (shifted - x) * mu_ref[...]).reshape(m, C_COLS)

    w0, a0, k_k, k_a, r_k, lnx_w, lnx_b = (vec_ref[i:i + 1, :] for i in range(7))
    r = xs[:, 0:W_C]
    k = xs[:, W_C:2 * W_C]
    v = xs[:, 2 * W_C:3 * W_C]
    lora = xs[:, 3 * W_C:3 * W_C + DECAY_LORA + AAA_LORA]
    gd = xs[:, 3 * W_C + DECAY_LORA + AAA_LORA:C_COLS]
    lane = lax.broadcasted_iota(jnp.int32, (1, DECAY_LORA + AAA_LORA), 1)
    lora = jnp.where(lane < DECAY_LORA, jnp.tanh(lora), lora)
    w = -_softplus(-(w0 + _dot(lora, w2_ref[...], _NN))) - 0.5
    a = _sigmoid(a0 + _dot(lora, a2_ref[...], _NN))
    gate = _dot(_sigmoid(gd), g2_ref[...], _NN)
    head_sum = _head_sum_matrix(W_C)
    kk = k * k_k
    kk = kk * lax.rsqrt(_select_dot(kk * kk, head_sum, 2) + 1e-12)
    k = k * (1.0 + (a - 1.0) * k_a)

    tpos = ci * c + lax.broadcasted_iota(jnp.int32, (1, c, 1), 1)
    valid = tpos < t_valid
    to3 = lambda z: z.reshape(g_b, c, W_C)
    lw = jnp.where(valid, to3(-jnp.exp(w)), 0.0)
    cl = _chunk_cumsum(lw, c)
    p_in = jnp.exp(cl)
    p_inv = jnp.exp(-cl)
    k3 = jnp.where(valid, to3(k), 0.0)
    kk3 = jnp.where(valid, to3(kk), 0.0)
    v3 = to3(v)
    at = -kk3 * jnp.exp(cl - lw)
    bt = kk3 * to3(a) * p_inv
    kt = k3 * p_inv
    rt = to3(r) * p_in

    strict, incl, diag_blocks = _pair_masks(c)
    tri4 = jnp.concatenate([jnp.concatenate([strict, strict], axis=1),
                            jnp.concatenate([incl, incl], axis=1)], axis=0)

    n_pairs = H_C // 2
    stack = lambda z: jnp.concatenate([z[:, :, p * PAIR:(p + 1) * PAIR] for p in range(n_pairs)], axis=0)
    atp, btp, ktp, rtp, vp = stack(at), stack(bt), stack(kt), stack(rt), stack(v3)
    lhs = jnp.concatenate([atp, rtp], axis=1)
    rhs = jnp.concatenate([_pair_rows(btp), _pair_rows(ktp)], axis=1)
    gm = jnp.where(tri4, _dot(lhs, rhs, _BNT), 0.0)
    top = gm[:, 0:c, :]
    bot = gm[:, c:2 * c, :]
    v_rows = _pair_rows(vp)
    s = jnp.concatenate([sp_ref[:, p] for p in range(n_pairs)], axis=0)
    from_state = _dot(lhs, s, _BNT)
    rhs_u = from_state[:, 0:c, :] + _dot(top[:, :, 2 * c:4 * c], v_rows, _BNN)
    u = _pair_solve(top[:, :, 0:2 * c], rhs_u, c)
    o = from_state[:, c:2 * c, :] + _dot(bot, jnp.concatenate([_pair_rows(u), v_rows], axis=1), _BNN)
    s_new = s + _dot(jnp.concatenate([u, vp], axis=1), jnp.concatenate([btp, ktp], axis=1), _BTN)
    s_new = jnp.where(diag_blocks, s_new, 0.0) * stack(p_in[:, c - 1:c, :])
    for p in range(n_pairs):
        sp_ref[:, p] = s_new[p * g_b:(p + 1) * g_b]
    outs = [o[p * g_b:(p + 1) * g_b] for p in range(n_pairs)]
    o = jnp.concatenate(outs, axis=-1).reshape(m, W_C)
    inv_n = 1.0 / HEAD_DIM
    mean = _select_dot(o, head_sum, 2) * inv_n
    var = _select_dot(jnp.square(o - mean), head_sum, 2) * inv_n
    o = (o - mean) * lax.rsqrt(var + RWKV_GN_EPS) * lnx_w + lnx_b
    k2 = k3.reshape(m, W_C)
    o = o + _select_dot(r * k2 * r_k, head_sum, 2) * v
    o_ref[...] = (o * gate).reshape(g_b, c, W_C)

    @pl.when(ci == pl.num_programs(1) - 1)
    def _():
        _store_pair_state(s_ref, sp_ref, H_C)


def _rwkv(c_cols, prev, s0, mu, vecs, w2p, a2p, g2, t_valid):
    b, t, _ = c_cols.shape
    g_b = min(GROUP_B, b)
    c = CHUNK
    kern = functools.partial(_rwkv_kernel, t_valid=t_valid)
    state = pl.BlockSpec((g_b, H_C, HEAD_DIM, HEAD_DIM), lambda bi, ci: (bi, 0, 0, 0))
    return pl.pallas_call(
        kern,
        grid=(b // g_b, t // c),
        in_specs=[pl.BlockSpec((g_b, c, C_COLS), lambda bi, ci: (bi, ci, 0)),
                  pl.BlockSpec((g_b, 1, C_COLS), lambda bi, ci: (bi, 0, 0)),
                  state, _const_spec(mu.shape), _const_spec(vecs.shape), _const_spec(w2p.shape),
                  _const_spec(a2p.shape), _const_spec(g2.shape)],
        out_specs=[pl.BlockSpec((g_b, c, W_C), lambda bi, ci: (bi, ci, 0)), state],
        out_shape=[jax.ShapeDtypeStruct((b, t, W_C), F32),
                   jax.ShapeDtypeStruct((b, H_C, HEAD_DIM, HEAD_DIM), F32)],
        scratch_shapes=[pltpu.VMEM((g_b, c + SUBLANES, C_COLS), F32),
                        pltpu.VMEM((g_b, H_C // 2, PAIR, PAIR), F32)],
        compiler_params=_params("arbitrary", "arbitrary"),
        name="rwkv_mixer",
    )(c_cols, prev, s0, mu, vecs, w2p, a2p, g2)


def _diff_lambda(lp_ref, lam_init):
    l1 = jnp.exp(jnp.sum(lp_ref[0:1, :] * lp_ref[1:2, :], axis=-1, keepdims=True))
    l2 = jnp.exp(jnp.sum(lp_ref[2:3, :] * lp_ref[3:4, :], axis=-1, keepdims=True))
    return l1 - l2 + lam_init


def _dattn_kernel(qi_ref, ki_ref, lp_ref, qt_ref, k_ref, vt_ref, sub_ref, o_ref, qm_ref, m_ref, l_ref, acc_ref,
                  *, lam_init):
    step = pl.program_id(1)
    qi = qi_ref[step]
    ki = ki_ref[step]
    tq = qt_ref.shape[2]
    tk = k_ref.shape[1]

    @pl.when(ki == 0)
    def _():
        m_ref[...] = jnp.full(m_ref.shape, -jnp.inf, F32)
        l_ref[...] = jnp.zeros(l_ref.shape, F32)
        acc_ref[...] = jnp.zeros(acc_ref.shape, F32)
        fmap = lax.broadcasted_iota(jnp.int32, (W_B, 1), 0) // DB
        qt = qt_ref[0]
        for hm in range(2 * H_B):
            qm_ref[hm] = jnp.where(fmap == hm, qt, jnp.zeros_like(qt))

    def scores(hm, diagonal):
        st = jnp.dot(k_ref[0], qm_ref[hm], preferred_element_type=F32)
        if diagonal:
            kpos = lax.broadcasted_iota(jnp.int32, (tk, tq), 0)
            qpos = lax.broadcasted_iota(jnp.int32, (tk, tq), 1)
            st = jnp.where(kpos <= qpos, st, -jnp.inf)
        m = m_ref[hm]
        return st, m, jnp.maximum(m, jnp.max(st, axis=0, keepdims=True))

    def weights(hm, st, m, m_new):
        alpha = jnp.exp(m - m_new)
        p = jnp.exp(st - m_new)
        l_ref[hm] = alpha * l_ref[hm] + jnp.sum(p, axis=0, keepdims=True)
        m_ref[hm] = m_new
        return alpha, p.astype(BF16)

    def accumulate(hm, alpha, p):
        h = hm // 2
        pv = jnp.dot(vt_ref[0, h * HEAD_DIM:(h + 1) * HEAD_DIM, :], p, preferred_element_type=F32)
        acc_ref[hm] = alpha * acc_ref[hm] + pv

    def tile(diagonal):
        n = 2 * H_B
        sc = scores(0, diagonal)
        for hm in range(n):
            wt = weights(hm, *sc)
            if hm + 1 < n:
                sc = scores(hm + 1, diagonal)
            accumulate(hm, *wt)

    pl.when(ki < qi)(lambda: tile(False))

    @pl.when(ki == qi)
    def _():
        tile(True)
        lam = _diff_lambda(lp_ref, lam_init)
        heads = []
        for h in range(H_B):
            oh = acc_ref[2 * h] / l_ref[2 * h] - lam * (acc_ref[2 * h + 1] / l_ref[2 * h + 1])
            heads.append(oh * lax.rsqrt(jnp.mean(oh * oh, axis=0, keepdims=True) + NORM_EPS))
        o = jnp.concatenate(heads, axis=0).T
        o_ref[0] = o * sub_ref[...] * (1.0 - lam_init)


def _dattn_prompt(qt, k, vt, lam_par, subln, lam_init):
    b, _, t = qt.shape
    tile = _pick_tile(t, ATT_TILE)
    n = t // tile
    pairs = [(qi, ki) for qi in range(n) for ki in range(qi + 1)]
    qi_tab = jnp.asarray(np.array([p[0] for p in pairs], np.int32))
    ki_tab = jnp.asarray(np.array([p[1] for p in pairs], np.int32))
    const = lambda shape: pl.BlockSpec(shape, lambda bi, s, qi, ki: (0,) * len(shape))
    grid_spec = pltpu.PrefetchScalarGridSpec(
        num_scalar_prefetch=2,
        grid=(b, len(pairs)),
        in_specs=[const(lam_par.shape),
                  pl.BlockSpec((1, W_B, tile), lambda bi, s, qi, ki: (bi, 0, qi[s])),
                  pl.BlockSpec((1, tile, W_B), lambda bi, s, qi, ki: (bi, ki[s], 0)),
                  pl.BlockSpec((1, W_B, tile), lambda bi, s, qi, ki: (bi, 0, ki[s])),
                  const(subln.shape)],
        out_specs=pl.BlockSpec((1, tile, W_B), lambda bi, s, qi, ki: (bi, qi[s], 0)),
        scratch_shapes=[pltpu.VMEM((2 * H_B, W_B, tile), BF16), pltpu.VMEM((2 * H_B, 1, tile), F32),
                        pltpu.VMEM((2 * H_B, 1, tile), F32), pltpu.VMEM((2 * H_B, HEAD_DIM, tile), F32)],
    )
    return pl.pallas_call(
        functools.partial(_dattn_kernel, lam_init=lam_init),
        grid_spec=grid_spec,
        out_shape=jax.ShapeDtypeStruct((b, t, W_B), F32),
        compiler_params=_params("arbitrary", "arbitrary"),
        name="diff_attn_prompt",
    )(qi_tab, ki_tab, lam_par, qt, k, vt, subln)


def _softmax_rows(s, m_ref, l_ref):
    m_old = m_ref[...]
    m_new = jnp.maximum(m_old, jnp.max(s, axis=-1, keepdims=True))
    alpha = jnp.exp(m_old - m_new)
    p = jnp.exp(s - m_new)
    l_ref[...] = alpha * l_ref[...] + jnp.sum(p, axis=-1, keepdims=True)
    m_ref[...] = m_new
    return p, alpha


def _paged_kernel(pt_ref, lp_ref, q_ref, kn_ref, vn_ref, sub_ref, *refs, pages, lam_init, t_valid):
    k_pages = refs[0:pages]
    v_pages = refs[pages:2 * pages]
    o_ref, m_ref, l_ref, acc_ref = refs[2 * pages:]
    j = pl.program_id(1)
    tp = q_ref.shape[1]
    maps = lax.broadcasted_iota(jnp.int32, (1, W_B), 1) // DB

    @pl.when(j == 0)
    def _():
        m_ref[...] = jnp.full(m_ref.shape, -jnp.inf, F32)
        l_ref[...] = jnp.zeros(l_ref.shape, F32)
        acc_ref[...] = jnp.zeros(acc_ref.shape, F32)

    q = q_ref[0]
    qm = jnp.concatenate([jnp.where(maps == hm, q, 0.0) for hm in range(2 * H_B)], axis=0).astype(BF16)
    scale = DB ** -0.5

    kt = jnp.concatenate([kp[0, 0].astype(BF16) for kp in k_pages], axis=1)
    vt = jnp.concatenate([vp[0, 0].astype(BF16) for vp in v_pages], axis=1)
    s = jnp.dot(qm, kt, preferred_element_type=F32) * scale
    p, alpha = _softmax_rows(s, m_ref, l_ref)
    acc_ref[...] = alpha * acc_ref[...] + _dot(p, vt, _NT)

    @pl.when(j == pl.num_programs(1) - 1)
    def _():
        rows = 2 * H_B * tp
        tq_pos = lax.broadcasted_iota(jnp.int32, (rows, tp), 0) % tp
        tk_pos = lax.broadcasted_iota(jnp.int32, (rows, tp), 1)
        s_new = _dot(qm, kn_ref[0], _NT) * scale
        s_new = jnp.where((tk_pos <= tq_pos) & (tk_pos < t_valid), s_new, -jnp.inf)
        p_new, alpha_new = _softmax_rows(s_new, m_ref, l_ref)
        res = (alpha_new * acc_ref[...] + _dot(p_new, vn_ref[0], _NN)) / l_ref[...]
        lam = _diff_lambda(lp_ref, lam_init)
        heads = lax.broadcasted_iota(jnp.int32, (1, W_B), 1) // HEAD_DIM
        o = jnp.zeros((tp, W_B), F32)
        for h in range(H_B):
            oh = res[2 * h * tp:(2 * h + 1) * tp] - lam * res[(2 * h + 1) * tp:(2 * h + 2) * tp]
            o = jnp.where(heads == h, oh, o)
        r = lax.broadcasted_iota(jnp.int32, (W_B, W_B), 0) // HEAD_DIM
        c = lax.broadcasted_iota(jnp.int32, (W_B, W_B), 1) // HEAD_DIM
        ms = _dot_f32(o * o, jnp.where(r == c, 1.0 / HEAD_DIM, 0.0).astype(F32), _NN)
        o_ref[0] = o * lax.rsqrt(ms + NORM_EPS) * sub_ref[...] * (1.0 - lam_init)


def _dattn_paged(q, k, v, kt_pool, vt_pool, layer, page_table, lam_par, subln, lam_init, t_valid):
    b, tp, _ = q.shape
    n_pages = page_table.shape[1]
    page = kt_pool.shape[3]
    pages = math.gcd(n_pages, PAGES_PER_STEP)
    rows = 2 * H_B * tp
    kern = functools.partial(_paged_kernel, pages=pages, lam_init=lam_init, t_valid=t_valid)

    def page_spec(i):
        return pl.BlockSpec((1, 1, W_B, page), lambda bi, j, pt: (layer, pt[bi, j * pages + i], 0, 0))

    tok = pl.BlockSpec((1, tp, W_B), lambda bi, j, pt: (bi, 0, 0))
    const = lambda shape: pl.BlockSpec(shape, lambda bi, j, pt: (0,) * len(shape))
    grid_spec = pltpu.PrefetchScalarGridSpec(
        num_scalar_prefetch=1,
        grid=(b, n_pages // pages),
        in_specs=[const(lam_par.shape), tok, tok, tok, const(subln.shape)]
        + [page_spec(i) for i in range(pages)] * 2,
        out_specs=tok,
        scratch_shapes=[pltpu.VMEM((rows, 1), F32), pltpu.VMEM((rows, 1), F32), pltpu.VMEM((rows, W_B), F32)],
    )
    return pl.pallas_call(
        kern,
        grid_spec=grid_spec,
        out_shape=jax.ShapeDtypeStruct((b, tp, W_B), F32),
        compiler_params=_params("arbitrary", "arbitrary"),
        name="diff_attn_paged",
    )(page_table, lam_par, q, k, v, subln, *([kt_pool] * pages), *([vt_pool] * pages))


def _mem_project_kernel(m_ref, g_ref, wk_ref, wv_ref, k_ref, v_ref):
    h = _rms(m_ref[...], g_ref[...]).astype(BF16)
    k_ref[...] = jnp.dot(h, wk_ref[...], preferred_element_type=F32)
    v_ref[...] = jnp.dot(h, wv_ref[...], preferred_element_type=F32)


def _mem_project(mem2d, g, wk, wv):
    n = mem2d.shape[0]
    tm = _pick_tile(n, ROW_TILE)
    return pl.pallas_call(
        _mem_project_kernel,
        grid=(n // tm,),
        in_specs=[pl.BlockSpec((tm, D_MODEL), lambda i: (i, 0)), _const_spec(g.shape), _const_spec(wk.shape),
                  _const_spec(wv.shape)],
        out_specs=[pl.BlockSpec((tm, MEM_INNER), lambda i: (i, 0))] * 2,
        out_shape=[jax.ShapeDtypeStruct((n, MEM_INNER), F32)] * 2,
        compiler_params=_params("arbitrary"),
        name="mem_project",
    )(mem2d, g, wk, wv)


def _mix_mem_kernel(x_ref, oa_ref, ob_ref, oc_ref, wout_ref, g_ref, wq_ref, mk_ref, mv_ref, wo_ref, out_ref):
    bb, tt, _ = x_ref.shape
    m = bb * tt
    x = x_ref[...].reshape(m, D_MODEL)
    mixed = jnp.concatenate([oa_ref[...].reshape(m, W_A), ob_ref[...].reshape(m, W_B), oc_ref[...].reshape(m, W_C)],
                            axis=-1)
    x = x + _dot(mixed, wout_ref[...], _NN)
    q = _dot(_rms(x, g_ref[...]), wq_ref[...], _NN)
    heads = []
    for h in range(MEM_HEADS):
        sl = slice(h * MEM_HD, (h + 1) * MEM_HD)
        qh = q[:, sl].reshape(bb, tt, MEM_HD)
        s = _dot(qh, mk_ref[:, :, sl], _BNT) * (MEM_HD ** -0.5)
        p = jnp.exp(s - jnp.max(s, axis=-1, keepdims=True))
        p = p / jnp.sum(p, axis=-1, keepdims=True)
        heads.append(_dot(p, mv_ref[:, :, sl], _BNN).reshape(m, MEM_HD))
    o = jnp.concatenate(heads, axis=-1)
    out_ref[...] = (x + _dot(o, wo_ref[...], _NN)).reshape(bb, tt, D_MODEL)


def _mix_mem(x, o_a, o_b, o_c, w_out, g, wq, mem_k, mem_v, wo, bb, tt):
    b, t, _ = x.shape
    mt = mem_k.shape[1]
    blk = lambda w: pl.BlockSpec((bb, tt, w), lambda bi, ti: (bi, ti, 0))
    mem = pl.BlockSpec((bb, mt, MEM_INNER), lambda bi, ti: (bi, 0, 0))
    return pl.pallas_call(
        _mix_mem_kernel,
        grid=(b // bb, t // tt),
        in_specs=[blk(D_MODEL), blk(W_A), blk(W_B), blk(W_C), _const_spec(w_out.shape), _const_spec(g.shape),
                  _const_spec(wq.shape), mem, mem, _const_spec(wo.shape)],
        out_specs=blk(D_MODEL),
        out_shape=jax.ShapeDtypeStruct((b, t, D_MODEL), F32),
        compiler_params=_params("arbitrary", "arbitrary"),
        name="mix_out_mem_attn",
    )(x, o_a, o_b, o_c, w_out, g, wq, mem_k, mem_v, wo)


def _ffn_kernel(x_ref, buf_ref, g_ref, wg_ref, wv_ref, cw_ref, cb_ref, wo_ref, gf_ref, out_ref, tail_ref,
                carry_ref, gp_ref, *, final_norm):
    ti = pl.program_id(1)
    bb, tt, _ = x_ref.shape
    m = bb * tt
    hist = SUBLANES - (FFN_CONV - 1)

    @pl.when(ti == 0)
    def _():
        carry_ref[:, hist:SUBLANES, :] = buf_ref[...]

    x = x_ref[...].reshape(m, D_MODEL)
    h = _rms(x, g_ref[...]).astype(BF16)
    gate = jnp.dot(h, wg_ref[...], preferred_element_type=F32).reshape(bb, tt, D_FF)
    val = jnp.dot(h, wv_ref[...], preferred_element_type=F32)
    gp_ref[:, hist:SUBLANES, :] = carry_ref[:, hist:SUBLANES, :]
    gp_ref[:, SUBLANES:SUBLANES + tt, :] = gate
    y = cb_ref[...] + gate * cw_ref[FFN_CONV - 1:FFN_CONV, :]
    for j in range(FFN_CONV - 1):
        y = y + gp_ref[:, hist + j:hist + j + tt, :] * cw_ref[j:j + 1, :]
    carry_ref[...] = gate[:, tt - SUBLANES:tt, :]
    tail_ref[...] = gate[:, tt - SUBLANES:tt, :]
    y = y.reshape(m, D_FF)
    act = (y * _sigmoid(y) * val).astype(BF16)
    y = x + jnp.dot(act, wo_ref[...], preferred_element_type=F32)
    if final_norm:
        y = _rms(y, gf_ref[...])
    out_ref[...] = y.reshape(bb, tt, D_MODEL)


def _ffn(x, buf, g, wg, wv, conv_w, conv_b, wo, g_final, bb, tt, final_norm):
    b, t, _ = x.shape
    kern = functools.partial(_ffn_kernel, final_norm=final_norm)
    blk = pl.BlockSpec((bb, tt, D_MODEL), lambda bi, ti: (bi, ti, 0))
    return pl.pallas_call(
        kern,
        grid=(b // bb, t // tt),
        in_specs=[blk, pl.BlockSpec((bb, FFN_CONV - 1, D_FF), lambda bi, ti: (bi, 0, 0)), _const_spec(g.shape),
                  _const_spec(wg.shape), _const_spec(wv.shape), _const_spec(conv_w.shape),
                  _const_spec(conv_b.shape), _const_spec(wo.shape), _const_spec(g_final.shape)],
        out_specs=[blk, pl.BlockSpec((bb, SUBLANES, D_FF), lambda bi, ti: (bi, 0, 0))],
        out_shape=[jax.ShapeDtypeStruct((b, t, D_MODEL), F32), jax.ShapeDtypeStruct((b, SUBLANES, D_FF), F32)],
        scratch_shapes=[pltpu.VMEM((bb, SUBLANES, D_FF), F32), pltpu.VMEM((bb, tt + SUBLANES, D_FF), F32)],
        compiler_params=_params("arbitrary", "arbitrary"),
        name="conv_glu_ffn",
    )(x, buf, g, wg, wv, conv_w, conv_b, wo, g_final)


def _layer_weights(l, w):
    bf = lambda z: z.astype(BF16)
    row = lambda z: z.reshape(1, -1)
    pad_gate = lambda z: jnp.pad(z, (H_A, GATE_PAD - 2 * H_A)).reshape(1, GATE_PAD)
    zeros_lora = jnp.zeros((DECAY_LORA, W_C), F32)
    return dict(
        norm_mix=row(w['norm_mix'][l]),
        gdn_conv_w=w['gdn_conv_w'][l],
        gdn_gate=jnp.concatenate([pad_gate(w['gdn_a_log'][l]), pad_gate(w['gdn_dt_bias'][l])], axis=0),
        gdn_norm=row(jnp.tile(w['gdn_norm'][l], H_A)),
        lam_par=jnp.stack([w['diff_lq1'][l], w['diff_lk1'][l], w['diff_lq2'][l], w['diff_lk2'][l]]),
        subln=row(jnp.tile(w['diff_subln'][l], H_B)),
        rw_mu=row(w['rw_mu'][l]),
        rw_vecs=jnp.stack([w['rw_w0'][l], w['rw_a0'][l], w['rw_k_k'][l], w['rw_k_a'][l],
                           w['rw_r_k'][l].reshape(-1), w['rw_lnx_w'][l], w['rw_lnx_b'][l],
                           jnp.zeros((W_C,), F32)]),
        rw_w2=bf(jnp.concatenate([w['rw_w2'][l], zeros_lora], axis=0)),
        rw_a2=bf(jnp.concatenate([zeros_lora, w['rw_a2'][l]], axis=0)),
        rw_g2=bf(w['rw_g2'][l]),
        w_out=bf(w['w_out'][l]),
        norm_mem=row(w['norm_mem'][l]),
        norm_mem_kv=row(w['norm_mem_kv'][l]),
        mem_wq=bf(w['mem_wq'][l]), mem_wk=bf(w['mem_wk'][l]), mem_wv=bf(w['mem_wv'][l]),
        mem_wo=bf(w['mem_wo'][l]),
        norm_ffn=row(w['norm_ffn'][l]),
        ffn_wg=bf(w['ffn_w_in'][l][:, 0:D_FF]), ffn_wv=bf(w['ffn_w_in'][l][:, D_FF:]),
        ffn_conv_w=w['ffn_conv_w'][l], ffn_conv_b=row(w['ffn_conv_b'][l]),
        ffn_wo=bf(w['ffn_w_out'][l]),
        norm_final=row(w['norm_final']),
    )


def _trunk_layer(x, t_valid, mem_k, mem_v, gdn_buf, gdn_s, rw_prev, rw_s, ffn_buf, attn_fn, p, w_in, layer,
                 proj_bb, mem_bb, ffn_bb, tile_t, final_norm, feature_major):
    b, t, _ = x.shape
    proj = _proj_in(x, p['norm_mix'], w_in, layer, proj_bb, tile_t, feature_major)
    if feature_major:
        a_cols, qt, k_tok, k_new, v_new, vt, c_cols = proj
        o_b = attn_fn(qt, k_tok, vt)
    else:
        a_cols, q_b, k_b, v_b, c_cols = proj
        o_b = attn_fn(q_b, k_b, v_b)
        k_new, v_new = k_b[:, 0:t_valid], v_b[:, 0:t_valid]
    pad_t = lambda z: jnp.pad(z, ((0, 0), (0, -t % CHUNK), (0, 0)))
    o_a, gdn_s_new = _gdn(pad_t(a_cols), gdn_buf, gdn_s, p['gdn_conv_w'], p['gdn_gate'], p['gdn_norm'], t_valid)
    o_c, rw_s_new = _rwkv(pad_t(c_cols), rw_prev.reshape(b, 1, C_COLS), rw_s, p['rw_mu'], p['rw_vecs'], p['rw_w2'],
                          p['rw_a2'], p['rw_g2'], t_valid)
    o_a, o_c = o_a[:, 0:t], o_c[:, 0:t]
    x = _mix_mem(x, o_a, o_b, o_c, p['w_out'], p['norm_mem'], p['mem_wq'], mem_k, mem_v, p['mem_wo'],
                 mem_bb, tile_t)
    x, tail = _ffn(x, ffn_buf, p['norm_ffn'], p['ffn_wg'], p['ffn_wv'], p['ffn_conv_w'], p['ffn_conv_b'],
                   p['ffn_wo'], p['norm_final'], ffn_bb, tile_t, final_norm)
    keep = min(t_valid, GDN_CONV - 1)
    gdn_buf_new = jnp.concatenate([gdn_buf[:, keep:], a_cols[:, t_valid - keep:t_valid, 0:3 * W_A]], axis=1)
    last = t_valid - (t - SUBLANES)
    ffn_buf_new = tail[:, last - (FFN_CONV - 1):last]
    return (x, k_new, v_new, gdn_buf_new, gdn_s_new, c_cols[:, t_valid - 1], rw_s_new, ffn_buf_new)


def kernel(x_prompt, x_sample, cache_diff_k, cache_diff_v, state_gdn_conv, state_gdn, state_rwkv_shift, state_rwkv, state_ffn_conv, cache_mem_k, cache_mem_v, page_table, mem_prompt, norm_mix, w_in, gdn_conv_w, gdn_a_log, gdn_dt_bias, gdn_norm, diff_lq1, diff_lk1, diff_lq2, diff_lk2, diff_subln, rw_mu, rw_w0, rw_w2, rw_a0, rw_a2, rw_g2, rw_k_k, rw_k_a, rw_r_k, rw_lnx_w, rw_lnx_b, w_out, norm_mem, norm_mem_kv, mem_wq, mem_wk, mem_wv, mem_wo, norm_ffn, ffn_w_in, ffn_conv_w, ffn_conv_b, ffn_w_out, norm_final):
    w = dict(norm_mix=norm_mix, w_in=w_in, gdn_conv_w=gdn_conv_w, gdn_a_log=gdn_a_log, gdn_dt_bias=gdn_dt_bias,
             gdn_norm=gdn_norm, diff_lq1=diff_lq1, diff_lk1=diff_lk1, diff_lq2=diff_lq2, diff_lk2=diff_lk2,
             diff_subln=diff_subln, rw_mu=rw_mu, rw_w0=rw_w0, rw_w2=rw_w2, rw_a0=rw_a0, rw_a2=rw_a2, rw_g2=rw_g2,
             rw_k_k=rw_k_k, rw_k_a=rw_k_a, rw_r_k=rw_r_k, rw_lnx_w=rw_lnx_w, rw_lnx_b=rw_lnx_b, w_out=w_out,
             norm_mem=norm_mem, norm_mem_kv=norm_mem_kv, mem_wq=mem_wq, mem_wk=mem_wk, mem_wv=mem_wv,
             mem_wo=mem_wo, norm_ffn=norm_ffn, ffn_w_in=ffn_w_in, ffn_conv_w=ffn_conv_w, ffn_conv_b=ffn_conv_b,
             ffn_w_out=ffn_w_out, norm_final=norm_final)
    depth = w_in.shape[0]
    bp, tp, _ = x_prompt.shape
    bs, ts, _ = x_sample.shape
    mt = mem_prompt.shape[1]
    ts_pad = -(-ts // SUBLANES) * SUBLANES
    n_pool, page = cache_diff_k.shape[1], cache_diff_k.shape[2]
    pool_view = lambda z: jnp.transpose(z, (0, 1, 3, 4, 2)).reshape(depth, n_pool, W_B, page)
    kt_pool = pool_view(cache_diff_k)
    vt_pool = pool_view(cache_diff_v)

    xp = x_prompt
    xs = jnp.pad(x_sample, ((0, 0), (0, ts_pad - ts), (0, 0)))
    prompt_tile = _pick_tile(tp, ROW_TILE)
    outs_p = [[] for _ in range(9)]
    outs_s = [[] for _ in range(7)]
    for l in range(depth):
        lam_init = 0.8 - 0.6 * math.exp(-0.3 * l)
        last = l == depth - 1
        p = _layer_weights(l, w)
        mk, mv = _mem_project(mem_prompt.reshape(bp * mt, D_MODEL), p['norm_mem_kv'], p['mem_wk'], p['mem_wv'])
        mk = mk.reshape(bp, mt, MEM_INNER)
        mv = mv.reshape(bp, mt, MEM_INNER)
        attn_p = functools.partial(_dattn_prompt, lam_par=p['lam_par'], subln=p['subln'], lam_init=lam_init)
        res = _trunk_layer(
            xp, tp, mk, mv,
            jnp.zeros((bp, GDN_CONV - 1, 3 * W_A), F32), jnp.zeros((bp, H_A, HEAD_DIM, HEAD_DIM), F32),
            jnp.zeros((bp, C_COLS), F32), jnp.zeros((bp, H_C, HEAD_DIM, HEAD_DIM), F32),
            jnp.zeros((bp, FFN_CONV - 1, D_FF), F32), attn_p, p, w_in, l, 1, 1, 1, prompt_tile, last, True)
        xp = res[0]
        for acc, val in zip(outs_p, res[1:] + (mk, mv)):
            acc.append(val)
        attn_s = functools.partial(_dattn_paged, kt_pool=kt_pool, vt_pool=vt_pool, layer=l, page_table=page_table,
                                   lam_par=p['lam_par'], subln=p['subln'], lam_init=lam_init, t_valid=ts)
        res = _trunk_layer(
            xs, ts, cache_mem_k[l].reshape(bs, mt, MEM_INNER), cache_mem_v[l].reshape(bs, mt, MEM_INNER),
            state_gdn_conv[l], state_gdn[l], state_rwkv_shift[l], state_rwkv[l], state_ffn_conv[l], attn_s, p, w_in, l,
            bs, min(GROUP_B, bs), bs, ts_pad, last, False)
        xs = res[0]
        for acc, val in zip(outs_s, res[1:]):
            acc.append(val)

    st = lambda vals, shape: jnp.stack(vals).reshape((depth,) + shape)
    p_k, p_v, p_gc, p_gs, p_rp, p_rs, p_fc, p_mk, p_mv = outs_p
    s_k, s_v, s_gc, s_gs, s_rp, s_rs, s_fc = outs_s
    kv_p = lambda vals: jnp.transpose(jnp.stack(vals).reshape(depth, bp, H_B, 2 * DB, tp), (0, 1, 4, 2, 3))
    kv_s = (bs, ts, H_B, 2 * DB)
    mem_shape = (bp, mt, MEM_HEADS, MEM_HD)
    return (xp, xs[:, 0:ts],
            kv_p(p_k), kv_p(p_v), jnp.stack(p_gc), jnp.stack(p_gs), jnp.stack(p_rp), jnp.stack(p_rs),
            jnp.stack(p_fc), st(p_mk, mem_shape), st(p_mv, mem_shape),
            st(s_k, kv_s), st(s_v, kv_s), jnp.stack(s_gc), jnp.stack(s_gs), jnp.stack(s_rp), jnp.stack(s_rs),
            jnp.stack(s_fc))
```

```python
import functools
import math

import numpy as np
import jax
import jax.numpy as jnp
from jax import lax
from jax.experimental import pallas as pl
from jax.experimental.pallas import tpu as pltpu

F32 = jnp.float32
BF16 = jnp.bfloat16

D_MODEL = 1024
HEAD_DIM = 64
W_A = 384
W_B = 256
W_C = 384
H_A = W_A // HEAD_DIM
H_B = W_B // HEAD_DIM
H_C = W_C // HEAD_DIM
DB = HEAD_DIM // 2
GDN_CONV = 4
DECAY_LORA = 64
AAA_LORA = 64
GATE_LORA = 128
RWKV_GN_EPS = 64e-5
MEM_HEADS = 4
MEM_HD = 128
MEM_INNER = MEM_HEADS * MEM_HD
D_FF = 2816
FFN_CONV = 3
NORM_EPS = 1e-6
A_COLS = 4 * W_A + 2 * H_A
B_COLS = 3 * W_B
C_COLS = 3 * W_C + DECAY_LORA + AAA_LORA + GATE_LORA
GATE_PAD = 128
A_PAD = 4 * W_A + GATE_PAD

SUBLANES = 8
LANES = 128
VMEM_LIMIT = 56 * 1024 * 1024
GROUP_B = 8
ROW_TILE = 512
ATT_TILE = 1024
PAGES_PER_STEP = 32

_NN = (((1,), (0,)), ((), ()))
_NT = (((1,), (1,)), ((), ()))
_BNN = (((2,), (1,)), ((0,), (0,)))
_BNT = (((2,), (2,)), ((0,), (0,)))
_BTN = (((1,), (1,)), ((0,), (0,)))


def _dot(a, b, dn):
    return lax.dot_general(a.astype(BF16), b.astype(BF16), dn, preferred_element_type=F32)


def _dot_f32(a, b, dn):
    return lax.dot_general(a, b, dn, precision=lax.Precision.HIGHEST, preferred_element_type=F32)


def _sigmoid(x):
    return 0.5 * jnp.tanh(0.5 * x) + 0.5


def _softplus(x):
    return jnp.maximum(x, 0.0) + jnp.log(1.0 + jnp.exp(-jnp.abs(x)))


def _rms(x, g, eps=NORM_EPS):
    return x * lax.rsqrt(jnp.mean(x * x, axis=-1, keepdims=True) + eps) * g


def _pick_tile(n, pref):
    if n <= pref:
        return n
    t = pref - pref % SUBLANES
    while n % t:
        t -= SUBLANES
    return t


def _params(*sem):
    return pltpu.CompilerParams(dimension_semantics=sem, vmem_limit_bytes=VMEM_LIMIT)


def _const_spec(shape):
    nd = len(shape)
    return pl.BlockSpec(shape, lambda *_: (0,) * nd, pipeline_mode=pl.Buffered(1))


def _proj_in_kernel(x_ref, g_ref, w32_ref, *refs, feature_major):
    outs, (wa_ref, wb_ref, wc_ref) = refs[:-3], refs[-3:]

    @pl.when((pl.program_id(0) == 0) & (pl.program_id(1) == 0))
    def _():
        wa_ref[:, 0:4 * W_A] = w32_ref[0, :, 0:4 * W_A].astype(BF16)
        gate_lane = lax.broadcasted_iota(jnp.int32, (1, GATE_PAD), 1) < 2 * H_A
        wa_ref[:, 4 * W_A:A_PAD] = jnp.where(gate_lane, w32_ref[0, :, 4 * W_A:A_PAD], 0.0).astype(BF16)
        wb_ref[...] = w32_ref[0, :, A_COLS:A_COLS + B_COLS].astype(BF16)
        wc_ref[...] = w32_ref[0, :, A_COLS + B_COLS:A_COLS + B_COLS + C_COLS].astype(BF16)

    bb, tt, _ = x_ref.shape
    m = bb * tt
    h = _rms(x_ref[...].reshape(m, D_MODEL), g_ref[...]).astype(BF16)
    cols = lambda w_ref, lo, n: jnp.dot(h, w_ref[:, lo:lo + n], preferred_element_type=F32)
    if feature_major:
        oa_ref, oqt_ref, ok_ref, okt_ref, ovt_ref, ovtb_ref, oc_ref = outs
        qkv = cols(wb_ref, 0, B_COLS)
        k = qkv[:, W_B:2 * W_B]
        vt = qkv[:, 2 * W_B:3 * W_B].T
        oqt_ref[0] = (qkv[:, 0:W_B] * (DB ** -0.5)).T.astype(BF16)
        ok_ref[0] = k.astype(BF16)
        okt_ref[0] = k.T
        ovt_ref[0] = vt
        ovtb_ref[0] = vt.astype(BF16)
    else:
        oa_ref, oq_ref, ok_ref, ov_ref, oc_ref = outs
        oq_ref[...] = cols(wb_ref, 0, W_B).reshape(bb, tt, W_B)
        ok_ref[...] = cols(wb_ref, W_B, W_B).reshape(bb, tt, W_B)
        ov_ref[...] = cols(wb_ref, 2 * W_B, W_B).reshape(bb, tt, W_B)
    oa_ref[...] = cols(wa_ref, 0, A_PAD).reshape(bb, tt, A_PAD)
    oc_ref[...] = cols(wc_ref, 0, C_COLS).reshape(bb, tt, C_COLS)


def _proj_in(x, g, w_in, layer, bb, tt, feature_major):
    b, t, _ = x.shape
    tok = lambda w: pl.BlockSpec((bb, tt, w), lambda bi, ti: (bi, ti, 0))
    feat = pl.BlockSpec((1, W_B, tt), lambda bi, ti: (bi, 0, ti))
    sds = jax.ShapeDtypeStruct
    if feature_major:
        assert bb == 1
        out_specs = [tok(A_PAD), feat, tok(W_B), feat, feat, feat, tok(C_COLS)]
        out_shape = [sds((b, t, A_PAD), F32), sds((b, W_B, t), BF16), sds((b, t, W_B), BF16),
                     sds((b, W_B, t), F32), sds((b, W_B, t), F32), sds((b, W_B, t), BF16), sds((b, t, C_COLS), F32)]
    else:
        out_specs = [tok(A_PAD), tok(W_B), tok(W_B), tok(W_B), tok(C_COLS)]
        out_shape = [sds((b, t, w), F32) for w in (A_PAD, W_B, W_B, W_B, C_COLS)]
    w_spec = pl.BlockSpec((1,) + w_in.shape[1:], lambda bi, ti: (layer, 0, 0), pipeline_mode=pl.Buffered(1))
    return pl.pallas_call(
        functools.partial(_proj_in_kernel, feature_major=feature_major),
        grid=(b // bb, t // tt),
        in_specs=[tok(D_MODEL), _const_spec(g.shape), w_spec],
        out_specs=out_specs,
        out_shape=out_shape,
        scratch_shapes=[pltpu.VMEM((D_MODEL, w), BF16) for w in (A_PAD, B_COLS, C_COLS)],
        compiler_params=_params("arbitrary", "arbitrary"),
        name="proj_in",
    )(x, g, w_in)


CHUNK = 64
PAIR = 2 * HEAD_DIM


def _chunk_cumsum(x, c):
    r = lax.broadcasted_iota(jnp.int32, (c, c), 0)
    q = lax.broadcasted_iota(jnp.int32, (c, c), 1)
    tri = jnp.broadcast_to(jnp.where(q <= r, 1.0, 0.0).astype(F32), (x.shape[0], c, c))
    return _dot_f32(tri, x, _BNN)


def _select_dot(x, sel, terms):
    acc = None
    rest = x
    for _ in range(terms):
        piece = rest.astype(BF16)
        part = jnp.dot(piece, sel, preferred_element_type=F32)
        acc = part if acc is None else acc + part
        rest = rest - piece.astype(F32)
    return acc


def _head_sum_matrix(width):
    r = lax.broadcasted_iota(jnp.int32, (width, width), 0) // HEAD_DIM
    q = lax.broadcasted_iota(jnp.int32, (width, width), 1) // HEAD_DIM
    return jnp.where(r == q, 1.0, 0.0).astype(BF16)


def _head_spread_matrix(first_lane, width):
    r = lax.broadcasted_iota(jnp.int32, (LANES, width), 0)
    q = lax.broadcasted_iota(jnp.int32, (LANES, width), 1) // HEAD_DIM
    return jnp.where(r == q + first_lane, 1.0, 0.0).astype(BF16)


def _pair_rows(z):
    first = (lax.broadcasted_iota(jnp.int32, (1, 1, z.shape[-1]), 2) % PAIR) < HEAD_DIM
    return jnp.concatenate([jnp.where(first, z, 0.0), jnp.where(first, 0.0, z)], axis=1)


def _pair_solve(l, x, c, live_rows):
    width = x.shape[-1]
    steps = max(1, math.ceil(math.log2(live_rows)))
    p = l
    for step in range(steps):
        if step == steps - 1:
            return x + _dot(p, _pair_rows(x), _BNN)
        both = _dot(p, _pair_rows(jnp.concatenate([x, p], axis=-1)), _BNN)
        x = x + both[:, :, 0:width]
        p = both[:, :, width:width + 2 * c]


def _load_pair_state(sp_ref, s0_ref, heads):
    sp_ref[...] = jnp.zeros(sp_ref.shape, F32)
    for p in range(heads // 2):
        sp_ref[:, p, 0:HEAD_DIM, 0:HEAD_DIM] = s0_ref[:, 2 * p]
        sp_ref[:, p, HEAD_DIM:PAIR, HEAD_DIM:PAIR] = s0_ref[:, 2 * p + 1]


def _store_pair_state(s_ref, sp_ref, heads):
    for p in range(heads // 2):
        s_ref[:, 2 * p] = sp_ref[:, p, 0:HEAD_DIM, 0:HEAD_DIM]
        s_ref[:, 2 * p + 1] = sp_ref[:, p, HEAD_DIM:PAIR, HEAD_DIM:PAIR]


def _pair_masks(c):
    i = lax.broadcasted_iota(jnp.int32, (c, 2 * c), 0)
    j = lax.broadcasted_iota(jnp.int32, (c, 2 * c), 1) % c
    r = lax.broadcasted_iota(jnp.int32, (PAIR, PAIR), 0) // HEAD_DIM
    q = lax.broadcasted_iota(jnp.int32, (PAIR, PAIR), 1) // HEAD_DIM
    return j < i, j <= i, r == q


def _gdn_kernel(a_ref, buf_ref, s0_ref, cw_ref, gp_ref, ng_ref, o_ref, s_ref, xp_ref, sp_ref, *, t_valid):
    ci = pl.program_id(1)
    g_b, c, _ = a_ref.shape
    m = g_b * c
    hist = SUBLANES - (GDN_CONV - 1)

    @pl.when(ci == 0)
    def _():
        xp_ref[:, hist:SUBLANES, :] = buf_ref[...]
        _load_pair_state(sp_ref, s0_ref, H_A)

    xp_ref[:, SUBLANES:SUBLANES + c, :] = a_ref[:, :, 0:3 * W_A]
    y = xp_ref[:, hist:hist + c, :] * cw_ref[0:1, :]
    for j in range(1, GDN_CONV):
        y = y + xp_ref[:, hist + j:hist + j + c, :] * cw_ref[j:j + 1, :]
    xp_ref[:, hist:SUBLANES, :] = xp_ref[:, c + hist:c + SUBLANES, :]
    qkv = (y * _sigmoid(y)).reshape(m, 3 * W_A)

    head_sum = _head_sum_matrix(W_A)
    q = qkv[:, 0:W_A]
    k = qkv[:, W_A:2 * W_A]
    v = qkv[:, 2 * W_A:3 * W_A]
    q = q * lax.rsqrt(_select_dot(q * q, head_sum, 2) + 1e-6) * (HEAD_DIM ** -0.5)
    k = k * lax.rsqrt(_select_dot(k * k, head_sum, 2) + 1e-6)

    gates = a_ref[:, :, 4 * W_A:A_PAD]
    tpos = ci * c + lax.broadcasted_iota(jnp.int32, (1, c, 1), 1)
    valid = tpos < t_valid
    beta_all = jnp.where(valid, _sigmoid(gates), 0.0)
    g_all = jnp.where(valid, -jnp.exp(gp_ref[0:1, :]) * _softplus(gates + gp_ref[1:2, :]), 0.0)
    gam_all = _chunk_cumsum(g_all, c)
    gam_t = jnp.swapaxes(gam_all, 1, 2)
    beta = _select_dot(beta_all.reshape(m, LANES), _head_spread_matrix(0, W_A), 3)
    gam = _select_dot(gam_all.reshape(m, LANES), _head_spread_matrix(H_A, W_A), 3)
    eg = jnp.exp(gam)
    kb = k * beta
    to3 = lambda z: z.reshape(g_b, c, W_A)
    gam3 = to3(gam)
    glast = gam3[:, c - 1:c, :]
    q3, k3, kb3, qe3, vb3, kbe3 = (to3(z) for z in (q, k, kb, q * eg, v * beta, kb * eg))
    kd3 = k3 * jnp.exp(glast - gam3)
    gl = jnp.exp(glast)
    strict, incl, diag_blocks = _pair_masks(c)

    n_pairs = H_A // 2
    stack = lambda z: jnp.concatenate([z[:, :, p * PAIR:(p + 1) * PAIR] for p in range(n_pairs)], axis=0)
    grow = jnp.concatenate(
        [jnp.concatenate([gam_t[:, H_A + 2 * p:H_A + 2 * p + 1, :], gam_t[:, H_A + 2 * p + 1:H_A + 2 * p + 2, :]],
                         axis=-1) for p in range(n_pairs)], axis=0)
    diff = stack(gam3) - grow
    k_rows = _pair_rows(stack(k3))
    kk = _dot(jnp.concatenate([stack(kb3), stack(q3)], axis=1), k_rows, _BNT)
    a_mat = kk[:, 0:c, :] * jnp.exp(jnp.where(strict, diff, -jnp.inf))
    qk = kk[:, c:2 * c, :] * jnp.exp(jnp.where(incl, diff, -jnp.inf))
    x = jnp.concatenate([stack(vb3), stack(kbe3)], axis=-1)
    x = _pair_solve(-a_mat, x, c, min(c, t_valid))
    s = jnp.concatenate([sp_ref[:, p] for p in range(n_pairs)], axis=0)
    ws = _dot(jnp.concatenate([x[:, :, PAIR:2 * PAIR], stack(qe3)], axis=1), s, _BNN)
    v_new = x[:, :, 0:PAIR] - ws[:, 0:c, :]
    o = ws[:, c:2 * c, :] + _dot(qk, _pair_rows(v_new), _BNN)
    s_new = jnp.where(diag_blocks, s * stack(gl) + _dot(stack(kd3), v_new, _BTN), 0.0)
    for p in range(n_pairs):
        sp_ref[:, p] = s_new[p * g_b:(p + 1) * g_b]
    outs = [o[p * g_b:(p + 1) * g_b] for p in range(n_pairs)]
    o = jnp.concatenate(outs, axis=-1).reshape(m, W_A)
    o = o * lax.rsqrt(_select_dot(o * o, head_sum, 2) * (1.0 / HEAD_DIM) + NORM_EPS) * ng_ref[...]
    z = a_ref[:, :, 3 * W_A:4 * W_A].reshape(m, W_A)
    o_ref[...] = (o * (z * _sigmoid(z))).reshape(g_b, c, W_A)

    @pl.when(ci == pl.num_programs(1) - 1)
    def _():
        _store_pair_state(s_ref, sp_ref, H_A)


def _gdn(a_cols, conv_buf, s0, conv_w, gate_par, norm_g, t_valid):
    b, t, _ = a_cols.shape
    g_b = min(GROUP_B, b)
    c = CHUNK
    kern = functools.partial(_gdn_kernel, t_valid=t_valid)
    state = pl.BlockSpec((g_b, H_A, HEAD_DIM, HEAD_DIM), lambda bi, ci: (bi, 0, 0, 0))
    return pl.pallas_call(
        kern,
        grid=(b // g_b, t // c),
        in_specs=[pl.BlockSpec((g_b, c, A_PAD), lambda bi, ci: (bi, ci, 0)),
                  pl.BlockSpec((g_b, GDN_CONV - 1, 3 * W_A), lambda bi, ci: (bi, 0, 0)),
                  state, _const_spec(conv_w.shape), _const_spec(gate_par.shape), _const_spec(norm_g.shape)],
        out_specs=[pl.BlockSpec((g_b, c, W_A), lambda bi, ci: (bi, ci, 0)), state],
        out_shape=[jax.ShapeDtypeStruct((b, t, W_A), F32),
                   jax.ShapeDtypeStruct((b, H_A, HEAD_DIM, HEAD_DIM), F32)],
        scratch_shapes=[pltpu.VMEM((g_b, c + SUBLANES, 3 * W_A), F32),
                        pltpu.VMEM((g_b, H_A // 2, PAIR, PAIR), F32)],
        compiler_params=_params("arbitrary", "arbitrary"),
        name="gdn_mixer",
    )(a_cols, conv_buf, s0, conv_w, gate_par, norm_g)


def _rwkv_kernel(c_ref, prev_ref, s0_ref, mu_ref, vec_ref, w2_ref, a2_ref, g2_ref, o_ref, s_ref, xp_ref, sp_ref,
                 *, t_valid):
    ci = pl.program_id(1)
    g_b, c, _ = c_ref.shape
    m = g_b * c

    @pl.when(ci == 0)
    def _():
        xp_ref[:, SUBLANES - 1:SUBLANES, :] = prev_ref[...]
        _load_pair_state(sp_ref, s0_ref, H_C)

    x = c_ref[...]
    xp_ref[:, SUBLANES:SUBLANES + c, :] = x
    shifted = xp_ref[:, SUBLANES - 1:SUBLANES - 1 + c, :]
    xp_ref[:, SUBLANES - 1:SUBLANES, :] = xp_ref[:, c + SUBLANES - 1:c + SUBLANES, :]
    xs = (x + (shifted - x) * mu_ref[...]).reshape(m, C_COLS)

    w0, a0, k_k, k_a, r_k, lnx_w, lnx_b = (vec_ref[i:i + 1, :] for i in range(7))
    r = xs[:, 0:W_C]
    k = xs[:, W_C:2 * W_C]
    v = xs[:, 2 * W_C:3 * W_C]
    lora = xs[:, 3 * W_C:3 * W_C + DECAY_LORA + AAA_LORA]
    gd = xs[:, 3 * W_C + DECAY_LORA + AAA_LORA:C_COLS]
    lane = lax.broadcasted_iota(jnp.int32, (1, DECAY_LORA + AAA_LORA), 1)
    lora = jnp.where(lane < DECAY_LORA, jnp.tanh(lora), lora)
    w = -_softplus(-(w0 + _dot(lora, w2_ref[...], _NN))) - 0.5
    a = _sigmoid(a0 + _dot(lora, a2_ref[...], _NN))
    gate = _dot(_sigmoid(gd), g2_ref[...], _NN)
    head_sum = _head_sum_matrix(W_C)
    kk = k * k_k
    kk = kk * lax.rsqrt(_select_dot(kk * kk, head_sum, 2) + 1e-12)
    k = k * (1.0 + (a - 1.0) * k_a)

    tpos = ci * c + lax.broadcasted_iota(jnp.int32, (1, c, 1), 1)
    valid = tpos < t_valid
    to3 = lambda z: z.reshape(g_b, c, W_C)
    lw = jnp.where(valid, to3(-jnp.exp(w)), 0.0)
    cl = _chunk_cumsum(lw, c)
    p_in = jnp.exp(cl)
    p_inv = jnp.exp(-cl)
    k3 = jnp.where(valid, to3(k), 0.0)
    kk3 = jnp.where(valid, to3(kk), 0.0)
    v3 = to3(v)
    at = -kk3 * jnp.exp(cl - lw)
    bt = kk3 * to3(a) * p_inv
    kt = k3 * p_inv
    rt = to3(r) * p_in

    strict, incl, diag_blocks = _pair_masks(c)
    tri4 = jnp.concatenate([jnp.concatenate([strict, strict], axis=1),
                            jnp.concatenate([incl, incl], axis=1)], axis=0)

    n_pairs = H_C // 2
    stack = lambda z: jnp.concatenate([z[:, :, p * PAIR:(p + 1) * PAIR] for p in range(n_pairs)], axis=0)
    atp, btp, ktp, rtp, vp = stack(at), stack(bt), stack(kt), stack(rt), stack(v3)
    lhs = jnp.concatenate([atp, rtp], axis=1)
    rhs = jnp.concatenate([_pair_rows(btp), _pair_rows(ktp)], axis=1)
    gm = jnp.where(tri4, _dot(lhs, rhs, _BNT), 0.0)
    top = gm[:, 0:c, :]
    bot = gm[:, c:2 * c, :]
    v_rows = _pair_rows(vp)
    s = jnp.concatenate([sp_ref[:, p] for p in range(n_pairs)], axis=0)
    from_state = _dot(lhs, s, _BNT)
    rhs_u = from_state[:, 0:c, :] + _dot(top[:, :, 2 * c:4 * c], v_rows, _BNN)
    u = _pair_solve(top[:, :, 0:2 * c], rhs_u, c, min(c, t_valid))
    o = from_state[:, c:2 * c, :] + _dot(bot, jnp.concatenate([_pair_rows(u), v_rows], axis=1), _BNN)
    s_new = s + _dot(jnp.concatenate([u, vp], axis=1), jnp.concatenate([btp, ktp], axis=1), _BTN)
    s_new = jnp.where(diag_blocks, s_new, 0.0) * stack(p_in[:, c - 1:c, :])
    for p in range(n_pairs):
        sp_ref[:, p] = s_new[p * g_b:(p + 1) * g_b]
    outs = [o[p * g_b:(p + 1) * g_b] for p in range(n_pairs)]
    o = jnp.concatenate(outs, axis=-1).reshape(m, W_C)
    inv_n = 1.0 / HEAD_DIM
    mean = _select_dot(o, head_sum, 2) * inv_n
    var = _select_dot(jnp.square(o - mean), head_sum, 2) * inv_n
    o = (o - mean) * lax.rsqrt(var + RWKV_GN_EPS) * lnx_w + lnx_b
    k2 = k3.reshape(m, W_C)
    o = o + _select_dot(r * k2 * r_k, head_sum, 2) * v
    o_ref[...] = (o * gate).reshape(g_b, c, W_C)

    @pl.when(ci == pl.num_programs(1) - 1)
    def _():
        _store_pair_state(s_ref, sp_ref, H_C)


def _rwkv(c_cols, prev, s0, mu, vecs, w2p, a2p, g2, t_valid):
    b, t, _ = c_cols.shape
    g_b = min(GROUP_B, b)
    c = CHUNK
    kern = functools.partial(_rwkv_kernel, t_valid=t_valid)
    state = pl.BlockSpec((g_b, H_C, HEAD_DIM, HEAD_DIM), lambda bi, ci: (bi, 0, 0, 0))
    return pl.pallas_call(
        kern,
        grid=(b // g_b, t // c),
        in_specs=[pl.BlockSpec((g_b, c, C_COLS), lambda bi, ci: (bi, ci, 0)),
                  pl.BlockSpec((g_b, 1, C_COLS), lambda bi, ci: (bi, 0, 0)),
                  state, _const_spec(mu.shape), _const_spec(vecs.shape), _const_spec(w2p.shape),
                  _const_spec(a2p.shape), _const_spec(g2.shape)],
        out_specs=[pl.BlockSpec((g_b, c, W_C), lambda bi, ci: (bi, ci, 0)), state],
        out_shape=[jax.ShapeDtypeStruct((b, t, W_C), F32),
                   jax.ShapeDtypeStruct((b, H_C, HEAD_DIM, HEAD_DIM), F32)],
        scratch_shapes=[pltpu.VMEM((g_b, c + SUBLANES, C_COLS), F32),
                        pltpu.VMEM((g_b, H_C // 2, PAIR, PAIR), F32)],
        compiler_params=_params("arbitrary", "arbitrary"),
        name="rwkv_mixer",
    )(c_cols, prev, s0, mu, vecs, w2p, a2p, g2)


def _diff_lambda(lp_ref, lam_init):
    l1 = jnp.exp(jnp.sum(lp_ref[0:1, :] * lp_ref[1:2, :], axis=-1, keepdims=True))
    l2 = jnp.exp(jnp.sum(lp_ref[2:3, :] * lp_ref[3:4, :], axis=-1, keepdims=True))
    return l1 - l2 + lam_init


def _dattn_kernel(qi_ref, ki_ref, lp_ref, qt_ref, k_ref, vt_ref, sub_ref, o_ref, qm_ref, m_ref, l_ref, acc_ref,
                  *, lam_init):
    step = pl.program_id(1)
    qi = qi_ref[step]
    ki = ki_ref[step]
    tq = qt_ref.shape[2]
    tk = k_ref.shape[1]

    @pl.when(ki == 0)
    def _():
        m_ref[...] = jnp.full(m_ref.shape, -jnp.inf, F32)
        l_ref[...] = jnp.zeros(l_ref.shape, F32)
        acc_ref[...] = jnp.zeros(acc_ref.shape, F32)
        fmap = lax.broadcasted_iota(jnp.int32, (W_B, 1), 0) // DB
        qt = qt_ref[0]
        for hm in range(2 * H_B):
            qm_ref[hm] = jnp.where(fmap == hm, qt, jnp.zeros_like(qt))

    def scores(hm, diagonal):
        st = jnp.dot(k_ref[0], qm_ref[hm], preferred_element_type=F32)
        if diagonal:
            kpos = lax.broadcasted_iota(jnp.int32, (tk, tq), 0)
            qpos = lax.broadcasted_iota(jnp.int32, (tk, tq), 1)
            st = jnp.where(kpos <= qpos, st, -jnp.inf)
        m = m_ref[hm]
        return st, m, jnp.maximum(m, jnp.max(st, axis=0, keepdims=True))

    def weights(hm, st, m, m_new):
        alpha = jnp.exp(m - m_new)
        p = jnp.exp(st - m_new)
        l_ref[hm] = alpha * l_ref[hm] + jnp.sum(p, axis=0, keepdims=True)
        m_ref[hm] = m_new
        return alpha, p.astype(BF16)

    def accumulate(hm, alpha, p):
        h = hm // 2
        pv = jnp.dot(vt_ref[0, h * HEAD_DIM:(h + 1) * HEAD_DIM, :], p, preferred_element_type=F32)
        acc_ref[hm] = alpha * acc_ref[hm] + pv

    def tile(diagonal):
        n = 2 * H_B
        sc = scores(0, diagonal)
        for hm in range(n):
            wt = weights(hm, *sc)
            if hm + 1 < n:
                sc = scores(hm + 1, diagonal)
            accumulate(hm, *wt)

    pl.when(ki < qi)(lambda: tile(False))

    @pl.when(ki == qi)
    def _():
        tile(True)
        lam = _diff_lambda(lp_ref, lam_init)
        heads = []
        for h in range(H_B):
            oh = acc_ref[2 * h] / l_ref[2 * h] - lam * (acc_ref[2 * h + 1] / l_ref[2 * h + 1])
            heads.append(oh * lax.rsqrt(jnp.mean(oh * oh, axis=0, keepdims=True) + NORM_EPS))
        o = jnp.concatenate(heads, axis=0).T
        o_ref[0] = o * sub_ref[...] * (1.0 - lam_init)


def _dattn_prompt(qt, k, vt, lam_par, subln, lam_init):
    b, _, t = qt.shape
    tile = _pick_tile(t, ATT_TILE)
    n = t // tile
    pairs = [(qi, ki) for qi in range(n) for ki in range(qi + 1)]
    qi_tab = jnp.asarray(np.array([p[0] for p in pairs], np.int32))
    ki_tab = jnp.asarray(np.array([p[1] for p in pairs], np.int32))
    const = lambda shape: pl.BlockSpec(shape, lambda bi, s, qi, ki: (0,) * len(shape))
    grid_spec = pltpu.PrefetchScalarGridSpec(
        num_scalar_prefetch=2,
        grid=(b, len(pairs)),
        in_specs=[const(lam_par.shape),
                  pl.BlockSpec((1, W_B, tile), lambda bi, s, qi, ki: (bi, 0, qi[s])),
                  pl.BlockSpec((1, tile, W_B), lambda bi, s, qi, ki: (bi, ki[s], 0)),
                  pl.BlockSpec((1, W_B, tile), lambda bi, s, qi, ki: (bi, 0, ki[s])),
                  const(subln.shape)],
        out_specs=pl.BlockSpec((1, tile, W_B), lambda bi, s, qi, ki: (bi, qi[s], 0)),
        scratch_shapes=[pltpu.VMEM((2 * H_B, W_B, tile), BF16), pltpu.VMEM((2 * H_B, 1, tile), F32),
                        pltpu.VMEM((2 * H_B, 1, tile), F32), pltpu.VMEM((2 * H_B, HEAD_DIM, tile), F32)],
    )
    return pl.pallas_call(
        functools.partial(_dattn_kernel, lam_init=lam_init),
        grid_spec=grid_spec,
        out_shape=jax.ShapeDtypeStruct((b, t, W_B), F32),
        compiler_params=_params("arbitrary", "arbitrary"),
        name="diff_attn_prompt",
    )(qi_tab, ki_tab, lam_par, qt, k, vt, subln)


def _softmax_rows(s, m_ref, l_ref):
    m_old = m_ref[...]
    m_new = jnp.maximum(m_old, jnp.max(s, axis=-1, keepdims=True))
    alpha = jnp.exp(m_old - m_new)
    p = jnp.exp(s - m_new)
    l_ref[...] = alpha * l_ref[...] + jnp.sum(p, axis=-1, keepdims=True)
    m_ref[...] = m_new
    return p, alpha


def _paged_kernel(pt_ref, lp_ref, q_ref, kn_ref, vn_ref, sub_ref, *refs, pages, lam_init, t_valid):
    k_pages = refs[0:pages]
    v_pages = refs[pages:2 * pages]
    o_ref, m_ref, l_ref, acc_ref = refs[2 * pages:]
    j = pl.program_id(1)
    tp = q_ref.shape[1]
    maps = lax.broadcasted_iota(jnp.int32, (1, W_B), 1) // DB

    @pl.when(j == 0)
    def _():
        m_ref[...] = jnp.full(m_ref.shape, -jnp.inf, F32)
        l_ref[...] = jnp.zeros(l_ref.shape, F32)
        acc_ref[...] = jnp.zeros(acc_ref.shape, F32)

    q = q_ref[0]
    qm = jnp.concatenate([jnp.where(maps == hm, q, 0.0) for hm in range(2 * H_B)], axis=0).astype(BF16)
    scale = DB ** -0.5

    kt = jnp.concatenate([kp[0, 0].astype(BF16) for kp in k_pages], axis=1)
    vt = jnp.concatenate([vp[0, 0].astype(BF16) for vp in v_pages], axis=1)
    s = jnp.dot(qm, kt, preferred_element_type=F32) * scale
    p, alpha = _softmax_rows(s, m_ref, l_ref)
    acc_ref[...] = alpha * acc_ref[...] + _dot(p, vt, _NT)

    @pl.when(j == pl.num_programs(1) - 1)
    def _():
        rows = 2 * H_B * tp
        tq_pos = lax.broadcasted_iota(jnp.int32, (rows, tp), 0) % tp
        tk_pos = lax.broadcasted_iota(jnp.int32, (rows, tp), 1)
        s_new = _dot(qm, kn_ref[0], _NT) * scale
        s_new = jnp.where((tk_pos <= tq_pos) & (tk_pos < t_valid), s_new, -jnp.inf)
        p_new, alpha_new = _softmax_rows(s_new, m_ref, l_ref)
        res = (alpha_new * acc_ref[...] + _dot(p_new, vn_ref[0], _NN)) / l_ref[...]
        lam = _diff_lambda(lp_ref, lam_init)
        heads = lax.broadcasted_iota(jnp.int32, (1, W_B), 1) // HEAD_DIM
        o = jnp.zeros((tp, W_B), F32)
        for h in range(H_B):
            oh = res[2 * h * tp:(2 * h + 1) * tp] - lam * res[(2 * h + 1) * tp:(2 * h + 2) * tp]
            o = jnp.where(heads == h, oh, o)
        r = lax.broadcasted_iota(jnp.int32, (W_B, W_B), 0) // HEAD_DIM
        c = lax.broadcasted_iota(jnp.int32, (W_B, W_B), 1) // HEAD_DIM
        ms = _dot_f32(o * o, jnp.where(r == c, 1.0 / HEAD_DIM, 0.0).astype(F32), _NN)
        o_ref[0] = o * lax.rsqrt(ms + NORM_EPS) * sub_ref[...] * (1.0 - lam_init)


def _dattn_paged(q, k, v, kt_pool, vt_pool, layer, page_table, lam_par, subln, lam_init, t_valid):
    b, tp, _ = q.shape
    n_pages = page_table.shape[1]
    page = kt_pool.shape[3]
    pages = math.gcd(n_pages, PAGES_PER_STEP)
    rows = 2 * H_B * tp
    kern = functools.partial(_paged_kernel, pages=pages, lam_init=lam_init, t_valid=t_valid)

    def page_spec(i):
        return pl.BlockSpec((1, 1, W_B, page), lambda bi, j, pt: (layer, pt[bi, j * pages + i], 0, 0))

    tok = pl.BlockSpec((1, tp, W_B), lambda bi, j, pt: (bi, 0, 0))
    const = lambda shape: pl.BlockSpec(shape, lambda bi, j, pt: (0,) * len(shape))
    grid_spec = pltpu.PrefetchScalarGridSpec(
        num_scalar_prefetch=1,
        grid=(b, n_pages // pages),
        in_specs=[const(lam_par.shape), tok, tok, tok, const(subln.shape)]
        + [page_spec(i) for i in range(pages)] * 2,
        out_specs=tok,
        scratch_shapes=[pltpu.VMEM((rows, 1), F32), pltpu.VMEM((rows, 1), F32), pltpu.VMEM((rows, W_B), F32)],
    )
    return pl.pallas_call(
        kern,
        grid_spec=grid_spec,
        out_shape=jax.ShapeDtypeStruct((b, tp, W_B), F32),
        compiler_params=_params("arbitrary", "arbitrary"),
        name="diff_attn_paged",
    )(page_table, lam_par, q, k, v, subln, *([kt_pool] * pages), *([vt_pool] * pages))


def _mem_project_kernel(m_ref, g_ref, wk_ref, wv_ref, k_ref, v_ref):
    h = _rms(m_ref[...], g_ref[...]).astype(BF16)
    k_ref[...] = jnp.dot(h, wk_ref[...], preferred_element_type=F32)
    v_ref[...] = jnp.dot(h, wv_ref[...], preferred_element_type=F32)


def _mem_project(mem2d, g, wk, wv):
    n = mem2d.shape[0]
    tm = _pick_tile(n, ROW_TILE)
    return pl.pallas_call(
        _mem_project_kernel,
        grid=(n // tm,),
        in_specs=[pl.BlockSpec((tm, D_MODEL), lambda i: (i, 0)), _const_spec(g.shape), _const_spec(wk.shape),
                  _const_spec(wv.shape)],
        out_specs=[pl.BlockSpec((tm, MEM_INNER), lambda i: (i, 0))] * 2,
        out_shape=[jax.ShapeDtypeStruct((n, MEM_INNER), F32)] * 2,
        compiler_params=_params("arbitrary"),
        name="mem_project",
    )(mem2d, g, wk, wv)


def _mix_mem_kernel(x_ref, oa_ref, ob_ref, oc_ref, wout_ref, g_ref, wq_ref, mk_ref, mv_ref, wo_ref, out_ref):
    bb, tt, _ = x_ref.shape
    m = bb * tt
    x = x_ref[...].reshape(m, D_MODEL)
    mixed = jnp.concatenate([oa_ref[...].reshape(m, W_A), ob_ref[...].reshape(m, W_B), oc_ref[...].reshape(m, W_C)],
                            axis=-1)
    x = x + _dot(mixed, wout_ref[...], _NN)
    q = _dot(_rms(x, g_ref[...]), wq_ref[...], _NN)
    heads = []
    for h in range(MEM_HEADS):
        sl = slice(h * MEM_HD, (h + 1) * MEM_HD)
        qh = q[:, sl].reshape(bb, tt, MEM_HD)
        mk = mk_ref[0, :, :, h, :] if len(mk_ref.shape) == 5 else mk_ref[:, :, sl]
        mv = mv_ref[0, :, :, h, :] if len(mv_ref.shape) == 5 else mv_ref[:, :, sl]
        s = _dot(qh, mk, _BNT) * (MEM_HD ** -0.5)
        p = jnp.exp(s - jnp.max(s, axis=-1, keepdims=True))
        p = p / jnp.sum(p, axis=-1, keepdims=True)
        heads.append(_dot(p, mv, _BNN).reshape(m, MEM_HD))
    o = jnp.concatenate(heads, axis=-1)
    out_ref[...] = (x + _dot(o, wo_ref[...], _NN)).reshape(bb, tt, D_MODEL)


def _mix_mem(x, o_a, o_b, o_c, w_out, g, wq, mem_k, mem_v, wo, bb, tt, layer=None):
    b, t, _ = x.shape
    blk = lambda w: pl.BlockSpec((bb, tt, w), lambda bi, ti: (bi, ti, 0))
    if layer is None:
        mem = pl.BlockSpec((bb,) + mem_k.shape[1:], lambda bi, ti: (bi, 0, 0))
    else:
        mem = pl.BlockSpec((1, bb) + mem_k.shape[2:], lambda bi, ti: (layer, bi, 0, 0, 0))
    return pl.pallas_call(
        _mix_mem_kernel,
        grid=(b // bb, t // tt),
        in_specs=[blk(D_MODEL), blk(W_A), blk(W_B), blk(W_C), _const_spec(w_out.shape), _const_spec(g.shape),
                  _const_spec(wq.shape), mem, mem, _const_spec(wo.shape)],
        out_specs=blk(D_MODEL),
        out_shape=jax.ShapeDtypeStruct((b, t, D_MODEL), F32),
        compiler_params=_params("arbitrary", "arbitrary"),
        name="mix_out_mem_attn",
    )(x, o_a, o_b, o_c, w_out, g, wq, mem_k, mem_v, wo)


def _ffn_kernel(x_ref, buf_ref, g_ref, wg_ref, wv_ref, cw_ref, cb_ref, wo_ref, gf_ref, out_ref, tail_ref,
                carry_ref, gp_ref, *, final_norm):
    ti = pl.program_id(1)
    bb, tt, _ = x_ref.shape
    m = bb * tt
    hist = SUBLANES - (FFN_CONV - 1)

    @pl.when(ti == 0)
    def _():
        carry_ref[:, hist:SUBLANES, :] = buf_ref[...]

    x = x_ref[...].reshape(m, D_MODEL)
    h = _rms(x, g_ref[...]).astype(BF16)
    gate = jnp.dot(h, wg_ref[...], preferred_element_type=F32).reshape(bb, tt, D_FF)
    val = jnp.dot(h, wv_ref[...], preferred_element_type=F32)
    gp_ref[:, hist:SUBLANES, :] = carry_ref[:, hist:SUBLANES, :]
    gp_ref[:, SUBLANES:SUBLANES + tt, :] = gate
    y = cb_ref[...] + gate * cw_ref[FFN_CONV - 1:FFN_CONV, :]
    for j in range(FFN_CONV - 1):
        y = y + gp_ref[:, hist + j:hist + j + tt, :] * cw_ref[j:j + 1, :]
    carry_ref[...] = gate[:, tt - SUBLANES:tt, :]
    tail_ref[...] = gate[:, tt - SUBLANES:tt, :]
    y = y.reshape(m, D_FF)
    act = (y * _sigmoid(y) * val).astype(BF16)
    y = x + jnp.dot(act, wo_ref[...], preferred_element_type=F32)
    if final_norm:
        y = _rms(y, gf_ref[...])
    out_ref[...] = y.reshape(bb, tt, D_MODEL)


def _ffn(x, buf, g, wg, wv, conv_w, conv_b, wo, g_final, bb, tt, final_norm):
    b, t, _ = x.shape
    kern = functools.partial(_ffn_kernel, final_norm=final_norm)
    blk = pl.BlockSpec((bb, tt, D_MODEL), lambda bi, ti: (bi, ti, 0))
    return pl.pallas_call(
        kern,
        grid=(b // bb, t // tt),
        in_specs=[blk, pl.BlockSpec((bb, FFN_CONV - 1, D_FF), lambda bi, ti: (bi, 0, 0)), _const_spec(g.shape),
                  _const_spec(wg.shape), _const_spec(wv.shape), _const_spec(conv_w.shape),
                  _const_spec(conv_b.shape), _const_spec(wo.shape), _const_spec(g_final.shape)],
        out_specs=[blk, pl.BlockSpec((bb, SUBLANES, D_FF), lambda bi, ti: (bi, 0, 0))],
        out_shape=[jax.ShapeDtypeStruct((b, t, D_MODEL), F32), jax.ShapeDtypeStruct((b, SUBLANES, D_FF), F32)],
        scratch_shapes=[pltpu.VMEM((bb, SUBLANES, D_FF), F32), pltpu.VMEM((bb, tt + SUBLANES, D_FF), F32)],
        compiler_params=_params("arbitrary", "arbitrary"),
        name="conv_glu_ffn",
    )(x, buf, g, wg, wv, conv_w, conv_b, wo, g_final)


def _layer_weights(l, w):
    bf = lambda z: z.astype(BF16)
    row = lambda z: z.reshape(1, -1)
    pad_gate = lambda z: jnp.pad(z, (H_A, GATE_PAD - 2 * H_A)).reshape(1, GATE_PAD)
    zeros_lora = jnp.zeros((DECAY_LORA, W_C), F32)
    return dict(
        norm_mix=row(w['norm_mix'][l]),
        gdn_conv_w=w['gdn_conv_w'][l],
        gdn_gate=jnp.concatenate([pad_gate(w['gdn_a_log'][l]), pad_gate(w['gdn_dt_bias'][l])], axis=0),
        gdn_norm=row(jnp.tile(w['gdn_norm'][l], H_A)),
        lam_par=jnp.stack([w['diff_lq1'][l], w['diff_lk1'][l], w['diff_lq2'][l], w['diff_lk2'][l]]),
        subln=row(jnp.tile(w['diff_subln'][l], H_B)),
        rw_mu=row(w['rw_mu'][l]),
        rw_vecs=jnp.stack([w['rw_w0'][l], w['rw_a0'][l], w['rw_k_k'][l], w['rw_k_a'][l],
                           w['rw_r_k'][l].reshape(-1), w['rw_lnx_w'][l], w['rw_lnx_b'][l],
                           jnp.zeros((W_C,), F32)]),
        rw_w2=bf(jnp.concatenate([w['rw_w2'][l], zeros_lora], axis=0)),
        rw_a2=bf(jnp.concatenate([zeros_lora, w['rw_a2'][l]], axis=0)),
        rw_g2=bf(w['rw_g2'][l]),
        w_out=bf(w['w_out'][l]),
        norm_mem=row(w['norm_mem'][l]),
        norm_mem_kv=row(w['norm_mem_kv'][l]),
        mem_wq=bf(w['mem_wq'][l]), mem_wk=bf(w['mem_wk'][l]), mem_wv=bf(w['mem_wv'][l]),
        mem_wo=bf(w['mem_wo'][l]),
        norm_ffn=row(w['norm_ffn'][l]),
        ffn_wg=bf(w['ffn_w_in'][l][:, 0:D_FF]), ffn_wv=bf(w['ffn_w_in'][l][:, D_FF:]),
        ffn_conv_w=w['ffn_conv_w'][l], ffn_conv_b=row(w['ffn_conv_b'][l]),
        ffn_wo=bf(w['ffn_w_out'][l]),
        norm_final=row(w['norm_final']),
    )


def _trunk_layer(x, t_valid, mem_k, mem_v, gdn_buf, gdn_s, rw_prev, rw_s, ffn_buf, attn_fn, p, w_in, layer,
                 proj_bb, mem_bb, ffn_bb, tile_t, final_norm, feature_major):
    b, t, _ = x.shape
    proj = _proj_in(x, p['norm_mix'], w_in, layer, proj_bb, tile_t, feature_major)
    if feature_major:
        a_cols, qt, k_tok, k_new, v_new, vt, c_cols = proj
        o_b = attn_fn(qt, k_tok, vt)
    else:
        a_cols, q_b, k_b, v_b, c_cols = proj
        o_b = attn_fn(q_b, k_b, v_b)
        k_new, v_new = k_b[:, 0:t_valid], v_b[:, 0:t_valid]
    pad_t = lambda z: jnp.pad(z, ((0, 0), (0, -t % CHUNK), (0, 0)))
    o_a, gdn_s_new = _gdn(pad_t(a_cols), gdn_buf, gdn_s, p['gdn_conv_w'], p['gdn_gate'], p['gdn_norm'], t_valid)
    o_c, rw_s_new = _rwkv(pad_t(c_cols), rw_prev.reshape(b, 1, C_COLS), rw_s, p['rw_mu'], p['rw_vecs'], p['rw_w2'],
                          p['rw_a2'], p['rw_g2'], t_valid)
    o_a, o_c = o_a[:, 0:t], o_c[:, 0:t]
    x = _mix_mem(x, o_a, o_b, o_c, p['w_out'], p['norm_mem'], p['mem_wq'], mem_k, mem_v, p['mem_wo'],
                 mem_bb, tile_t, layer if mem_k.ndim == 5 else None)
    x, tail = _ffn(x, ffn_buf, p['norm_ffn'], p['ffn_wg'], p['ffn_wv'], p['ffn_conv_w'], p['ffn_conv_b'],
                   p['ffn_wo'], p['norm_final'], ffn_bb, tile_t, final_norm)
    keep = min(t_valid, GDN_CONV - 1)
    gdn_buf_new = jnp.concatenate([gdn_buf[:, keep:], a_cols[:, t_valid - keep:t_valid, 0:3 * W_A]], axis=1)
    last = t_valid - (t - SUBLANES)
    ffn_buf_new = tail[:, last - (FFN_CONV - 1):last]
    return (x, k_new, v_new, gdn_buf_new, gdn_s_new, c_cols[:, t_valid - 1], rw_s_new, ffn_buf_new)


def kernel(x_prompt, x_sample, cache_diff_k, cache_diff_v, state_gdn_conv, state_gdn, state_rwkv_shift, state_rwkv, state_ffn_conv, cache_mem_k, cache_mem_v, page_table, mem_prompt, norm_mix, w_in, gdn_conv_w, gdn_a_log, gdn_dt_bias, gdn_norm, diff_lq1, diff_lk1, diff_lq2, diff_lk2, diff_subln, rw_mu, rw_w0, rw_w2, rw_a0, rw_a2, rw_g2, rw_k_k, rw_k_a, rw_r_k, rw_lnx_w, rw_lnx_b, w_out, norm_mem, norm_mem_kv, mem_wq, mem_wk, mem_wv, mem_wo, norm_ffn, ffn_w_in, ffn_conv_w, ffn_conv_b, ffn_w_out, norm_final):
    w = dict(norm_mix=norm_mix, w_in=w_in, gdn_conv_w=gdn_conv_w, gdn_a_log=gdn_a_log, gdn_dt_bias=gdn_dt_bias,
             gdn_norm=gdn_norm, diff_lq1=diff_lq1, diff_lk1=diff_lk1, diff_lq2=diff_lq2, diff_lk2=diff_lk2,
             diff_subln=diff_subln, rw_mu=rw_mu, rw_w0=rw_w0, rw_w2=rw_w2, rw_a0=rw_a0, rw_a2=rw_a2, rw_g2=rw_g2,
             rw_k_k=rw_k_k, rw_k_a=rw_k_a, rw_r_k=rw_r_k, rw_lnx_w=rw_lnx_w, rw_lnx_b=rw_lnx_b, w_out=w_out,
             norm_mem=norm_mem, norm_mem_kv=norm_mem_kv, mem_wq=mem_wq, mem_wk=mem_wk, mem_wv=mem_wv,
             mem_wo=mem_wo, norm_ffn=norm_ffn, ffn_w_in=ffn_w_in, ffn_conv_w=ffn_conv_w, ffn_conv_b=ffn_conv_b,
             ffn_w_out=ffn_w_out, norm_final=norm_final)
    depth = w_in.shape[0]
    bp, tp, _ = x_prompt.shape
    bs, ts, _ = x_sample.shape
    mt = mem_prompt.shape[1]
    ts_pad = -(-ts // SUBLANES) * SUBLANES
    n_pool, page = cache_diff_k.shape[1], cache_diff_k.shape[2]
    pool_view = lambda z: jnp.transpose(z, (0, 1, 3, 4, 2)).reshape(depth, n_pool, W_B, page)
    kt_pool = pool_view(cache_diff_k)
    vt_pool = pool_view(cache_diff_v)

    xp = x_prompt
    xs = jnp.pad(x_sample, ((0, 0), (0, ts_pad - ts), (0, 0)))
    prompt_tile = _pick_tile(tp, ROW_TILE)
    outs_p = [[] for _ in range(9)]
    outs_s = [[] for _ in range(7)]
    for l in range(depth):
        lam_init = 0.8 - 0.6 * math.exp(-0.3 * l)
        last = l == depth - 1
        p = _layer_weights(l, w)
        mk, mv = _mem_project(mem_prompt.reshape(bp * mt, D_MODEL), p['norm_mem_kv'], p['mem_wk'], p['mem_wv'])
        mk = mk.reshape(bp, mt, MEM_INNER)
        mv = mv.reshape(bp, mt, MEM_INNER)
        attn_p = functools.partial(_dattn_prompt, lam_par=p['lam_par'], subln=p['subln'], lam_init=lam_init)
        res = _trunk_layer(
            xp, tp, mk, mv,
            jnp.zeros((bp, GDN_CONV - 1, 3 * W_A), F32), jnp.zeros((bp, H_A, HEAD_DIM, HEAD_DIM), F32),
            jnp.zeros((bp, C_COLS), F32), jnp.zeros((bp, H_C, HEAD_DIM, HEAD_DIM), F32),
            jnp.zeros((bp, FFN_CONV - 1, D_FF), F32), attn_p, p, w_in, l, 1, 1, 1, prompt_tile, last, True)
        xp = res[0]
        for acc, val in zip(outs_p, res[1:] + (mk, mv)):
            acc.append(val)
        attn_s = functools.partial(_dattn_paged, kt_pool=kt_pool, vt_pool=vt_pool, layer=l, page_table=page_table,
                                   lam_par=p['lam_par'], subln=p['subln'], lam_init=lam_init, t_valid=ts)
        res = _trunk_layer(
            xs, ts, cache_mem_k, cache_mem_v,
            state_gdn_conv[l], state_gdn[l], state_rwkv_shift[l], state_rwkv[l], state_ffn_conv[l], attn_s, p, w_in, l,
            bs, min(GROUP_B, bs), bs, ts_pad, last, False)
        xs = res[0]
        for acc, val in zip(outs_s, res[1:]):
            acc.append(val)

    st = lambda vals, shape: jnp.stack(vals).reshape((depth,) + shape)
    p_k, p_v, p_gc, p_gs, p_rp, p_rs, p_fc, p_mk, p_mv = outs_p
    s_k, s_v, s_gc, s_gs, s_rp, s_rs, s_fc = outs_s
    kv_p = lambda vals: jnp.transpose(jnp.stack(vals).reshape(depth, bp, H_B, 2 * DB, tp), (0, 1, 4, 2, 3))
    kv_s = (bs, ts, H_B, 2 * DB)
    mem_shape = (bp, mt, MEM_HEADS, MEM_HD)
    return (xp, xs[:, 0:ts],
            kv_p(p_k), kv_p(p_v), jnp.stack(p_gc), jnp.stack(p_gs), jnp.stack(p_rp), jnp.stack(p_rs),
            jnp.stack(p_fc), st(p_mk, mem_shape), st(p_mv, mem_shape),
            st(s_k, kv_s), st(s_v, kv_s), jnp.stack(s_gc), jnp.stack(s_gs), jnp.stack(s_rp), jnp.stack(s_rs),
            jnp.stack(s_fc))
```

```python
import functools
import math

import numpy as np
import jax
import jax.numpy as jnp
from jax import lax
from jax.experimental import pallas as pl
from jax.experimental.pallas import tpu as pltpu

F32 = jnp.float32
BF16 = jnp.bfloat16

D_MODEL = 1024
HEAD_DIM = 64
W_A = 384
W_B = 256
W_C = 384
H_A = W_A // HEAD_DIM
H_B = W_B // HEAD_DIM
H_C = W_C // HEAD_DIM
DB = HEAD_DIM // 2
GDN_CONV = 4
DECAY_LORA = 64
AAA_LORA = 64
GATE_LORA = 128
RWKV_GN_EPS = 64e-5
MEM_HEADS = 4
MEM_HD = 128
MEM_INNER = MEM_HEADS * MEM_HD
D_FF = 2816
FFN_CONV = 3
NORM_EPS = 1e-6
A_COLS = 4 * W_A + 2 * H_A
B_COLS = 3 * W_B
C_COLS = 3 * W_C + DECAY_LORA + AAA_LORA + GATE_LORA
GATE_PAD = 128
A_PAD = 4 * W_A + GATE_PAD

SUBLANES = 8
LANES = 128
VMEM_LIMIT = 56 * 1024 * 1024
GROUP_B = 8
ROW_TILE = 512
ATT_TILE = 1024
PAGES_PER_STEP = 32

_NN = (((1,), (0,)), ((), ()))
_NT = (((1,), (1,)), ((), ()))
_BNN = (((2,), (1,)), ((0,), (0,)))
_BNT = (((2,), (2,)), ((0,), (0,)))
_BTN = (((1,), (1,)), ((0,), (0,)))


def _dot(a, b, dn):
    return lax.dot_general(a.astype(BF16), b.astype(BF16), dn, preferred_element_type=F32)


def _dot_f32(a, b, dn):
    return lax.dot_general(a, b, dn, precision=lax.Precision.HIGHEST, preferred_element_type=F32)


def _sigmoid(x):
    return 0.5 * jnp.tanh(0.5 * x) + 0.5


def _softplus(x):
    return jnp.maximum(x, 0.0) + jnp.log(1.0 + jnp.exp(-jnp.abs(x)))


def _rms(x, g, eps=NORM_EPS):
    return x * lax.rsqrt(jnp.mean(x * x, axis=-1, keepdims=True) + eps) * g


def _pick_tile(n, pref):
    if n <= pref:
        return n
    t = pref - pref % SUBLANES
    while n % t:
        t -= SUBLANES
    return t


def _params(*sem):
    return pltpu.CompilerParams(dimension_semantics=sem, vmem_limit_bytes=VMEM_LIMIT)


def _const_spec(shape):
    nd = len(shape)
    return pl.BlockSpec(shape, lambda *_: (0,) * nd, pipeline_mode=pl.Buffered(1))


def _proj_in_kernel(x_ref, g_ref, w32_ref, *refs, feature_major, n_prev):
    prev, refs = refs[:2 * (n_prev > 0)], refs[2 * (n_prev > 0):]
    outs, (wa_ref, wb_ref, wc_ref) = refs[:-3], refs[-3:]

    @pl.when((pl.program_id(0) == 0) & (pl.program_id(1) == 0))
    def _():
        wa_ref[:, 0:4 * W_A] = w32_ref[0, :, 0:4 * W_A].astype(BF16)
        gate_lane = lax.broadcasted_iota(jnp.int32, (1, GATE_PAD), 1) < 2 * H_A
        wa_ref[:, 4 * W_A:A_PAD] = jnp.where(gate_lane, w32_ref[0, :, 4 * W_A:A_PAD], 0.0).astype(BF16)
        wb_ref[...] = w32_ref[0, :, A_COLS:A_COLS + B_COLS].astype(BF16)
        wc_ref[...] = w32_ref[0, :, A_COLS + B_COLS:A_COLS + B_COLS + C_COLS].astype(BF16)

    bb, tt, _ = x_ref.shape
    m = bb * tt
    h = _rms(x_ref[...].reshape(m, D_MODEL), g_ref[...]).astype(BF16)
    cols = lambda w_ref, lo, n: jnp.dot(h, w_ref[:, lo:lo + n], preferred_element_type=F32)
    if feature_major:
        oa_ref, oqt_ref, ok_ref, okt_ref, ovt_ref, ovtb_ref, oc_ref = outs
        qkv = cols(wb_ref, 0, B_COLS)
        k = qkv[:, W_B:2 * W_B]
        vt = qkv[:, 2 * W_B:3 * W_B].T
        oqt_ref[0] = (qkv[:, 0:W_B] * (DB ** -0.5)).T.astype(BF16)
        ok_ref[0] = k.astype(BF16)
        okt_ref[n_prev, 0] = k.T
        ovt_ref[n_prev, 0] = vt
        ovtb_ref[0] = vt.astype(BF16)
        if n_prev:
            okt_ref[0:n_prev] = prev[0][...]
            ovt_ref[0:n_prev] = prev[1][...]
    else:
        oa_ref, oq_ref, ok_ref, ov_ref, oc_ref = outs
        oq_ref[...] = cols(wb_ref, 0, W_B).reshape(bb, tt, W_B)
        ok_ref[...] = cols(wb_ref, W_B, W_B).reshape(bb, tt, W_B)
        ov_ref[...] = cols(wb_ref, 2 * W_B, W_B).reshape(bb, tt, W_B)
    oa_ref[...] = cols(wa_ref, 0, A_PAD).reshape(bb, tt, A_PAD)
    oc_ref[...] = cols(wc_ref, 0, C_COLS).reshape(bb, tt, C_COLS)


def _proj_in(x, g, w_in, layer, bb, tt, feature_major, prev_kv=()):
    b, t, _ = x.shape
    n_prev = prev_kv[0].shape[0] if prev_kv else 0
    tok = lambda w: pl.BlockSpec((bb, tt, w), lambda bi, ti: (bi, ti, 0))
    feat = pl.BlockSpec((1, W_B, tt), lambda bi, ti: (bi, 0, ti))
    layers = lambda n: pl.BlockSpec((n, 1, W_B, tt), lambda bi, ti: (0, bi, 0, ti))
    sds = jax.ShapeDtypeStruct
    if feature_major:
        assert bb == 1
        out_specs = [tok(A_PAD), feat, tok(W_B), layers(n_prev + 1), layers(n_prev + 1), feat, tok(C_COLS)]
        out_shape = [sds((b, t, A_PAD), F32), sds((b, W_B, t), BF16), sds((b, t, W_B), BF16),
                     sds((n_prev + 1, b, W_B, t), F32), sds((n_prev + 1, b, W_B, t), F32), sds((b, W_B, t), BF16),
                     sds((b, t, C_COLS), F32)]
    else:
        out_specs = [tok(A_PAD), tok(W_B), tok(W_B), tok(W_B), tok(C_COLS)]
        out_shape = [sds((b, t, w), F32) for w in (A_PAD, W_B, W_B, W_B, C_COLS)]
    w_spec = pl.BlockSpec((1,) + w_in.shape[1:], lambda bi, ti: (layer, 0, 0), pipeline_mode=pl.Buffered(1))
    return pl.pallas_call(
        functools.partial(_proj_in_kernel, feature_major=feature_major, n_prev=n_prev),
        grid=(b // bb, t // tt),
        in_specs=[tok(D_MODEL), _const_spec(g.shape), w_spec] + [layers(n_prev)] * len(prev_kv),
        out_specs=out_specs,
        out_shape=out_shape,
        scratch_shapes=[pltpu.VMEM((D_MODEL, w), BF16) for w in (A_PAD, B_COLS, C_COLS)],
        compiler_params=_params("arbitrary", "arbitrary"),
        name="proj_in",
    )(x, g, w_in, *prev_kv)


CHUNK = 64
PAIR = 2 * HEAD_DIM


def _chunk_cumsum(x, c):
    r = lax.broadcasted_iota(jnp.int32, (c, c), 0)
    q = lax.broadcasted_iota(jnp.int32, (c, c), 1)
    tri = jnp.broadcast_to(jnp.where(q <= r, 1.0, 0.0).astype(F32), (x.shape[0], c, c))
    return _dot_f32(tri, x, _BNN)


def _select_dot(x, sel, terms):
    acc = None
    rest = x
    for _ in range(terms):
        piece = rest.astype(BF16)
        part = jnp.dot(piece, sel, preferred_element_type=F32)
        acc = part if acc is None else acc + part
        rest = rest - piece.astype(F32)
    return acc


def _head_sum_matrix(width):
    r = lax.broadcasted_iota(jnp.int32, (width, width), 0) // HEAD_DIM
    q = lax.broadcasted_iota(jnp.int32, (width, width), 1) // HEAD_DIM
    return jnp.where(r == q, 1.0, 0.0).astype(BF16)


def _head_spread_matrix(first_lane, width):
    r = lax.broadcasted_iota(jnp.int32, (LANES, width), 0)
    q = lax.broadcasted_iota(jnp.int32, (LANES, width), 1) // HEAD_DIM
    return jnp.where(r == q + first_lane, 1.0, 0.0).astype(BF16)


def _pair_rows(z):
    first = (lax.broadcasted_iota(jnp.int32, (1, 1, z.shape[-1]), 2) % PAIR) < HEAD_DIM
    return jnp.concatenate([jnp.where(first, z, 0.0), jnp.where(first, 0.0, z)], axis=1)


def _pair_solve(l, x, c, live_rows):
    width = x.shape[-1]
    steps = max(1, math.ceil(math.log2(live_rows)))
    p = l
    for step in range(steps):
        if step == steps - 1:
            return x + _dot(p, _pair_rows(x), _BNN)
        both = _dot(p, _pair_rows(jnp.concatenate([x, p], axis=-1)), _BNN)
        x = x + both[:, :, 0:width]
        p = both[:, :, width:width + 2 * c]


def _load_pair_state(sp_ref, s0_ref, heads):
    sp_ref[...] = jnp.zeros(sp_ref.shape, F32)
    for p in range(heads // 2):
        sp_ref[:, p, 0:HEAD_DIM, 0:HEAD_DIM] = s0_ref[:, 2 * p]
        sp_ref[:, p, HEAD_DIM:PAIR, HEAD_DIM:PAIR] = s0_ref[:, 2 * p + 1]


def _store_pair_state(s_ref, sp_ref, heads):
    for p in range(heads // 2):
        s_ref[:, 2 * p] = sp_ref[:, p, 0:HEAD_DIM, 0:HEAD_DIM]
        s_ref[:, 2 * p + 1] = sp_ref[:, p, HEAD_DIM:PAIR, HEAD_DIM:PAIR]


def _pair_masks(c):
    i = lax.broadcasted_iota(jnp.int32, (c, 2 * c), 0)
    j = lax.broadcasted_iota(jnp.int32, (c, 2 * c), 1) % c
    r = lax.broadcasted_iota(jnp.int32, (PAIR, PAIR), 0) // HEAD_DIM
    q = lax.broadcasted_iota(jnp.int32, (PAIR, PAIR), 1) // HEAD_DIM
    return j < i, j <= i, r == q


def _gdn_kernel(a_ref, buf_ref, s0_ref, cw_ref, gp_ref, ng_ref, o_ref, s_ref, xp_ref, sp_ref, *, t_valid):
    ci = pl.program_id(1)
    g_b, c, _ = a_ref.shape
    m = g_b * c
    hist = SUBLANES - (GDN_CONV - 1)

    @pl.when(ci == 0)
    def _():
        xp_ref[:, hist:SUBLANES, :] = buf_ref[...]
        _load_pair_state(sp_ref, s0_ref, H_A)

    xp_ref[:, SUBLANES:SUBLANES + c, :] = a_ref[:, :, 0:3 * W_A]
    y = xp_ref[:, hist:hist + c, :] * cw_ref[0:1, :]
    for j in range(1, GDN_CONV):
        y = y + xp_ref[:, hist + j:hist + j + c, :] * cw_ref[j:j + 1, :]
    xp_ref[:, hist:SUBLANES, :] = xp_ref[:, c + hist:c + SUBLANES, :]
    qkv = (y * _sigmoid(y)).reshape(m, 3 * W_A)

    head_sum = _head_sum_matrix(W_A)
    q = qkv[:, 0:W_A]
    k = qkv[:, W_A:2 * W_A]
    v = qkv[:, 2 * W_A:3 * W_A]
    q = q * lax.rsqrt(_select_dot(q * q, head_sum, 2) + 1e-6) * (HEAD_DIM ** -0.5)
    k = k * lax.rsqrt(_select_dot(k * k, head_sum, 2) + 1e-6)

    gates = a_ref[:, :, 4 * W_A:A_PAD]
    tpos = ci * c + lax.broadcasted_iota(jnp.int32, (1, c, 1), 1)
    valid = tpos < t_valid
    beta_all = jnp.where(valid, _sigmoid(gates), 0.0)
    g_all = jnp.where(valid, -jnp.exp(gp_ref[0:1, :]) * _softplus(gates + gp_ref[1:2, :]), 0.0)
    gam_all = _chunk_cumsum(g_all, c)
    gam_t = jnp.swapaxes(gam_all, 1, 2)
    beta = _select_dot(beta_all.reshape(m, LANES), _head_spread_matrix(0, W_A), 3)
    gam = _select_dot(gam_all.reshape(m, LANES), _head_spread_matrix(H_A, W_A), 3)
    eg = jnp.exp(gam)
    kb = k * beta
    to3 = lambda z: z.reshape(g_b, c, W_A)
    gam3 = to3(gam)
    glast = gam3[:, c - 1:c, :]
    q3, k3, kb3, qe3, vb3, kbe3 = (to3(z) for z in (q, k, kb, q * eg, v * beta, kb * eg))
    kd3 = k3 * jnp.exp(glast - gam3)
    gl = jnp.exp(glast)
    strict, incl, diag_blocks = _pair_masks(c)

    n_pairs = H_A // 2
    stack = lambda z: jnp.concatenate([z[:, :, p * PAIR:(p + 1) * PAIR] for p in range(n_pairs)], axis=0)
    grow = jnp.concatenate(
        [jnp.concatenate([gam_t[:, H_A + 2 * p:H_A + 2 * p + 1, :], gam_t[:, H_A + 2 * p + 1:H_A + 2 * p + 2, :]],
                         axis=-1) for p in range(n_pairs)], axis=0)
    diff = stack(gam3) - grow
    k_rows = _pair_rows(stack(k3))
    kk = _dot(jnp.concatenate([stack(kb3), stack(q3)], axis=1), k_rows, _BNT)
    a_mat = kk[:, 0:c, :] * jnp.exp(jnp.where(strict, diff, -jnp.inf))
    qk = kk[:, c:2 * c, :] * jnp.exp(jnp.where(incl, diff, -jnp.inf))
    x = jnp.concatenate([stack(vb3), stack(kbe3)], axis=-1)
    x = _pair_solve(-a_mat, x, c, min(c, t_valid))
    s = jnp.concatenate([sp_ref[:, p] for p in range(n_pairs)], axis=0)
    ws = _dot(jnp.concatenate([x[:, :, PAIR:2 * PAIR], stack(qe3)], axis=1), s, _BNN)
    v_new = x[:, :, 0:PAIR] - ws[:, 0:c, :]
    o = ws[:, c:2 * c, :] + _dot(qk, _pair_rows(v_new), _BNN)
    s_new = jnp.where(diag_blocks, s * stack(gl) + _dot(stack(kd3), v_new, _BTN), 0.0)
    for p in range(n_pairs):
        sp_ref[:, p] = s_new[p * g_b:(p + 1) * g_b]
    outs = [o[p * g_b:(p + 1) * g_b] for p in range(n_pairs)]
    o = jnp.concatenate(outs, axis=-1).reshape(m, W_A)
    o = o * lax.rsqrt(_select_dot(o * o, head_sum, 2) * (1.0 / HEAD_DIM) + NORM_EPS) * ng_ref[...]
    z = a_ref[:, :, 3 * W_A:4 * W_A].reshape(m, W_A)
    o_ref[...] = (o * (z * _sigmoid(z))).reshape(g_b, c, W_A)

    @pl.when(ci == pl.num_programs(1) - 1)
    def _():
        _store_pair_state(s_ref, sp_ref, H_A)


def _gdn(a_cols, conv_buf, s0, conv_w, gate_par, norm_g, t_valid):
    b, t, _ = a_cols.shape
    g_b = min(GROUP_B, b)
    c = CHUNK
    kern = functools.partial(_gdn_kernel, t_valid=t_valid)
    state = pl.BlockSpec((g_b, H_A, HEAD_DIM, HEAD_DIM), lambda bi, ci: (bi, 0, 0, 0))
    return pl.pallas_call(
        kern,
        grid=(b // g_b, t // c),
        in_specs=[pl.BlockSpec((g_b, c, A_PAD), lambda bi, ci: (bi, ci, 0)),
                  pl.BlockSpec((g_b, GDN_CONV - 1, 3 * W_A), lambda bi, ci: (bi, 0, 0)),
                  state, _const_spec(conv_w.shape), _const_spec(gate_par.shape), _const_spec(norm_g.shape)],
        out_specs=[pl.BlockSpec((g_b, c, W_A), lambda bi, ci: (bi, ci, 0)), state],
        out_shape=[jax.ShapeDtypeStruct((b, t, W_A), F32),
                   jax.ShapeDtypeStruct((b, H_A, HEAD_DIM, HEAD_DIM), F32)],
        scratch_shapes=[pltpu.VMEM((g_b, c + SUBLANES, 3 * W_A), F32),
                        pltpu.VMEM((g_b, H_A // 2, PAIR, PAIR), F32)],
        compiler_params=_params("arbitrary", "arbitrary"),
        name="gdn_mixer",
    )(a_cols, conv_buf, s0, conv_w, gate_par, norm_g)


def _rwkv_kernel(c_ref, prev_ref, s0_ref, mu_ref, vec_ref, w2_ref, a2_ref, g2_ref, o_ref, s_ref, xp_ref, sp_ref,
                 *, t_valid):
    ci = pl.program_id(1)
    g_b, c, _ = c_ref.shape
    m = g_b * c

    @pl.when(ci == 0)
    def _():
        xp_ref[:, SUBLANES - 1:SUBLANES, :] = prev_ref[...]
        _load_pair_state(sp_ref, s0_ref, H_C)

    x = c_ref[...]
    xp_ref[:, SUBLANES:SUBLANES + c, :] = x
    shifted = xp_ref[:, SUBLANES - 1:SUBLANES - 1 + c, :]
    xp_ref[:, SUBLANES - 1:SUBLANES, :] = xp_ref[:, c + SUBLANES - 1:c + SUBLANES, :]
    xs = (x + (shifted - x) * mu_ref[...]).reshape(m, C_COLS)

    w0, a0, k_k, k_a, r_k, lnx_w, lnx_b = (vec_ref[i:i + 1, :] for i in range(7))
    r = xs[:, 0:W_C]
    k = xs[:, W_C:2 * W_C]
    v = xs[:, 2 * W_C:3 * W_C]
    lora = xs[:, 3 * W_C:3 * W_C + DECAY_LORA + AAA_LORA]
    gd = xs[:, 3 * W_C + DECAY_LORA + AAA_LORA:C_COLS]
    lane = lax.broadcasted_iota(jnp.int32, (1, DECAY_LORA + AAA_LORA), 1)
    lora = jnp.where(lane < DECAY_LORA, jnp.tanh(lora), lora)
    w = -_softplus(-(w0 + _dot(lora, w2_ref[...], _NN))) - 0.5
    a = _sigmoid(a0 + _dot(lora, a2_ref[...], _NN))
    gate = _dot(_sigmoid(gd), g2_ref[...], _NN)
    head_sum = _head_sum_matrix(W_C)
    kk = k * k_k
    kk = kk * lax.rsqrt(_select_dot(kk * kk, head_sum, 2) + 1e-12)
    k = k * (1.0 + (a - 1.0) * k_a)

    tpos = ci * c + lax.broadcasted_iota(jnp.int32, (1, c, 1), 1)
    valid = tpos < t_valid
    to3 = lambda z: z.reshape(g_b, c, W_C)
    lw = jnp.where(valid, to3(-jnp.exp(w)), 0.0)
    cl = _chunk_cumsum(lw, c)
    p_in = jnp.exp(cl)
    p_inv = jnp.exp(-cl)
    k3 = jnp.where(valid, to3(k), 0.0)
    kk3 = jnp.where(valid, to3(kk), 0.0)
    v3 = to3(v)
    at = -kk3 * jnp.exp(cl - lw)
    bt = kk3 * to3(a) * p_inv
    kt = k3 * p_inv
    rt = to3(r) * p_in

    strict, incl, diag_blocks = _pair_masks(c)
    tri4 = jnp.concatenate([jnp.concatenate([strict, strict], axis=1),
                            jnp.concatenate([incl, incl], axis=1)], axis=0)

    n_pairs = H_C // 2
    stack = lambda z: jnp.concatenate([z[:, :, p * PAIR:(p + 1) * PAIR] for p in range(n_pairs)], axis=0)
    atp, btp, ktp, rtp, vp = stack(at), stack(bt), stack(kt), stack(rt), stack(v3)
    lhs = jnp.concatenate([atp, rtp], axis=1)
    rhs = jnp.concatenate([_pair_rows(btp), _pair_rows(ktp)], axis=1)
    gm = jnp.where(tri4, _dot(lhs, rhs, _BNT), 0.0)
    top = gm[:, 0:c, :]
    bot = gm[:, c:2 * c, :]
    v_rows = _pair_rows(vp)
    s = jnp.concatenate([sp_ref[:, p] for p in range(n_pairs)], axis=0)
    from_state = _dot(lhs, s, _BNT)
    rhs_u = from_state[:, 0:c, :] + _dot(top[:, :, 2 * c:4 * c], v_rows, _BNN)
    u = _pair_solve(top[:, :, 0:2 * c], rhs_u, c, min(c, t_valid))
    o = from_state[:, c:2 * c, :] + _dot(bot, jnp.concatenate([_pair_rows(u), v_rows], axis=1), _BNN)
    s_new = s + _dot(jnp.concatenate([u, vp], axis=1), jnp.concatenate([btp, ktp], axis=1), _BTN)
    s_new = jnp.where(diag_blocks, s_new, 0.0) * stack(p_in[:, c - 1:c, :])
    for p in range(n_pairs):
        sp_ref[:, p] = s_new[p * g_b:(p + 1) * g_b]
    outs = [o[p * g_b:(p + 1) * g_b] for p in range(n_pairs)]
    o = jnp.concatenate(outs, axis=-1).reshape(m, W_C)
    inv_n = 1.0 / HEAD_DIM
    mean = _select_dot(o, head_sum, 2) * inv_n
    var = _select_dot(jnp.square(o - mean), head_sum, 2) * inv_n
    o = (o - mean) * lax.rsqrt(var + RWKV_GN_EPS) * lnx_w + lnx_b
    k2 = k3.reshape(m, W_C)
    o = o + _select_dot(r * k2 * r_k, head_sum, 2) * v
    o_ref[...] = (o * gate).reshape(g_b, c, W_C)

    @pl.when(ci == pl.num_programs(1) - 1)
    def _():
        _store_pair_state(s_ref, sp_ref, H_C)


def _rwkv(c_cols, prev, s0, mu, vecs, w2p, a2p, g2, t_valid):
    b, t, _ = c_cols.shape
    g_b = min(GROUP_B, b)
    c = CHUNK
    kern = functools.partial(_rwkv_kernel, t_valid=t_valid)
    state = pl.BlockSpec((g_b, H_C, HEAD_DIM, HEAD_DIM), lambda bi, ci: (bi, 0, 0, 0))
    return pl.pallas_call(
        kern,
        grid=(b // g_b, t // c),
        in_specs=[pl.BlockSpec((g_b, c, C_COLS), lambda bi, ci: (bi, ci, 0)),
                  pl.BlockSpec((g_b, 1, C_COLS), lambda bi, ci: (bi, 0, 0)),
                  state, _const_spec(mu.shape), _const_spec(vecs.shape), _const_spec(w2p.shape),
                  _const_spec(a2p.shape), _const_spec(g2.shape)],
        out_specs=[pl.BlockSpec((g_b, c, W_C), lambda bi, ci: (bi, ci, 0)), state],
        out_shape=[jax.ShapeDtypeStruct((b, t, W_C), F32),
                   jax.ShapeDtypeStruct((b, H_C, HEAD_DIM, HEAD_DIM), F32)],
        scratch_shapes=[pltpu.VMEM((g_b, c + SUBLANES, C_COLS), F32),
                        pltpu.VMEM((g_b, H_C // 2, PAIR, PAIR), F32)],
        compiler_params=_params("arbitrary", "arbitrary"),
        name="rwkv_mixer",
    )(c_cols, prev, s0, mu, vecs, w2p, a2p, g2)


def _diff_lambda(lp_ref, lam_init):
    l1 = jnp.exp(jnp.sum(lp_ref[0:1, :] * lp_ref[1:2, :], axis=-1, keepdims=True))
    l2 = jnp.exp(jnp.sum(lp_ref[2:3, :] * lp_ref[3:4, :], axis=-1, keepdims=True))
    return l1 - l2 + lam_init


def _dattn_kernel(qi_ref, ki_ref, lp_ref, qt_ref, k_ref, vt_ref, sub_ref, o_ref, qm_ref, m_ref, l_ref, acc_ref,
                  *, lam_init):
    step = pl.program_id(1)
    qi = qi_ref[step]
    ki = ki_ref[step]
    tq = qt_ref.shape[2]
    tk = k_ref.shape[1]

    @pl.when(ki == 0)
    def _():
        m_ref[...] = jnp.full(m_ref.shape, -jnp.inf, F32)
        l_ref[...] = jnp.zeros(l_ref.shape, F32)
        acc_ref[...] = jnp.zeros(acc_ref.shape, F32)
        fmap = lax.broadcasted_iota(jnp.int32, (W_B, 1), 0) // DB
        qt = qt_ref[0]
        for hm in range(2 * H_B):
            qm_ref[hm] = jnp.where(fmap == hm, qt, jnp.zeros_like(qt))

    def scores(hm, diagonal):
        st = jnp.dot(k_ref[0], qm_ref[hm], preferred_element_type=F32)
        if diagonal:
            kpos = lax.broadcasted_iota(jnp.int32, (tk, tq), 0)
            qpos = lax.broadcasted_iota(jnp.int32, (tk, tq), 1)
            st = jnp.where(kpos <= qpos, st, -jnp.inf)
        m = m_ref[hm]
        return st, m, jnp.maximum(m, jnp.max(st, axis=0, keepdims=True))

    def weights(hm, st, m, m_new):
        alpha = jnp.exp(m - m_new)
        p = jnp.exp(st - m_new)
        l_ref[hm] = alpha * l_ref[hm] + jnp.sum(p, axis=0, keepdims=True)
        m_ref[hm] = m_new
        return alpha, p.astype(BF16)

    def accumulate(hm, alpha, p):
        h = hm // 2
        pv = jnp.dot(vt_ref[0, h * HEAD_DIM:(h + 1) * HEAD_DIM, :], p, preferred_element_type=F32)
        acc_ref[hm] = alpha * acc_ref[hm] + pv

    def tile(diagonal):
        n = 2 * H_B
        sc = scores(0, diagonal)
        for hm in range(n):
            wt = weights(hm, *sc)
            if hm + 1 < n:
                sc = scores(hm + 1, diagonal)
            accumulate(hm, *wt)

    pl.when(ki < qi)(lambda: tile(False))

    @pl.when(ki == qi)
    def _():
        tile(True)
        lam = _diff_lambda(lp_ref, lam_init)
        heads = []
        for h in range(H_B):
            oh = acc_ref[2 * h] / l_ref[2 * h] - lam * (acc_ref[2 * h + 1] / l_ref[2 * h + 1])
            heads.append(oh * lax.rsqrt(jnp.mean(oh * oh, axis=0, keepdims=True) + NORM_EPS))
        o = jnp.concatenate(heads, axis=0).T
        o_ref[0] = o * sub_ref[...] * (1.0 - lam_init)


def _dattn_prompt(qt, k, vt, lam_par, subln, lam_init):
    b, _, t = qt.shape
    tile = _pick_tile(t, ATT_TILE)
    n = t // tile
    pairs = [(qi, ki) for qi in range(n) for ki in range(qi + 1)]
    qi_tab = jnp.asarray(np.array([p[0] for p in pairs], np.int32))
    ki_tab = jnp.asarray(np.array([p[1] for p in pairs], np.int32))
    const = lambda shape: pl.BlockSpec(shape, lambda bi, s, qi, ki: (0,) * len(shape))
    grid_spec = pltpu.PrefetchScalarGridSpec(
        num_scalar_prefetch=2,
        grid=(b, len(pairs)),
        in_specs=[const(lam_par.shape),
                  pl.BlockSpec((1, W_B, tile), lambda bi, s, qi, ki: (bi, 0, qi[s])),
                  pl.BlockSpec((1, tile, W_B), lambda bi, s, qi, ki: (bi, ki[s], 0)),
                  pl.BlockSpec((1, W_B, tile), lambda bi, s, qi, ki: (bi, 0, ki[s])),
                  const(subln.shape)],
        out_specs=pl.BlockSpec((1, tile, W_B), lambda bi, s, qi, ki: (bi, qi[s], 0)),
        scratch_shapes=[pltpu.VMEM((2 * H_B, W_B, tile), BF16), pltpu.VMEM((2 * H_B, 1, tile), F32),
                        pltpu.VMEM((2 * H_B, 1, tile), F32), pltpu.VMEM((2 * H_B, HEAD_DIM, tile), F32)],
    )
    return pl.pallas_call(
        functools.partial(_dattn_kernel, lam_init=lam_init),
        grid_spec=grid_spec,
        out_shape=jax.ShapeDtypeStruct((b, t, W_B), F32),
        compiler_params=_params("arbitrary", "arbitrary"),
        name="diff_attn_prompt",
    )(qi_tab, ki_tab, lam_par, qt, k, vt, subln)


def _softmax_rows(s, m_ref, l_ref):
    m_old = m_ref[...]
    m_new = jnp.maximum(m_old, jnp.max(s, axis=-1, keepdims=True))
    alpha = jnp.exp(m_old - m_new)
    p = jnp.exp(s - m_new)
    l_ref[...] = alpha * l_ref[...] + jnp.sum(p, axis=-1, keepdims=True)
    m_ref[...] = m_new
    return p, alpha


def _paged_kernel(pt_ref, lp_ref, q_ref, kn_ref, vn_ref, sub_ref, *refs, pages, lam_init, t_valid):
    k_pages = refs[0:pages]
    v_pages = refs[pages:2 * pages]
    o_ref, m_ref, l_ref, acc_ref = refs[2 * pages:]
    j = pl.program_id(1)
    tp = q_ref.shape[1]
    maps = lax.broadcasted_iota(jnp.int32, (1, W_B), 1) // DB

    @pl.when(j == 0)
    def _():
        m_ref[...] = jnp.full(m_ref.shape, -jnp.inf, F32)
        l_ref[...] = jnp.zeros(l_ref.shape, F32)
        acc_ref[...] = jnp.zeros(acc_ref.shape, F32)

    q = q_ref[0]
    qm = jnp.concatenate([jnp.where(maps == hm, q, 0.0) for hm in range(2 * H_B)], axis=0).astype(BF16)
    scale = DB ** -0.5

    kt = jnp.concatenate([kp[0, 0].astype(BF16) for kp in k_pages], axis=1)
    vt = jnp.concatenate([vp[0, 0].astype(BF16) for vp in v_pages], axis=1)
    s = jnp.dot(qm, kt, preferred_element_type=F32) * scale
    p, alpha = _softmax_rows(s, m_ref, l_ref)
    acc_ref[...] = alpha * acc_ref[...] + _dot(p, vt, _NT)

    @pl.when(j == pl.num_programs(1) - 1)
    def _():
        rows = 2 * H_B * tp
        tq_pos = lax.broadcasted_iota(jnp.int32, (rows, tp), 0) % tp
        tk_pos = lax.broadcasted_iota(jnp.int32, (rows, tp), 1)
        s_new = _dot(qm, kn_ref[0], _NT) * scale
        s_new = jnp.where((tk_pos <= tq_pos) & (tk_pos < t_valid), s_new, -jnp.inf)
        p_new, alpha_new = _softmax_rows(s_new, m_ref, l_ref)
        res = (alpha_new * acc_ref[...] + _dot(p_new, vn_ref[0], _NN)) / l_ref[...]
        lam = _diff_lambda(lp_ref, lam_init)
        heads = lax.broadcasted_iota(jnp.int32, (1, W_B), 1) // HEAD_DIM
        o = jnp.zeros((tp, W_B), F32)
        for h in range(H_B):
            oh = res[2 * h * tp:(2 * h + 1) * tp] - lam * res[(2 * h + 1) * tp:(2 * h + 2) * tp]
            o = jnp.where(heads == h, oh, o)
        r = lax.broadcasted_iota(jnp.int32, (W_B, W_B), 0) // HEAD_DIM
        c = lax.broadcasted_iota(jnp.int32, (W_B, W_B), 1) // HEAD_DIM
        ms = _dot_f32(o * o, jnp.where(r == c, 1.0 / HEAD_DIM, 0.0).astype(F32), _NN)
        o_ref[0] = o * lax.rsqrt(ms + NORM_EPS) * sub_ref[...] * (1.0 - lam_init)


def _dattn_paged(q, k, v, kt_pool, vt_pool, layer, page_table, lam_par, subln, lam_init, t_valid):
    b, tp, _ = q.shape
    n_pages = page_table.shape[1]
    page = kt_pool.shape[3]
    pages = math.gcd(n_pages, PAGES_PER_STEP)
    rows = 2 * H_B * tp
    kern = functools.partial(_paged_kernel, pages=pages, lam_init=lam_init, t_valid=t_valid)

    def page_spec(i):
        return pl.BlockSpec((1, 1, W_B, page), lambda bi, j, pt: (layer, pt[bi, j * pages + i], 0, 0))

    tok = pl.BlockSpec((1, tp, W_B), lambda bi, j, pt: (bi, 0, 0))
    const = lambda shape: pl.BlockSpec(shape, lambda bi, j, pt: (0,) * len(shape))
    grid_spec = pltpu.PrefetchScalarGridSpec(
        num_scalar_prefetch=1,
        grid=(b, n_pages // pages),
        in_specs=[const(lam_par.shape), tok, tok, tok, const(subln.shape)]
        + [page_spec(i) for i in range(pages)] * 2,
        out_specs=tok,
        scratch_shapes=[pltpu.VMEM((rows, 1), F32), pltpu.VMEM((rows, 1), F32), pltpu.VMEM((rows, W_B), F32)],
    )
    return pl.pallas_call(
        kern,
        grid_spec=grid_spec,
        out_shape=jax.ShapeDtypeStruct((b, tp, W_B), F32),
        compiler_params=_params("arbitrary", "arbitrary"),
        name="diff_attn_paged",
    )(page_table, lam_par, q, k, v, subln, *([kt_pool] * pages), *([vt_pool] * pages))


def _mem_project_kernel(m_ref, g_ref, wk_ref, wv_ref, k_ref, v_ref):
    h = _rms(m_ref[...], g_ref[...]).astype(BF16)
    k_ref[...] = jnp.dot(h, wk_ref[...], preferred_element_type=F32)
    v_ref[...] = jnp.dot(h, wv_ref[...], preferred_element_type=F32)


def _mem_project(mem2d, g, wk, wv):
    n = mem2d.shape[0]
    tm = _pick_tile(n, ROW_TILE)
    return pl.pallas_call(
        _mem_project_kernel,
        grid=(n // tm,),
        in_specs=[pl.BlockSpec((tm, D_MODEL), lambda i: (i, 0)), _const_spec(g.shape), _const_spec(wk.shape),
                  _const_spec(wv.shape)],
        out_specs=[pl.BlockSpec((tm, MEM_INNER), lambda i: (i, 0))] * 2,
        out_shape=[jax.ShapeDtypeStruct((n, MEM_INNER), F32)] * 2,
        compiler_params=_params("arbitrary"),
        name="mem_project",
    )(mem2d, g, wk, wv)


def _mix_mem_kernel(x_ref, oa_ref, ob_ref, oc_ref, wout_ref, g_ref, wq_ref, mk_ref, mv_ref, wo_ref, out_ref):
    bb, tt, _ = x_ref.shape
    m = bb * tt
    x = x_ref[...].reshape(m, D_MODEL)
    mixed = jnp.concatenate([oa_ref[...].reshape(m, W_A), ob_ref[...].reshape(m, W_B), oc_ref[...].reshape(m, W_C)],
                            axis=-1)
    x = x + _dot(mixed, wout_ref[...], _NN)
    q = _dot(_rms(x, g_ref[...]), wq_ref[...], _NN)
    heads = []
    for h in range(MEM_HEADS):
        sl = slice(h * MEM_HD, (h + 1) * MEM_HD)
        qh = q[:, sl].reshape(bb, tt, MEM_HD)
        mk = mk_ref[0, :, :, h, :] if len(mk_ref.shape) == 5 else mk_ref[:, :, sl]
        mv = mv_ref[0, :, :, h, :] if len(mv_ref.shape) == 5 else mv_ref[:, :, sl]
        s = _dot(qh, mk, _BNT) * (MEM_HD ** -0.5)
        p = jnp.exp(s - jnp.max(s, axis=-1, keepdims=True))
        p = p / jnp.sum(p, axis=-1, keepdims=True)
        heads.append(_dot(p, mv, _BNN).reshape(m, MEM_HD))
    o = jnp.concatenate(heads, axis=-1)
    out_ref[...] = (x + _dot(o, wo_ref[...], _NN)).reshape(bb, tt, D_MODEL)


def _mix_mem(x, o_a, o_b, o_c, w_out, g, wq, mem_k, mem_v, wo, bb, tt, layer=None):
    b, t, _ = x.shape
    blk = lambda w: pl.BlockSpec((bb, tt, w), lambda bi, ti: (bi, ti, 0))
    if layer is None:
        mem = pl.BlockSpec((bb,) + mem_k.shape[1:], lambda bi, ti: (bi, 0, 0))
    else:
        mem = pl.BlockSpec((1, bb) + mem_k.shape[2:], lambda bi, ti: (layer, bi, 0, 0, 0))
    return pl.pallas_call(
        _mix_mem_kernel,
        grid=(b // bb, t // tt),
        in_specs=[blk(D_MODEL), blk(W_A), blk(W_B), blk(W_C), _const_spec(w_out.shape), _const_spec(g.shape),
                  _const_spec(wq.shape), mem, mem, _const_spec(wo.shape)],
        out_specs=blk(D_MODEL),
        out_shape=jax.ShapeDtypeStruct((b, t, D_MODEL), F32),
        compiler_params=_params("arbitrary", "arbitrary"),
        name="mix_out_mem_attn",
    )(x, o_a, o_b, o_c, w_out, g, wq, mem_k, mem_v, wo)


def _ffn_kernel(x_ref, buf_ref, g_ref, wg_ref, wv_ref, cw_ref, cb_ref, wo_ref, gf_ref, out_ref, tail_ref,
                carry_ref, gp_ref, *, final_norm):
    ti = pl.program_id(1)
    bb, tt, _ = x_ref.shape
    m = bb * tt
    hist = SUBLANES - (FFN_CONV - 1)

    @pl.when(ti == 0)
    def _():
        carry_ref[:, hist:SUBLANES, :] = buf_ref[...]

    x = x_ref[...].reshape(m, D_MODEL)
    h = _rms(x, g_ref[...]).astype(BF16)
    gate = jnp.dot(h, wg_ref[...], preferred_element_type=F32).reshape(bb, tt, D_FF)
    val = jnp.dot(h, wv_ref[...], preferred_element_type=F32)
    gp_ref[:, hist:SUBLANES, :] = carry_ref[:, hist:SUBLANES, :]
    gp_ref[:, SUBLANES:SUBLANES + tt, :] = gate
    y = cb_ref[...] + gate * cw_ref[FFN_CONV - 1:FFN_CONV, :]
    for j in range(FFN_CONV - 1):
        y = y + gp_ref[:, hist + j:hist + j + tt, :] * cw_ref[j:j + 1, :]
    carry_ref[...] = gate[:, tt - SUBLANES:tt, :]
    tail_ref[...] = gate[:, tt - SUBLANES:tt, :]
    y = y.reshape(m, D_FF)
    act = (y * _sigmoid(y) * val).astype(BF16)
    y = x + jnp.dot(act, wo_ref[...], preferred_element_type=F32)
    if final_norm:
        y = _rms(y, gf_ref[...])
    out_ref[...] = y.reshape(bb, tt, D_MODEL)


def _ffn(x, buf, g, wg, wv, conv_w, conv_b, wo, g_final, bb, tt, final_norm):
    b, t, _ = x.shape
    kern = functools.partial(_ffn_kernel, final_norm=final_norm)
    blk = pl.BlockSpec((bb, tt, D_MODEL), lambda bi, ti: (bi, ti, 0))
    return pl.pallas_call(
        kern,
        grid=(b // bb, t // tt),
        in_specs=[blk, pl.BlockSpec((bb, FFN_CONV - 1, D_FF), lambda bi, ti: (bi, 0, 0)), _const_spec(g.shape),
                  _const_spec(wg.shape), _const_spec(wv.shape), _const_spec(conv_w.shape),
                  _const_spec(conv_b.shape), _const_spec(wo.shape), _const_spec(g_final.shape)],
        out_specs=[blk, pl.BlockSpec((bb, SUBLANES, D_FF), lambda bi, ti: (bi, 0, 0))],
        out_shape=[jax.ShapeDtypeStruct((b, t, D_MODEL), F32), jax.ShapeDtypeStruct((b, SUBLANES, D_FF), F32)],
        scratch_shapes=[pltpu.VMEM((bb, SUBLANES, D_FF), F32), pltpu.VMEM((bb, tt + SUBLANES, D_FF), F32)],
        compiler_params=_params("arbitrary", "arbitrary"),
        name="conv_glu_ffn",
    )(x, buf, g, wg, wv, conv_w, conv_b, wo, g_final)


def _layer_weights(l, w):
    bf = lambda z: z.astype(BF16)
    row = lambda z: z.reshape(1, -1)
    pad_gate = lambda z: jnp.pad(z, (H_A, GATE_PAD - 2 * H_A)).reshape(1, GATE_PAD)
    zeros_lora = jnp.zeros((DECAY_LORA, W_C), F32)
    return dict(
        norm_mix=row(w['norm_mix'][l]),
        gdn_conv_w=w['gdn_conv_w'][l],
        gdn_gate=jnp.concatenate([pad_gate(w['gdn_a_log'][l]), pad_gate(w['gdn_dt_bias'][l])], axis=0),
        gdn_norm=row(jnp.tile(w['gdn_norm'][l], H_A)),
        lam_par=jnp.stack([w['diff_lq1'][l], w['diff_lk1'][l], w['diff_lq2'][l], w['diff_lk2'][l]]),
        subln=row(jnp.tile(w['diff_subln'][l], H_B)),
        rw_mu=row(w['rw_mu'][l]),
        rw_vecs=jnp.stack([w['rw_w0'][l], w['rw_a0'][l], w['rw_k_k'][l], w['rw_k_a'][l],
                           w['rw_r_k'][l].reshape(-1), w['rw_lnx_w'][l], w['rw_lnx_b'][l],
                           jnp.zeros((W_C,), F32)]),
        rw_w2=bf(jnp.concatenate([w['rw_w2'][l], zeros_lora], axis=0)),
        rw_a2=bf(jnp.concatenate([zeros_lora, w['rw_a2'][l]], axis=0)),
        rw_g2=bf(w['rw_g2'][l]),
        w_out=bf(w['w_out'][l]),
        norm_mem=row(w['norm_mem'][l]),
        norm_mem_kv=row(w['norm_mem_kv'][l]),
        mem_wq=bf(w['mem_wq'][l]), mem_wk=bf(w['mem_wk'][l]), mem_wv=bf(w['mem_wv'][l]),
        mem_wo=bf(w['mem_wo'][l]),
        norm_ffn=row(w['norm_ffn'][l]),
        ffn_wg=bf(w['ffn_w_in'][l][:, 0:D_FF]), ffn_wv=bf(w['ffn_w_in'][l][:, D_FF:]),
        ffn_conv_w=w['ffn_conv_w'][l], ffn_conv_b=row(w['ffn_conv_b'][l]),
        ffn_wo=bf(w['ffn_w_out'][l]),
        norm_final=row(w['norm_final']),
    )


def _trunk_layer(x, t_valid, mem_k, mem_v, gdn_buf, gdn_s, rw_prev, rw_s, ffn_buf, attn_fn, p, w_in, layer,
                 proj_bb, mem_bb, ffn_bb, tile_t, final_norm, feature_major, prev_kv=()):
    b, t, _ = x.shape
    proj = _proj_in(x, p['norm_mix'], w_in, layer, proj_bb, tile_t, feature_major, prev_kv)
    if feature_major:
        a_cols, qt, k_tok, k_new, v_new, vt, c_cols = proj
        o_b = attn_fn(qt, k_tok, vt)
    else:
        a_cols, q_b, k_b, v_b, c_cols = proj
        o_b = attn_fn(q_b, k_b, v_b)
        k_new, v_new = k_b[:, 0:t_valid], v_b[:, 0:t_valid]
    pad_t = lambda z: jnp.pad(z, ((0, 0), (0, -t % CHUNK), (0, 0)))
    o_a, gdn_s_new = _gdn(pad_t(a_cols), gdn_buf, gdn_s, p['gdn_conv_w'], p['gdn_gate'], p['gdn_norm'], t_valid)
    o_c, rw_s_new = _rwkv(pad_t(c_cols), rw_prev.reshape(b, 1, C_COLS), rw_s, p['rw_mu'], p['rw_vecs'], p['rw_w2'],
                          p['rw_a2'], p['rw_g2'], t_valid)
    o_a, o_c = o_a[:, 0:t], o_c[:, 0:t]
    x = _mix_mem(x, o_a, o_b, o_c, p['w_out'], p['norm_mem'], p['mem_wq'], mem_k, mem_v, p['mem_wo'],
                 mem_bb, tile_t, layer if mem_k.ndim == 5 else None)
    x, tail = _ffn(x, ffn_buf, p['norm_ffn'], p['ffn_wg'], p['ffn_wv'], p['ffn_conv_w'], p['ffn_conv_b'],
                   p['ffn_wo'], p['norm_final'], ffn_bb, tile_t, final_norm)
    keep = min(t_valid, GDN_CONV - 1)
    gdn_buf_new = jnp.concatenate([gdn_buf[:, keep:], a_cols[:, t_valid - keep:t_valid, 0:3 * W_A]], axis=1)
    last = t_valid - (t - SUBLANES)
    ffn_buf_new = tail[:, last - (FFN_CONV - 1):last]
    return (x, k_new, v_new, gdn_buf_new, gdn_s_new, c_cols[:, t_valid - 1], rw_s_new, ffn_buf_new)


def kernel(x_prompt, x_sample, cache_diff_k, cache_diff_v, state_gdn_conv, state_gdn, state_rwkv_shift, state_rwkv, state_ffn_conv, cache_mem_k, cache_mem_v, page_table, mem_prompt, norm_mix, w_in, gdn_conv_w, gdn_a_log, gdn_dt_bias, gdn_norm, diff_lq1, diff_lk1, diff_lq2, diff_lk2, diff_subln, rw_mu, rw_w0, rw_w2, rw_a0, rw_a2, rw_g2, rw_k_k, rw_k_a, rw_r_k, rw_lnx_w, rw_lnx_b, w_out, norm_mem, norm_mem_kv, mem_wq, mem_wk, mem_wv, mem_wo, norm_ffn, ffn_w_in, ffn_conv_w, ffn_conv_b, ffn_w_out, norm_final):
    w = dict(norm_mix=norm_mix, w_in=w_in, gdn_conv_w=gdn_conv_w, gdn_a_log=gdn_a_log, gdn_dt_bias=gdn_dt_bias,
             gdn_norm=gdn_norm, diff_lq1=diff_lq1, diff_lk1=diff_lk1, diff_lq2=diff_lq2, diff_lk2=diff_lk2,
             diff_subln=diff_subln, rw_mu=rw_mu, rw_w0=rw_w0, rw_w2=rw_w2, rw_a0=rw_a0, rw_a2=rw_a2, rw_g2=rw_g2,
             rw_k_k=rw_k_k, rw_k_a=rw_k_a, rw_r_k=rw_r_k, rw_lnx_w=rw_lnx_w, rw_lnx_b=rw_lnx_b, w_out=w_out,
             norm_mem=norm_mem, norm_mem_kv=norm_mem_kv, mem_wq=mem_wq, mem_wk=mem_wk, mem_wv=mem_wv,
             mem_wo=mem_wo, norm_ffn=norm_ffn, ffn_w_in=ffn_w_in, ffn_conv_w=ffn_conv_w, ffn_conv_b=ffn_conv_b,
             ffn_w_out=ffn_w_out, norm_final=norm_final)
    depth = w_in.shape[0]
    bp, tp, _ = x_prompt.shape
    bs, ts, _ = x_sample.shape
    mt = mem_prompt.shape[1]
    ts_pad = -(-ts // SUBLANES) * SUBLANES
    n_pool, page = cache_diff_k.shape[1], cache_diff_k.shape[2]
    pool_view = lambda z: jnp.transpose(z, (0, 1, 3, 4, 2)).reshape(depth, n_pool, W_B, page)
    kt_pool = pool_view(cache_diff_k)
    vt_pool = pool_view(cache_diff_v)

    xp = x_prompt
    xs = jnp.pad(x_sample, ((0, 0), (0, ts_pad - ts), (0, 0)))
    prompt_tile = _pick_tile(tp, ROW_TILE)
    outs_p = [[] for _ in range(9)]
    prompt_kv = ()
    outs_s = [[] for _ in range(7)]
    for l in range(depth):
        lam_init = 0.8 - 0.6 * math.exp(-0.3 * l)
        last = l == depth - 1
        p = _layer_weights(l, w)
        mk, mv = _mem_project(mem_prompt.reshape(bp * mt, D_MODEL), p['norm_mem_kv'], p['mem_wk'], p['mem_wv'])
        mk = mk.reshape(bp, mt, MEM_INNER)
        mv = mv.reshape(bp, mt, MEM_INNER)
        attn_p = functools.partial(_dattn_prompt, lam_par=p['lam_par'], subln=p['subln'], lam_init=lam_init)
        res = _trunk_layer(
            xp, tp, mk, mv,
            jnp.zeros((bp, GDN_CONV - 1, 3 * W_A), F32), jnp.zeros((bp, H_A, HEAD_DIM, HEAD_DIM), F32),
            jnp.zeros((bp, C_COLS), F32), jnp.zeros((bp, H_C, HEAD_DIM, HEAD_DIM), F32),
            jnp.zeros((bp, FFN_CONV - 1, D_FF), F32), attn_p, p, w_in, l, 1, 1, 1, prompt_tile, last, True,
            prompt_kv)
        xp = res[0]
        prompt_kv = (res[1], res[2])
        for acc, val in zip(outs_p, res[1:] + (mk, mv)):
            acc.append(val)
        attn_s = functools.partial(_dattn_paged, kt_pool=kt_pool, vt_pool=vt_pool, layer=l, page_table=page_table,
                                   lam_par=p['lam_par'], subln=p['subln'], lam_init=lam_init, t_valid=ts)
        res = _trunk_layer(
            xs, ts, cache_mem_k, cache_mem_v,
            state_gdn_conv[l], state_gdn[l], state_rwkv_shift[l], state_rwkv[l], state_ffn_conv[l], attn_s, p, w_in, l,
            bs, min(GROUP_B, bs), bs, ts_pad, last, False)
        xs = res[0]
        for acc, val in zip(outs_s, res[1:]):
            acc.append(val)

    st = lambda vals, shape: jnp.stack(vals).reshape((depth,) + shape)
    p_k, p_v, p_gc, p_gs, p_rp, p_rs, p_fc, p_mk, p_mv = outs_p
    s_k, s_v, s_gc, s_gs, s_rp, s_rs, s_fc = outs_s
    kv_p = lambda rows: jnp.transpose(rows.reshape(depth, bp, H_B, 2 * DB, tp), (0, 1, 4, 2, 3))
    kv_s = (bs, ts, H_B, 2 * DB)
    mem_shape = (bp, mt, MEM_HEADS, MEM_HD)
    return (xp, xs[:, 0:ts],
            kv_p(prompt_kv[0]), kv_p(prompt_kv[1]), jnp.stack(p_gc), jnp.stack(p_gs), jnp.stack(p_rp), jnp.stack(p_rs),
            jnp.stack(p_fc), st(p_mk, mem_shape), st(p_mv, mem_shape),
            st(s_k, kv_s), st(s_v, kv_s), jnp.stack(s_gc), jnp.stack(s_gs), jnp.stack(s_rp), jnp.stack(s_rs),
            jnp.stack(s_fc))
```
